```python
import math
import jax, jax.numpy as jnp
from jax import lax
import numpy as np

D_MODEL = 1024
BATCH = 4
SEQ = 4096
DEPTH = 2
DEC_BATCH = 8
DEC_SEQ = 4096
PAST_LEN = 128

FNET_GROUP_DIM = 64
FNET_GROUPS = 6
FNET_W = FNET_GROUPS * FNET_GROUP_DIM
S5_GROUP_DIM = 16
S5_GROUPS = 24
S5_W = S5_GROUPS * S5_GROUP_DIM
S5_STATE = 64
S5_DT_MIN = 0.001
S5_DT_MAX = 0.1
MLA_HEADS = 16
QK_NOPE = 64
QK_ROPE = 32
V_DIM = 64
Q_LORA = 384
KV_LORA = 256
ROPE_BASE = 10000.0
Q_BLOCK = 128
N_BRANCH = 3
D_FF = 4 * D_MODEL
EPS = 1e-6

OFF_FNET = 0
OFF_S5 = OFF_FNET + FNET_W
OFF_Q = OFF_S5 + S5_W
OFF_KV = OFF_Q + Q_LORA
OFF_KR = OFF_KV + KV_LORA
OFF_GATE = OFF_KR + QK_ROPE
IN_W = OFF_GATE + N_BRANCH * D_MODEL

kernel_name = "hybrid_fnet_s5_mla_encoder"


def rms_norm(x, g):
    x32 = x.astype(jnp.float32)
    y = x32 * lax.rsqrt(jnp.mean(x32 * x32, axis=-1, keepdims=True) + EPS)
    return (y * g.astype(jnp.float32)).astype(x.dtype)


def fourier_mix(u):
    b, l, _ = u.shape
    ug = u.astype(jnp.float32).reshape(b, l, FNET_GROUPS, FNET_GROUP_DIM)
    f = jnp.fft.fftn(ug, axes=(1, 3), norm="ortho")
    return jnp.real(f).reshape(b, l, FNET_W).astype(u.dtype)


def s5_scan(u, lam_re, lam_im, log_dt, b_re, b_im, c_re, c_im, reverse):
    f32 = jnp.float32
    lam = lax.complex(lam_re.astype(f32), lam_im.astype(f32))
    dt = jnp.exp(log_dt.astype(f32))[:, None]
    lam_bar = jnp.exp(lam * dt)
    b_bar = ((lam_bar - 1.0) / lam)[..., None] * lax.complex(b_re.astype(f32), b_im.astype(f32))
    bu = jnp.einsum('blgp,gnp->blgn', u.astype(jnp.complex64), b_bar)
    a = jnp.broadcast_to(lam_bar, bu.shape)

    def combine(e1, e2):
        a1, h1 = e1
        a2, h2 = e2
        return a1 * a2, a2 * h1 + h2

    _, h = lax.associative_scan(combine, (a, bu), axis=1, reverse=reverse)
    cmat = lax.complex(c_re.astype(f32), c_im.astype(f32))
    return jnp.real(jnp.einsum('blgn,gpn->blgp', h, cmat))


def s5_branch(u, lam_re, lam_im, log_dt, b_re, b_im, c_re, c_im, d_skip, w_glu):
    bsz, l, _ = u.shape
    u32 = u.astype(jnp.float32)
    ug = u32.reshape(bsz, l, S5_GROUPS, S5_GROUP_DIM)
    y_f = s5_scan(ug, lam_re[0], lam_im[0], log_dt[0], b_re[0], b_im[0], c_re[0], c_im[0], False)
    y_b = s5_scan(ug, lam_re[1], lam_im[1], log_dt[1], b_re[1], b_im[1], c_re[1], c_im[1], True)
    y = (y_f + y_b).reshape(bsz, l, S5_W) + d_skip.astype(jnp.float32) * u32
    y = jax.nn.gelu(y).astype(u.dtype)
    h = y @ w_glu
    return h[..., :S5_W] * jax.nn.sigmoid(h[..., S5_W:])


def apply_rope(x):
    l = x.shape[1]
    half = QK_ROPE // 2
    inv = ROPE_BASE ** (-jnp.arange(half, dtype=jnp.float32) / half)
    ang = jnp.arange(l, dtype=jnp.float32)[:, None] * inv[None, :]
    cos = jnp.cos(ang)[None, :, None, :]
    sin = jnp.sin(ang)[None, :, None, :]
    x32 = x.astype(jnp.float32)
    x1, x2 = x32[..., :half], x32[..., half:]
    return jnp.concatenate([x1 * cos - x2 * sin, x1 * sin + x2 * cos], axis=-1).astype(x.dtype)


def mla_branch(c_q, c_kv, k_rope, g_q, w_qb, g_kv, w_kvb, w_o):
    b, l, _ = c_q.shape
    q = (rms_norm(c_q, g_q) @ w_qb).reshape(b, l, MLA_HEADS, QK_NOPE + QK_ROPE)
    q = jnp.concatenate([q[..., :QK_NOPE], apply_rope(q[..., QK_NOPE:])], axis=-1)
    kv = (rms_norm(c_kv, g_kv) @ w_kvb).reshape(b, l, MLA_HEADS, QK_NOPE + V_DIM)
    k_pe = jnp.broadcast_to(apply_rope(k_rope[:, :, None, :]), (b, l, MLA_HEADS, QK_ROPE))
    k32 = jnp.concatenate([kv[..., :QK_NOPE], k_pe], axis=-1).astype(jnp.float32)
    v32 = kv[..., QK_NOPE:].astype(jnp.float32)
    scale = (QK_NOPE + QK_ROPE) ** -0.5
    q_blocks = q.reshape(b, l // Q_BLOCK, Q_BLOCK, MLA_HEADS, QK_NOPE + QK_ROPE).transpose(1, 0, 2, 3, 4)

    def attend(q_blk):
        s = jnp.einsum('bqhd,bkhd->bhqk', q_blk.astype(jnp.float32), k32) * scale
        p = jax.nn.softmax(s, axis=-1)
        return jnp.einsum('bhqk,bkhd->bqhd', p, v32)

    o = lax.map(attend, q_blocks)
    o = o.transpose(1, 0, 2, 3, 4).reshape(b, l, MLA_HEADS * V_DIM).astype(c_q.dtype)
    return o @ w_o


def encoder_layer(x, g_mix, w_in, w_fnet, s5_lam_re, s5_lam_im, s5_log_dt, s5_b_re, s5_b_im,
                  s5_c_re, s5_c_im, s5_d, w_glu, w_s5, g_q, w_qb, g_kv, w_kvb, w_o_mla,
                  w_out, g_mlp, w_up, w_down):
    b, l, _ = x.shape
    h = rms_norm(x, g_mix)
    z = h @ w_in
    y_a = fourier_mix(z[..., OFF_FNET:OFF_S5]) @ w_fnet
    y_b = s5_branch(z[..., OFF_S5:OFF_Q], s5_lam_re, s5_lam_im, s5_log_dt, s5_b_re, s5_b_im,
                    s5_c_re, s5_c_im, s5_d, w_glu) @ w_s5
    y_c = mla_branch(z[..., OFF_Q:OFF_KV], z[..., OFF_KV:OFF_KR], z[..., OFF_KR:OFF_GATE],
                     g_q, w_qb, g_kv, w_kvb, w_o_mla)
    gates = jax.nn.sigmoid(z[..., OFF_GATE:].astype(jnp.float32)).reshape(b, l, N_BRANCH, D_MODEL)
    merged = (gates[:, :, 0] * y_a.astype(jnp.float32)
              + gates[:, :, 1] * y_b.astype(jnp.float32)
              + gates[:, :, 2] * y_c.astype(jnp.float32)).astype(x.dtype)
    x = x + merged @ w_out
    h = rms_norm(x, g_mlp)
    x = x + jnp.square(jax.nn.relu(h @ w_up)) @ w_down
    return x


def trunk(x, g_mix, w_in, w_fnet, s5_lam_re, s5_lam_im, s5_log_dt, s5_b_re, s5_b_im,
          s5_c_re, s5_c_im, s5_d, w_glu, w_s5, g_q, w_qb, g_kv, w_kvb, w_o_mla,
          w_out, g_mlp, w_up, w_down, g_final):
    for i in range(DEPTH):
        x = encoder_layer(x, g_mix[i], w_in[i], w_fnet[i], s5_lam_re[i], s5_lam_im[i], s5_log_dt[i],
                          s5_b_re[i], s5_b_im[i], s5_c_re[i], s5_c_im[i], s5_d[i], w_glu[i], w_s5[i],
                          g_q[i], w_qb[i], g_kv[i], w_kvb[i], w_o_mla[i], w_out[i], g_mlp[i],
                          w_up[i], w_down[i])
    return rms_norm(x, g_final)


def setup_inputs(seed: int = 0) -> dict:
    key = jax.random.key(seed)
    ks = jax.random.split(key, 32)
    f32 = jnp.float32

    def nrm(k, shape, scale):
        return jax.random.normal(k, shape, f32) * scale

    def gain(k, shape):
        return 1.0 + 0.02 * jax.random.normal(k, shape, f32)

    G, N, P = S5_GROUPS, S5_STATE, S5_GROUP_DIM
    n_idx = jnp.arange(N, dtype=f32)
    inp = {
        "x_prompt": nrm(ks[0], (BATCH, SEQ, D_MODEL), 1.0),
        "x_sample": nrm(ks[1], (DEC_BATCH, DEC_SEQ, D_MODEL), 1.0),
        "g_mix": gain(ks[2], (DEPTH, D_MODEL)),
        "w_in": nrm(ks[3], (DEPTH, D_MODEL, IN_W), D_MODEL ** -0.5),
        "w_fnet": nrm(ks[4], (DEPTH, FNET_W, D_MODEL), FNET_W ** -0.5),
        "s5_lam_re": -0.5 + 0.01 * jax.random.normal(ks[5], (DEPTH, 2, G, N), f32),
        "s5_lam_im": math.pi * n_idx + 0.01 * jax.random.normal(ks[6], (DEPTH, 2, G, N), f32),
        "s5_log_dt": jax.random.uniform(ks[7], (DEPTH, 2, G), f32, math.log(S5_DT_MIN), math.log(S5_DT_MAX)),
        "s5_b_re": nrm(ks[8], (DEPTH, 2, G, N, P), (2.0 * P) ** -0.5),
        "s5_b_im": nrm(ks[9], (DEPTH, 2, G, N, P), (2.0 * P) ** -0.5),
        "s5_c_re": nrm(ks[10], (DEPTH, 2, G, P, N), (2.0 * N) ** -0.5),
        "s5_c_im": nrm(ks[11], (DEPTH, 2, G, P, N), (2.0 * N) ** -0.5),
        "s5_d": nrm(ks[12], (DEPTH, S5_W), 1.0),
        "w_glu": nrm(ks[13], (DEPTH, S5_W, 2 * S5_W), S5_W ** -0.5),
        "w_s5": nrm(ks[14], (DEPTH, S5_W, D_MODEL), S5_W ** -0.5),
        "g_q": gain(ks[15], (DEPTH, Q_LORA)),
        "w_qb": nrm(ks[16], (DEPTH, Q_LORA, MLA_HEADS * (QK_NOPE + QK_ROPE)), Q_LORA ** -0.5),
        "g_kv": gain(ks[17], (DEPTH, KV_LORA)),
        "w_kvb": nrm(ks[18], (DEPTH, KV_LORA, MLA_HEADS * (QK_NOPE + V_DIM)), KV_LORA ** -0.5),
        "w_o_mla": nrm(ks[19], (DEPTH, MLA_HEADS * V_DIM, D_MODEL), (MLA_HEADS * V_DIM) ** -0.5),
        "w_out": nrm(ks[20], (DEPTH, D_MODEL, D_MODEL), D_MODEL ** -0.5),
        "g_mlp": gain(ks[21], (DEPTH, D_MODEL)),
        "w_up": nrm(ks[22], (DEPTH, D_MODEL, D_FF), D_MODEL ** -0.5),
        "w_down": nrm(ks[23], (DEPTH, D_FF, D_MODEL), D_FF ** -0.5),
        "g_final": gain(ks[24], (D_MODEL,)),
    }
    return inp


def reference(x_prompt, x_sample, g_mix, w_in, w_fnet, s5_lam_re, s5_lam_im, s5_log_dt, s5_b_re,
              s5_b_im, s5_c_re, s5_c_im, s5_d, w_glu, w_s5, g_q, w_qb, g_kv, w_kvb, w_o_mla,
              w_out, g_mlp, w_up, w_down, g_final):
    params = (g_mix, w_in, w_fnet, s5_lam_re, s5_lam_im, s5_log_dt, s5_b_re, s5_b_im, s5_c_re,
              s5_c_im, s5_d, w_glu, w_s5, g_q, w_qb, g_kv, w_kvb, w_o_mla, w_out, g_mlp,
              w_up, w_down, g_final)
    y_prompt = trunk(x_prompt, *params)
    y_sample = trunk(x_sample, *params)
    return (y_prompt, y_sample)
```

```python
import functools
import math

import jax
import jax.numpy as jnp
from jax import lax
from jax.experimental import pallas as pl
from jax.experimental.pallas import tpu as pltpu

F32 = jnp.float32
BF16 = jnp.bfloat16

D_MODEL = 1024
DEPTH = 2
FNET_GROUP_DIM = 64
FNET_GROUPS = 6
FNET_W = 384
S5_GROUP_DIM = 16
S5_GROUPS = 24
S5_W = 384
S5_STATE = 64
MLA_HEADS = 16
QK_NOPE = 64
QK_ROPE = 32
V_DIM = 64
Q_LORA = 384
KV_LORA = 256
ROPE_BASE = 10000.0
N_BRANCH = 3
D_FF = 4 * D_MODEL
EPS = 1e-6

OFF_FNET = 0
OFF_S5 = OFF_FNET + FNET_W
OFF_Q = OFF_S5 + S5_W
OFF_KV = OFF_Q + Q_LORA
OFF_KR = OFF_KV + KV_LORA
OFF_GATE = OFF_KR + QK_ROPE

LANES = 128
HEAD_PAD = 128
S5_CHUNK = 32
S5_BATCH_PAD = 8
VMEM_LIMIT = 56 * 1024 * 1024

ZA_CQ = OFF_Q
ZA_CKV = OFF_KV
ZA_KRA = OFF_KR
ZA_KRB = OFF_KR + LANES
ZA_W = ZA_KRB + LANES


def _rms(x, g):
    return x * lax.rsqrt(jnp.mean(x * x, axis=-1, keepdims=True) + EPS) * g


def _dot(a, b):
    return jnp.dot(a, b, preferred_element_type=F32)


def _params(*sem):
    return pltpu.CompilerParams(dimension_semantics=sem, vmem_limit_bytes=VMEM_LIMIT)


def _const_spec(shape):
    nd = len(shape)
    return pl.BlockSpec(shape, lambda *_: (0,) * nd)


def _inproj_kernel(x_ref, gmix_ref, wa_ref, cs_ref, gq_ref, wqa_ref, wqb_ref, gkv_ref, wk_ref,
                   wv_ref, taq_ref, tbq_ref, tak_ref, tbk_ref,
                   ucs_ref, us_ref, q_ref, k_ref, v_ref):
    h = _rms(x_ref[...], gmix_ref[...]).astype(BF16)
    z = _dot(h, wa_ref[...])
    ucs_ref[...] = _dot(z[:, OFF_FNET:OFF_S5].astype(BF16), cs_ref[...]).astype(BF16)
    us_ref[...] = z[:, OFF_S5:OFF_Q].astype(BF16)
    hq = _rms(z[:, ZA_CQ:ZA_CQ + Q_LORA], gq_ref[...]).astype(BF16)
    hkv = _rms(z[:, ZA_CKV:ZA_CKV + KV_LORA], gkv_ref[...]).astype(BF16)
    kpe = z[:, ZA_KRA:ZA_KRA + LANES] * tak_ref[...] + z[:, ZA_KRB:ZA_KRB + LANES] * tbk_ref[...]
    kpe2 = jnp.concatenate([kpe, kpe], axis=1)
    taq2 = jnp.concatenate([taq_ref[...]] * 2, axis=1)
    tbq2 = jnp.concatenate([tbq_ref[...]] * 2, axis=1)
    for j in range(MLA_HEADS // 2):
        sl = slice(2 * HEAD_PAD * j, 2 * HEAD_PAD * (j + 1))
        qa = _dot(hq, wqa_ref[:, sl])
        qb = _dot(hq, wqb_ref[:, sl])
        q_ref[:, sl] = (qa * taq2 + qb * tbq2).astype(BF16)
        k_ref[:, sl] = (_dot(hkv, wk_ref[:, sl]) + kpe2).astype(BF16)
    v_ref[...] = _dot(hkv, wv_ref[...]).astype(BF16)


def _inproj(x, lw, tabs, seq):
    n = x.shape[0]
    tm = min(512, seq)
    nt = seq // tm
    row = lambda w: pl.BlockSpec((tm, w), lambda i: (i, 0))
    tab = pl.BlockSpec((tm, LANES), lambda i: (i % nt, 0))
    hw = MLA_HEADS * HEAD_PAD
    return pl.pallas_call(
        _inproj_kernel,
        grid=(n // tm,),
        in_specs=[row(D_MODEL), _const_spec((1, D_MODEL)), _const_spec((D_MODEL, ZA_W)),
                  _const_spec((FNET_W, 2 * FNET_W)), _const_spec((1, Q_LORA)),
                  _const_spec((Q_LORA, hw)), _const_spec((Q_LORA, hw)), _const_spec((1, KV_LORA)),
                  _const_spec((KV_LORA, hw)), _const_spec((KV_LORA, MLA_HEADS * V_DIM)),
                  tab, tab, tab, tab],
        out_specs=[row(2 * FNET_W), row(S5_W), row(hw), row(hw), row(MLA_HEADS * V_DIM)],
        out_shape=[jax.ShapeDtypeStruct((n, 2 * FNET_W), BF16), jax.ShapeDtypeStruct((n, S5_W), BF16),
                   jax.ShapeDtypeStruct((n, hw), BF16), jax.ShapeDtypeStruct((n, hw), BF16),
                   jax.ShapeDtypeStruct((n, MLA_HEADS * V_DIM), BF16)],
        compiler_params=_params("parallel"),
    )(x, lw["g_mix"], lw["w_a"], tabs["cs"], lw["g_q"], lw["wq_a"], lw["wq_b"], lw["g_kv"],
      lw["wk"], lw["wv"], tabs["taq"], tabs["tbq"], tabs["tak"], tabs["tbk"])


def _fnet_kernel(c_ref, s_ref, ucs_ref, out_ref, acc_ref, *, nk):
    k = pl.program_id(1)
    b = pl.program_id(2)
    u = ucs_ref[0]
    part = _dot(c_ref[...], u[:, :FNET_W]) - _dot(s_ref[...], u[:, FNET_W:])

    @pl.when(k == 0)
    def _():
        acc_ref[b] = part

    @pl.when(k > 0)
    def _():
        acc_ref[b] += part

    @pl.when(k == nk - 1)
    def _():
        out_ref[b] = acc_ref[b].astype(BF16)


def _fnet(ucs, tabs, bsz, seq):
    tm = min(1024, seq)
    tk = min(1024, seq)
    nk = seq // tk
    return pl.pallas_call(
        functools.partial(_fnet_kernel, nk=nk),
        grid=(seq // tm, nk, bsz),
        in_specs=[pl.BlockSpec((tm, tk), lambda m, k, b: (m, k)),
                  pl.BlockSpec((tm, tk), lambda m, k, b: (m, k)),
                  pl.BlockSpec((1, tk, 2 * FNET_W), lambda m, k, b: (b, k, 0))],
        out_specs=pl.BlockSpec((bsz, tm, FNET_W), lambda m, k, b: (0, m, 0)),
        out_shape=jax.ShapeDtypeStruct((bsz, seq, FNET_W), BF16),
        scratch_shapes=[pltpu.VMEM((bsz, tm, FNET_W), F32)],
        compiler_params=_params("parallel", "arbitrary", "arbitrary"),
    )(tabs["dft_c"], tabs["dft_s"], ucs.reshape(bsz, seq, 2 * FNET_W))


def _s5_kernel(u_ref, m_ref, bre_ref, bim_ref, are_ref, aim_ref, cre_ref, cim_ref, y_ref,
               lre_ref, lim_ref, pre_ref, pim_ref, *, nc):
    u = u_ref[0]
    lre_ref[...] = _dot(u, bre_ref[0])
    lim_ref[...] = _dot(u, bim_ref[0])
    are = are_ref[0]
    aim = aim_ref[0]
    fwd_lane = lax.broadcasted_iota(jnp.int32, (S5_BATCH_PAD, LANES), 1) < S5_STATE
    zero = jnp.zeros((S5_BATCH_PAD, LANES), F32)

    def rows(c):
        return pl.ds(pl.multiple_of(c * S5_BATCH_PAD, S5_BATCH_PAD), S5_BATCH_PAD)

    def fwd(c, carry):
        sr, si = carry
        r = rows(c)
        pre_ref[r, :] = sr
        pim_ref[r, :] = si
        return (are * sr - aim * si + lre_ref[r, :], are * si + aim * sr + lim_ref[r, :])

    lax.fori_loop(0, nc, fwd, (zero, zero))

    def bwd(i, carry):
        sr, si = carry
        r = rows(nc - 1 - i)
        pre_ref[r, :] = jnp.where(fwd_lane, pre_ref[r, :], sr)
        pim_ref[r, :] = jnp.where(fwd_lane, pim_ref[r, :], si)
        return (are * sr - aim * si + lre_ref[r, :], are * si + aim * sr + lim_ref[r, :])

    lax.fori_loop(0, nc, bwd, (zero, zero))
    y = (_dot(u, m_ref[0]) + _dot(pre_ref[...].astype(BF16), cre_ref[0])
         + _dot(pim_ref[...].astype(BF16), cim_ref[0]))
    y_ref[0] = y.astype(BF16)


def _s5(us, lw, bsz, seq):
    t, p, g = S5_CHUNK, S5_GROUP_DIM, S5_GROUPS
    nc = seq // t
    rows = nc * S5_BATCH_PAD
    u = us.reshape(bsz, nc, t, g, p).transpose(3, 1, 0, 2, 4)
    u = jnp.pad(u, ((0, 0), (0, 0), (0, S5_BATCH_PAD - bsz), (0, 0), (0, 0)))
    u = u.reshape(g, rows, t * p)
    blk = lambda a, b: pl.BlockSpec((1, a, b), lambda i: (i, 0, 0))
    y = pl.pallas_call(
        functools.partial(_s5_kernel, nc=nc),
        grid=(g,),
        in_specs=[blk(rows, t * p), blk(t * p, t * p), blk(t * p, LANES), blk(t * p, LANES),
                  blk(S5_BATCH_PAD, LANES), blk(S5_BATCH_PAD, LANES), blk(LANES, t * p),
                  blk(LANES, t * p)],
        out_specs=blk(rows, t * p),
        out_shape=jax.ShapeDtypeStruct((g, rows, t * p), BF16),
        scratch_shapes=[pltpu.VMEM((rows, LANES), F32)] * 4,
        compiler_params=_params("parallel"),
    )(u, lw["s5_m"], lw["s5_bre"], lw["s5_bim"], lw["s5_are"], lw["s5_aim"], lw["s5_cre"],
      lw["s5_cim"])
    y = y.reshape(g, nc, S5_BATCH_PAD, t, p)[:, :, :bsz]
    return y.transpose(2, 1, 3, 0, 4).reshape(bsz * seq, g * p)


def _attn_kernel(q_ref, k_ref, v_ref, o_ref):
    v = v_ref[0]
    outs = []
    for j in range(2):
        sl = slice(HEAD_PAD * j, HEAD_PAD * (j + 1))
        s = lax.dot_general(q_ref[0, :, sl], k_ref[0, :, sl], (((1,), (1,)), ((), ())),
                            preferred_element_type=F32)
        m = jnp.max(s, axis=-1, keepdims=True)
        p = jnp.exp(s - m)
        l = jnp.sum(p, axis=-1, keepdims=True)
        outs.append(_dot(p.astype(BF16), v) / l)
    lane = lax.broadcasted_iota(jnp.int32, outs[0].shape, 1)
    o_ref[0] = jnp.where(lane < V_DIM, outs[0], outs[1]).astype(BF16)


def _attention(q, k, v, bsz, seq):
    tq = min(256, seq)
    hw = MLA_HEADS * HEAD_PAD
    return pl.pallas_call(
        _attn_kernel,
        grid=(bsz, MLA_HEADS // 2, seq // tq),
        in_specs=[pl.BlockSpec((1, tq, 2 * HEAD_PAD), lambda b, h, i: (b, i, h)),
                  pl.BlockSpec((1, seq, 2 * HEAD_PAD), lambda b, h, i: (b, 0, h)),
                  pl.BlockSpec((1, seq, 2 * V_DIM), lambda b, h, i: (b, 0, h))],
        out_specs=pl.BlockSpec((1, tq, 2 * V_DIM), lambda b, h, i: (b, i, h)),
        out_shape=jax.ShapeDtypeStruct((bsz, seq, MLA_HEADS * V_DIM), BF16),
        compiler_params=_params("parallel", "parallel", "arbitrary"),
    )(q.reshape(bsz, seq, hw), k.reshape(bsz, seq, hw), v.reshape(bsz, seq, MLA_HEADS * V_DIM))


def _merge_kernel(x_ref, yf_ref, ys_ref, o_ref, gmix_ref, wfn_ref, wglu_ref, ws5_ref, wo_ref,
                  wg_ref, wout_ref, out_ref):
    x = x_ref[...]
    h = _rms(x, gmix_ref[...]).astype(BF16)
    y_a = _dot(yf_ref[...], wfn_ref[...])
    s = jax.nn.gelu(ys_ref[...].astype(F32)).astype(BF16)
    hg = _dot(s, wglu_ref[...])
    glu = (hg[:, :S5_W] * jax.nn.sigmoid(hg[:, S5_W:])).astype(BF16)
    y_b = _dot(glu, ws5_ref[...])
    y_c = _dot(o_ref[...], wo_ref[...])
    merged = jax.nn.sigmoid(_dot(h, wg_ref[:, :D_MODEL])) * y_a
    merged += jax.nn.sigmoid(_dot(h, wg_ref[:, D_MODEL:2 * D_MODEL])) * y_b
    merged += jax.nn.sigmoid(_dot(h, wg_ref[:, 2 * D_MODEL:])) * y_c
    out_ref[...] = x + _dot(merged.astype(BF16), wout_ref[...])


def _merge(x, yf, ys, o, lw):
    n = x.shape[0]
    tm = min(512, n)
    row = lambda w: pl.BlockSpec((tm, w), lambda i: (i, 0))
    return pl.pallas_call(
        _merge_kernel,
        grid=(n // tm,),
        in_specs=[row(D_MODEL), row(FNET_W), row(S5_W), row(MLA_HEADS * V_DIM),
                  _const_spec((1, D_MODEL)), _const_spec((FNET_W, D_MODEL)),
                  _const_spec((S5_W, 2 * S5_W)), _const_spec((S5_W, D_MODEL)),
                  _const_spec((MLA_HEADS * V_DIM, D_MODEL)),
                  _const_spec((D_MODEL, N_BRANCH * D_MODEL)), _const_spec((D_MODEL, D_MODEL))],
        out_specs=row(D_MODEL),
        out_shape=jax.ShapeDtypeStruct((n, D_MODEL), F32),
        compiler_params=_params("parallel"),
    )(x, yf, ys, o, lw["g_mix"], lw["w_fnet"], lw["w_glu"], lw["w_s5"], lw["w_o"], lw["w_gate"],
      lw["w_out"])


def _mlp_kernel(x_ref, g_ref, wup_ref, wdn_ref, gfin_ref, out_ref, h_ref, acc_ref, *, nf, final):
    j = pl.program_id(1)

    @pl.when(j == 0)
    def _():
        h_ref[...] = _rms(x_ref[...], g_ref[...]).astype(BF16)

    a = jnp.square(jnp.maximum(_dot(h_ref[...], wup_ref[...]), 0.0)).astype(BF16)
    part = _dot(a, wdn_ref[...])

    @pl.when(j == 0)
    def _():
        acc_ref[...] = part

    @pl.when(j > 0)
    def _():
        acc_ref[...] += part

    @pl.when(j == nf - 1)
    def _():
        y = x_ref[...] + acc_ref[...]
        if final:
            y = _rms(y, gfin_ref[...])
        out_ref[...] = y


def _mlp(x, lw, g_final, final):
    n = x.shape[0]
    tm = min(1024, n)
    tf = 1024
    nf = D_FF // tf
    return pl.pallas_call(
        functools.partial(_mlp_kernel, nf=nf, final=final),
        grid=(n // tm, nf),
        in_specs=[pl.BlockSpec((tm, D_MODEL), lambda i, j: (i, 0)),
                  pl.BlockSpec((1, D_MODEL), lambda i, j: (0, 0)),
                  pl.BlockSpec((D_MODEL, tf), lambda i, j: (0, j)),
                  pl.BlockSpec((tf, D_MODEL), lambda i, j: (j, 0)),
                  pl.BlockSpec((1, D_MODEL), lambda i, j: (0, 0))],
        out_specs=pl.BlockSpec((tm, D_MODEL), lambda i, j: (i, 0)),
        out_shape=jax.ShapeDtypeStruct((n, D_MODEL), F32),
        scratch_shapes=[pltpu.VMEM((tm, D_MODEL), BF16), pltpu.VMEM((tm, D_MODEL), F32)],
        compiler_params=_params("parallel", "arbitrary"),
    )(x, lw["g_mlp"], lw["w_up"], lw["w_down"], g_final)


def _tables(seq):
    half = QK_ROPE // 2
    inv = ROPE_BASE ** (-jnp.arange(half, dtype=F32) / half)
    ang = jnp.arange(seq, dtype=F32)[:, None] * inv[None, :]
    cos, sin = jnp.cos(ang), jnp.sin(ang)
    one = jnp.ones((seq, QK_NOPE), F32)
    z64 = jnp.zeros((seq, QK_NOPE), F32)
    z32 = jnp.zeros((seq, HEAD_PAD - QK_NOPE - QK_ROPE), F32)
    scale = (QK_NOPE + QK_ROPE) ** -0.5
    ta = jnp.concatenate([z64, cos, cos, z32], axis=1)
    tb = jnp.concatenate([z64, -sin, sin, z32], axis=1)
    taq = jnp.concatenate([one, cos, cos, z32], axis=1) * scale
    c = jnp.arange(FNET_GROUP_DIM)
    ang64 = (2.0 * math.pi / FNET_GROUP_DIM) * ((c[:, None] * c[None, :]) % FNET_GROUP_DIM).astype(F32)
    eye = jnp.eye(FNET_GROUPS, dtype=F32)
    norm = 1.0 / math.sqrt(seq * FNET_GROUP_DIM)
    cs = jnp.concatenate([jnp.kron(eye, jnp.cos(ang64)), jnp.kron(eye, jnp.sin(ang64))], axis=1) * norm
    n = jnp.arange(seq)
    angl = (2.0 * math.pi / seq) * ((n[:, None] * n[None, :]) % seq).astype(F32)
    return dict(taq=taq, tbq=tb * scale, tak=ta, tbk=tb, cs=cs.astype(BF16),
                dft_c=jnp.cos(angl).astype(BF16), dft_s=jnp.sin(angl).astype(BF16))


def _pad_heads(w, width):
    k = w.shape[0]
    w = w.reshape(k, MLA_HEADS, width)
    return jnp.pad(w, ((0, 0), (0, 0), (0, HEAD_PAD - width))).reshape(k, MLA_HEADS * HEAD_PAD)


def _s5_tables(lam_re, lam_im, log_dt, b_re, b_im, c_re, c_im, d_skip):
    t, p, g, ns = S5_CHUNK, S5_GROUP_DIM, S5_GROUPS, S5_STATE
    lam = lax.complex(lam_re, lam_im)
    ldt = lam * jnp.exp(log_dt)[..., None]
    lam_bar = jnp.exp(ldt)
    b_bar = ((lam_bar - 1.0) / lam)[..., None] * lax.complex(b_re, b_im)
    cm = lax.complex(c_re, c_im)
    d = jnp.arange(t + 1, dtype=F32)
    pw = jnp.exp(ldt[..., None] * d)
    kern = jnp.real(jnp.einsum('xgpn,xgnd,xgnq->xgdpq', cm, pw[..., :t], b_bar))
    s = jnp.arange(t)[:, None]
    tt = jnp.arange(t)[None, :]
    diff = tt - s
    kf = kern[0][:, jnp.clip(diff, 0, t - 1)]
    kb = kern[1][:, jnp.clip(-diff, 0, t - 1)]
    mask = lambda c: c[None, :, :, None, None]
    skip = jnp.eye(p, dtype=F32)[None] * d_skip.reshape(g, p)[:, :, None]
    m = (jnp.where(mask(diff >= 0), kf, 0.0) + jnp.where(mask(diff <= 0), kb, 0.0)
         + jnp.where(mask(diff == 0), skip[:, None, None], 0.0))
    m = m.transpose(0, 1, 4, 2, 3).reshape(g, t * p, t * p)
    bf = pw[0][..., ::-1][..., 1:][:, :, :, None] * b_bar[0][:, :, None, :]
    bb = pw[1][..., :t][:, :, :, None] * b_bar[1][:, :, None, :]
    bst = jnp.concatenate([bf, bb], axis=1).transpose(0, 2, 3, 1).reshape(g, t * p, 2 * ns)
    cf = cm[0].transpose(0, 2, 1)[:, :, None, :] * pw[0][..., 1:][..., None]
    cb = cm[1].transpose(0, 2, 1)[:, :, None, :] * pw[1][..., ::-1][..., :t][..., None]
    cst = jnp.concatenate([cf, cb], axis=1).reshape(g, 2 * ns, t * p)
    a = jnp.concatenate([pw[0][..., t], pw[1][..., t]], axis=1)
    tile = lambda v: jnp.broadcast_to(v[:, None, :], (g, S5_BATCH_PAD, 2 * ns)).astype(F32)
    return dict(s5_m=m.astype(BF16), s5_bre=jnp.real(bst).astype(BF16),
                s5_bim=jnp.imag(bst).astype(BF16), s5_cre=jnp.real(cst).astype(BF16),
                s5_cim=(-jnp.imag(cst)).astype(BF16), s5_are=tile(jnp.real(a)),
                s5_aim=tile(jnp.imag(a)))


def _layer_weights(i, p):
    w_in = p["w_in"][i]
    k_dim = w_in.shape[0]
    half = QK_ROPE // 2
    kr = w_in[:, OFF_KR:OFF_GATE]
    z64 = jnp.zeros((k_dim, QK_NOPE), F32)
    z32 = jnp.zeros((k_dim, HEAD_PAD - QK_NOPE - QK_ROPE), F32)
    w_a = jnp.concatenate([w_in[:, :OFF_KR], z64, kr, z32, z64, kr[:, half:], kr[:, :half], z32], axis=1)
    wq = p["w_qb"][i].reshape(Q_LORA, MLA_HEADS, QK_NOPE + QK_ROPE)
    rope = wq[:, :, QK_NOPE:]
    wq_b = jnp.concatenate([jnp.zeros_like(wq[:, :, :QK_NOPE]), rope[:, :, half:], rope[:, :, :half]], axis=2)
    wkv = p["w_kvb"][i].reshape(KV_LORA, MLA_HEADS, QK_NOPE + V_DIM)
    row = lambda v: v.reshape(1, -1).astype(F32)
    lw = dict(
        g_mix=row(p["g_mix"][i]), w_a=w_a.astype(BF16), w_gate=w_in[:, OFF_GATE:].astype(BF16),
        g_q=row(p["g_q"][i]), g_kv=row(p["g_kv"][i]),
        wq_a=_pad_heads(p["w_qb"][i], QK_NOPE + QK_ROPE).astype(BF16),
        wq_b=_pad_heads(wq_b.reshape(Q_LORA, -1), QK_NOPE + QK_ROPE).astype(BF16),
        wk=_pad_heads(wkv[:, :, :QK_NOPE].reshape(KV_LORA, -1), QK_NOPE).astype(BF16),
        wv=wkv[:, :, QK_NOPE:].reshape(KV_LORA, -1).astype(BF16),
        w_fnet=p["w_fnet"][i].astype(BF16), w_glu=p["w_glu"][i].astype(BF16),
        w_s5=p["w_s5"][i].astype(BF16), w_o=p["w_o_mla"][i].astype(BF16),
        w_out=p["w_out"][i].astype(BF16), g_mlp=row(p["g_mlp"][i]),
        w_up=p["w_up"][i].astype(BF16), w_down=p["w_down"][i].astype(BF16))
    lw.update(_s5_tables(p["s5_lam_re"][i], p["s5_lam_im"][i], p["s5_log_dt"][i], p["s5_b_re"][i],
                         p["s5_b_im"][i], p["s5_c_re"][i], p["s5_c_im"][i], p["s5_d"][i]))
    return lw


def _trunk(x, layers, tabs, g_final):
    bsz, seq, _ = x.shape
    x = x.reshape(bsz * seq, D_MODEL)
    for i, lw in enumerate(layers):
        ucs, us, q, k, v = _inproj(x, lw, tabs, seq)
        yf = _fnet(ucs, tabs, bsz, seq).reshape(bsz * seq, FNET_W)
        ys = _s5(us, lw, bsz, seq)
        o = _attention(q, k, v, bsz, seq).reshape(bsz * seq, MLA_HEADS * V_DIM)
        x = _merge(x, yf, ys, o, lw)
        x = _mlp(x, lw, g_final, final=(i == len(layers) - 1))
    return x.reshape(bsz, seq, D_MODEL)


def kernel(x_prompt, x_sample, g_mix, w_in, w_fnet, s5_lam_re, s5_lam_im, s5_log_dt, s5_b_re,
           s5_b_im, s5_c_re, s5_c_im, s5_d, w_glu, w_s5, g_q, w_qb, g_kv, w_kvb, w_o_mla,
           w_out, g_mlp, w_up, w_down, g_final):
    p = dict(g_mix=g_mix, w_in=w_in, w_fnet=w_fnet, s5_lam_re=s5_lam_re, s5_lam_im=s5_lam_im,
             s5_log_dt=s5_log_dt, s5_b_re=s5_b_re, s5_b_im=s5_b_im, s5_c_re=s5_c_re,
             s5_c_im=s5_c_im, s5_d=s5_d, w_glu=w_glu, w_s5=w_s5, g_q=g_q, w_qb=w_qb, g_kv=g_kv,
             w_kvb=w_kvb, w_o_mla=w_o_mla, w_out=w_out, g_mlp=g_mlp, w_up=w_up, w_down=w_down)
    layers = [_layer_weights(i, p) for i in range(g_mix.shape[0])]
    gfin = g_final.reshape(1, -1).astype(F32)
    outs = []
    for x in (x_prompt, x_sample):
        tabs = _tables(x.shape[1])
        outs.append(_trunk(x, layers, tabs, gfin))
    return tuple(outs)
```

```python
import functools
import math

import jax
import jax.numpy as jnp
from jax import lax
from jax.experimental import pallas as pl
from jax.experimental.pallas import tpu as pltpu

F32 = jnp.float32
BF16 = jnp.bfloat16

D_MODEL = 1024
FNET_GROUP_DIM = 64
FNET_GROUPS = 6
FNET_W = 384
S5_GROUP_DIM = 16
S5_GROUPS = 24
S5_W = 384
S5_STATE = 64
MLA_HEADS = 16
QK_NOPE = 64
QK_ROPE = 32
V_DIM = 64
Q_LORA = 384
KV_LORA = 256
ROPE_BASE = 10000.0
N_BRANCH = 3
D_FF = 4 * D_MODEL
EPS = 1e-6

OFF_FNET = 0
OFF_S5 = OFF_FNET + FNET_W
OFF_Q = OFF_S5 + S5_W
OFF_KV = OFF_Q + Q_LORA
OFF_KR = OFF_KV + KV_LORA
OFF_GATE = OFF_KR + QK_ROPE

LANES = 128
SUBLANES = 8
HEAD_PAD = 128
VMEM_LIMIT = 56 * 1024 * 1024

S5_CHUNK = SUBLANES
S5_TILES = S5_W // LANES
S5_TILE_GROUPS = LANES // S5_GROUP_DIM
S5_HALF = S5_TILE_GROUPS * S5_STATE
S5_DIR = 2 * S5_HALF
S5_BATCH = SUBLANES
S5_BLOCK_CHUNKS = 64

ZA_CQ = OFF_Q
ZA_CKV = OFF_KV
ZA_KRA = OFF_KR
ZA_KRB = OFF_KR + LANES
ZA_W = ZA_KRB + LANES

ATTN_ROWS = 1024
ATTN_SUB = 256


def _rms(x, g):
    return x * lax.rsqrt(jnp.mean(x * x, axis=-1, keepdims=True) + EPS) * g


def _dot(a, b):
    return jnp.dot(a, b, preferred_element_type=F32)


def _params(*sem):
    return pltpu.CompilerParams(dimension_semantics=sem, vmem_limit_bytes=VMEM_LIMIT)


def _const_spec(shape):
    nd = len(shape)
    return pl.BlockSpec(shape, lambda *_: (0,) * nd)


def _inproj_kernel(x_ref, gmix_ref, wa_ref, cs_ref, gq_ref, wqa_ref, wqb_ref, gkv_ref, wk_ref,
                   wv_ref, taq_ref, tbq_ref, tak_ref, tbk_ref, *rest):
    ucs_ref, u5_ref, q_ref, k_ref, v_ref = rest[-5:]
    h = _rms(x_ref[...], gmix_ref[...]).astype(BF16)
    z = _dot(h, wa_ref[...])
    ucs_ref[...] = _dot(z[:, OFF_FNET:OFF_S5].astype(BF16), cs_ref[...]).astype(BF16)
    for t in range(S5_TILES):
        for c in range(z.shape[0] // S5_CHUNK):
            u5_ref[t, c, 0] = z[c * S5_CHUNK:(c + 1) * S5_CHUNK, OFF_S5 + t * LANES:OFF_S5 + (t + 1) * LANES]
    hq = _rms(z[:, ZA_CQ:ZA_CQ + Q_LORA], gq_ref[...]).astype(BF16)
    hkv = _rms(z[:, ZA_CKV:ZA_CKV + KV_LORA], gkv_ref[...]).astype(BF16)
    kpe = z[:, ZA_KRA:ZA_KRA + LANES] * tak_ref[...] + z[:, ZA_KRB:ZA_KRB + LANES] * tbk_ref[...]
    kpe2 = jnp.concatenate([kpe, kpe], axis=1)
    taq2 = jnp.concatenate([taq_ref[...]] * 2, axis=1)
    tbq2 = jnp.concatenate([tbq_ref[...]] * 2, axis=1)
    for j in range(MLA_HEADS // 2):
        sl = slice(2 * HEAD_PAD * j, 2 * HEAD_PAD * (j + 1))
        qa = _dot(hq, wqa_ref[:, sl])
        qb = _dot(hq, wqb_ref[:, sl])
        q_ref[:, sl] = (qa * taq2 + qb * tbq2).astype(BF16)
        k_ref[:, sl] = (_dot(hkv, wk_ref[:, sl]) + kpe2).astype(BF16)
    v_ref[...] = _dot(hkv, wv_ref[...]).astype(BF16)


def _inproj(x, lw, tabs, bsz, seq):
    n = x.shape[0]
    tm = min(512, seq)
    nt = seq // tm
    nc = seq // S5_CHUNK
    row = lambda w: pl.BlockSpec((tm, w), lambda i: (i, 0))
    tab = pl.BlockSpec((tm, LANES), lambda i: (i % nt, 0))
    hw = MLA_HEADS * HEAD_PAD
    u5_shape = (S5_TILES, nc, S5_BATCH, S5_CHUNK, LANES)
    u5_spec = pl.BlockSpec((S5_TILES, tm // S5_CHUNK, 1, S5_CHUNK, LANES),
                           lambda i: (0, i % nt, i // nt, 0, 0))
    in_specs = [row(D_MODEL), _const_spec((1, D_MODEL)), _const_spec((D_MODEL, ZA_W)),
                _const_spec((FNET_W, 2 * FNET_W)), _const_spec((1, Q_LORA)),
                _const_spec((Q_LORA, hw)), _const_spec((Q_LORA, hw)), _const_spec((1, KV_LORA)),
                _const_spec((KV_LORA, hw)), _const_spec((KV_LORA, MLA_HEADS * V_DIM)),
                tab, tab, tab, tab]
    args = [x, lw["g_mix"], lw["w_a"], tabs["cs"], lw["g_q"], lw["wq_a"], lw["wq_b"], lw["g_kv"],
            lw["wk"], lw["wv"], tabs["taq"], tabs["tbq"], tabs["tak"], tabs["tbk"]]
    aliases = {}
    if bsz < S5_BATCH:
        aliases = {len(args): 1}
        in_specs.append(pl.BlockSpec(memory_space=pl.ANY))
        args.append(jnp.zeros(u5_shape, F32))
    return pl.pallas_call(
        _inproj_kernel,
        grid=(n // tm,),
        in_specs=in_specs,
        out_specs=[row(2 * FNET_W), u5_spec, row(hw), row(hw), row(MLA_HEADS * V_DIM)],
        out_shape=[jax.ShapeDtypeStruct((n, 2 * FNET_W), BF16), jax.ShapeDtypeStruct(u5_shape, F32),
                   jax.ShapeDtypeStruct((n, hw), BF16), jax.ShapeDtypeStruct((n, hw), BF16),
                   jax.ShapeDtypeStruct((n, MLA_HEADS * V_DIM), BF16)],
        input_output_aliases=aliases,
        compiler_params=_params("parallel"),
        name="inproj",
    )(*args)


def _fnet_kernel(c_ref, s_ref, ucs_ref, out_ref, acc_ref, *, nk):
    k = pl.program_id(1)
    b = pl.program_id(2)
    u = ucs_ref[0]
    part = _dot(c_ref[...], u[:, :FNET_W]) - _dot(s_ref[...], u[:, FNET_W:])

    @pl.when(k == 0)
    def _():
        acc_ref[b] = part

    @pl.when(k > 0)
    def _():
        acc_ref[b] += part

    @pl.when(k == nk - 1)
    def _():
        out_ref[b] = acc_ref[b].astype(BF16)


def _fnet(ucs, tabs, bsz, seq):
    tm = min(1024, seq)
    tk = min(1024, seq)
    nk = seq // tk
    return pl.pallas_call(
        functools.partial(_fnet_kernel, nk=nk),
        grid=(seq // tm, nk, bsz),
        in_specs=[pl.BlockSpec((tm, tk), lambda m, k, b: (m, k)),
                  pl.BlockSpec((tm, tk), lambda m, k, b: (m, k)),
                  pl.BlockSpec((1, tk, 2 * FNET_W), lambda m, k, b: (b, k, 0))],
        out_specs=pl.BlockSpec((bsz, tm, FNET_W), lambda m, k, b: (0, m, 0)),
        out_shape=jax.ShapeDtypeStruct((bsz, seq, FNET_W), BF16),
        scratch_shapes=[pltpu.VMEM((bsz, tm, FNET_W), F32)],
        compiler_params=_params("parallel", "arbitrary", "arbitrary"),
        name="fnet",
    )(tabs["dft_c"], tabs["dft_s"], ucs.reshape(bsz, seq, 2 * FNET_W))


def _s5_rows(u_ref, rows):
    return jnp.concatenate([u_ref[0, pl.ds(s, rows, stride=S5_CHUNK), :] for s in range(S5_CHUNK)],
                           axis=1).astype(BF16)


def _s5_in_kernel(u_ref, bst_ref, loc_ref, *, rows):
    loc_ref[0] = _dot(_s5_rows(u_ref, rows), bst_ref[0])


def _s5_scan_kernel(loc_ref, a_ref, sp_ref, cre_ref, cim_ref, *, cb):
    d = pl.program_id(1)

    @pl.when(pl.program_id(2) == 0)
    def _():
        cre_ref[...] = jnp.zeros_like(cre_ref)
        cim_ref[...] = jnp.zeros_like(cim_ref)

    are = a_ref[0, :, :S5_HALF]
    aim = a_ref[0, :, S5_HALF:]

    def step(i, carry):
        sr, si = carry
        c = jnp.where(d == 0, i, cb - 1 - i)
        r = pl.ds(pl.multiple_of(c * S5_BATCH, S5_BATCH), S5_BATCH)
        sp_ref[0, r, :S5_HALF] = sr
        sp_ref[0, r, S5_HALF:] = si
        return (are * sr - aim * si + loc_ref[0, r, :S5_HALF],
                are * si + aim * sr + loc_ref[0, r, S5_HALF:])

    sr, si = lax.fori_loop(0, cb, step, (cre_ref[...], cim_ref[...]))
    cre_ref[...] = sr
    cim_ref[...] = si


def _s5_out_kernel(u_ref, m_ref, sp_ref, cst_ref, y_ref, *, rows):
    y = _dot(_s5_rows(u_ref, rows), m_ref[0]) + _dot(sp_ref[0].astype(BF16), cst_ref[0])
    for s in range(S5_CHUNK):
        y_ref[0, pl.ds(s, rows, stride=S5_CHUNK), :] = y[:, s * LANES:(s + 1) * LANES]


def _s5(u5, lw, seq):
    nc = seq // S5_CHUNK
    cb = min(S5_BLOCK_CHUNKS, nc)
    nblk = nc // cb
    rows = cb * S5_BATCH
    kw = S5_CHUNK * LANES
    u2 = u5.reshape(S5_TILES, nc * S5_BATCH * S5_CHUNK, LANES)
    u_spec = pl.BlockSpec((1, rows * S5_CHUNK, LANES), lambda t, j: (t, j, 0))
    wspec = lambda a, b: pl.BlockSpec((1, a, b), lambda t, j: (t, 0, 0))
    loc = pl.pallas_call(
        functools.partial(_s5_in_kernel, rows=rows),
        grid=(S5_TILES, nblk),
        in_specs=[u_spec, wspec(kw, 2 * S5_DIR)],
        out_specs=pl.BlockSpec((1, rows, 2 * S5_DIR), lambda t, j: (t, j, 0)),
        out_shape=jax.ShapeDtypeStruct((S5_TILES, nc * S5_BATCH, 2 * S5_DIR), F32),
        compiler_params=_params("parallel", "parallel"),
        name="s5_in",
    )(u2, lw["s5_bst"])
    blk = lambda t, d, j: (t, j + d * (nblk - 1 - 2 * j), d)
    sp = pl.pallas_call(
        functools.partial(_s5_scan_kernel, cb=cb),
        grid=(S5_TILES, 2, nblk),
        in_specs=[pl.BlockSpec((1, rows, S5_DIR), blk),
                  pl.BlockSpec((1, S5_BATCH, S5_DIR), lambda t, d, j: (t, 0, d))],
        out_specs=pl.BlockSpec((1, rows, S5_DIR), blk),
        out_shape=jax.ShapeDtypeStruct((S5_TILES, nc * S5_BATCH, 2 * S5_DIR), F32),
        scratch_shapes=[pltpu.VMEM((S5_BATCH, S5_HALF), F32)] * 2,
        compiler_params=_params("parallel", "arbitrary", "arbitrary"),
        name="s5_scan",
    )(loc, lw["s5_a"])
    y2 = pl.pallas_call(
        functools.partial(_s5_out_kernel, rows=rows),
        grid=(S5_TILES, nblk),
        in_specs=[u_spec, wspec(kw, kw), pl.BlockSpec((1, rows, 2 * S5_DIR), lambda t, j: (t, j, 0)),
                  wspec(2 * S5_DIR, kw)],
        out_specs=u_spec,
        out_shape=jax.ShapeDtypeStruct(u2.shape, F32),
        compiler_params=_params("parallel", "parallel"),
        name="s5_out",
    )(u2, lw["s5_m"], sp, lw["s5_cst"])
    return y2.reshape(u5.shape)


def _attn_kernel(q_ref, k_ref, v_ref, o_ref, *, sub):
    v = v_ref[0]
    for u in range(q_ref.shape[1] // sub):
        rows = slice(u * sub, (u + 1) * sub)
        outs = []
        for j in range(2):
            sl = slice(HEAD_PAD * j, HEAD_PAD * (j + 1))
            s = lax.dot_general(q_ref[0, rows, sl], k_ref[0, :, sl], (((1,), (1,)), ((), ())),
                                preferred_element_type=F32)
            m = jnp.max(s, axis=-1, keepdims=True)
            p = jnp.exp2(s - m)
            l = jnp.sum(p, axis=-1, keepdims=True)
            outs.append(_dot(p.astype(BF16), v) / l)
        lane = lax.broadcasted_iota(jnp.int32, outs[0].shape, 1)
        o_ref[0, rows, :] = jnp.where(lane < V_DIM, outs[0], outs[1]).astype(BF16)


def _attention(q, k, v, bsz, seq):
    tq = min(ATTN_ROWS, seq)
    hw = MLA_HEADS * HEAD_PAD
    return pl.pallas_call(
        functools.partial(_attn_kernel, sub=min(ATTN_SUB, tq)),
        grid=(bsz, MLA_HEADS // 2, seq // tq),
        in_specs=[pl.BlockSpec((1, tq, 2 * HEAD_PAD), lambda b, h, i: (b, i, h)),
                  pl.BlockSpec((1, seq, 2 * HEAD_PAD), lambda b, h, i: (b, 0, h)),
                  pl.BlockSpec((1, seq, 2 * V_DIM), lambda b, h, i: (b, 0, h))],
        out_specs=pl.BlockSpec((1, tq, 2 * V_DIM), lambda b, h, i: (b, i, h)),
        out_shape=jax.ShapeDtypeStruct((bsz, seq, MLA_HEADS * V_DIM), BF16),
        compiler_params=_params("parallel", "parallel", "arbitrary"),
        name="attention",
    )(q.reshape(bsz, seq, hw), k.reshape(bsz, seq, hw), v.reshape(bsz, seq, MLA_HEADS * V_DIM))


def _merge_kernel(x_ref, yf_ref, y5_ref, o_ref, gmix_ref, wfn_ref, wglu_ref, ws5_ref, wo_ref,
                  wg_ref, wout_ref, out_ref):
    x = x_ref[...]
    h = _rms(x, gmix_ref[...]).astype(BF16)
    y_a = _dot(yf_ref[...], wfn_ref[...])
    nchunk = y5_ref.shape[1]
    ys = jnp.concatenate(
        [jnp.concatenate([y5_ref[t, c, 0] for c in range(nchunk)], axis=0) for t in range(S5_TILES)],
        axis=1)
    s = jax.nn.gelu(ys).astype(BF16)
    hg = _dot(s, wglu_ref[...])
    glu = (hg[:, :S5_W] * jax.nn.sigmoid(hg[:, S5_W:])).astype(BF16)
    y_b = _dot(glu, ws5_ref[...])
    y_c = _dot(o_ref[...], wo_ref[...])
    merged = jax.nn.sigmoid(_dot(h, wg_ref[:, :D_MODEL])) * y_a
    merged += jax.nn.sigmoid(_dot(h, wg_ref[:, D_MODEL:2 * D_MODEL])) * y_b
    merged += jax.nn.sigmoid(_dot(h, wg_ref[:, 2 * D_MODEL:])) * y_c
    out_ref[...] = x + _dot(merged.astype(BF16), wout_ref[...])


def _merge(x, yf, y5, o, lw, seq):
    n = x.shape[0]
    tm = min(512, seq)
    nt = seq // tm
    row = lambda w: pl.BlockSpec((tm, w), lambda i: (i, 0))
    y5_spec = pl.BlockSpec((S5_TILES, tm // S5_CHUNK, 1, S5_CHUNK, LANES),
                           lambda i: (0, i % nt, i // nt, 0, 0))
    return pl.pallas_call(
        _merge_kernel,
        grid=(n // tm,),
        in_specs=[row(D_MODEL), row(FNET_W), y5_spec, row(MLA_HEADS * V_DIM),
                  _const_spec((1, D_MODEL)), _const_spec((FNET_W, D_MODEL)),
                  _const_spec((S5_W, 2 * S5_W)), _const_spec((S5_W, D_MODEL)),
                  _const_spec((MLA_HEADS * V_DIM, D_MODEL)),
                  _const_spec((D_MODEL, N_BRANCH * D_MODEL)), _const_spec((D_MODEL, D_MODEL))],
        out_specs=row(D_MODEL),
        out_shape=jax.ShapeDtypeStruct((n, D_MODEL), F32),
        compiler_params=_params("parallel"),
        name="merge",
    )(x, yf, y5, o, lw["g_mix"], lw["w_fnet"], lw["w_glu"], lw["w_s5"], lw["w_o"], lw["w_gate"],
      lw["w_out"])


def _mlp_kernel(x_ref, g_ref, wup_ref, wdn_ref, gfin_ref, out_ref, h_ref, acc_ref, *, nf, final):
    j = pl.program_id(1)

    @pl.when(j == 0)
    def _():
        h_ref[...] = _rms(x_ref[...], g_ref[...]).astype(BF16)

    a = jnp.square(jnp.maximum(_dot(h_ref[...], wup_ref[...]), 0.0)).astype(BF16)
    part = _dot(a, wdn_ref[...])

    @pl.when(j == 0)
    def _():
        acc_ref[...] = part

    @pl.when(j > 0)
    def _():
        acc_ref[...] += part

    @pl.when(j == nf - 1)
    def _():
        y = x_ref[...] + acc_ref[...]
        if final:
            y = _rms(y, gfin_ref[...])
        out_ref[...] = y


def _mlp(x, lw, g_final, final):
    n = x.shape[0]
    tm = min(1024, n)
    tf = 1024
    nf = D_FF // tf
    return pl.pallas_call(
        functools.partial(_mlp_kernel, nf=nf, final=final),
        grid=(n // tm, nf),
        in_specs=[pl.BlockSpec((tm, D_MODEL), lambda i, j: (i, 0)),
                  pl.BlockSpec((1, D_MODEL), lambda i, j: (0, 0)),
                  pl.BlockSpec((D_MODEL, tf), lambda i, j: (0, j)),
                  pl.BlockSpec((tf, D_MODEL), lambda i, j: (j, 0)),
                  pl.BlockSpec((1, D_MODEL), lambda i, j: (0, 0))],
        out_specs=pl.BlockSpec((tm, D_MODEL), lambda i, j: (i, 0)),
        out_shape=jax.ShapeDtypeStruct((n, D_MODEL), F32),
        scratch_shapes=[pltpu.VMEM((tm, D_MODEL), BF16), pltpu.VMEM((tm, D_MODEL), F32)],
        compiler_params=_params("parallel", "arbitrary"),
        name="mlp",
    )(x, lw["g_mlp"], lw["w_up"], lw["w_down"], g_final)


def _tables(seq):
    half = QK_ROPE // 2
    inv = ROPE_BASE ** (-jnp.arange(half, dtype=F32) / half)
    ang = jnp.arange(seq, dtype=F32)[:, None] * inv[None, :]
    cos, sin = jnp.cos(ang), jnp.sin(ang)
    one = jnp.ones((seq, QK_NOPE), F32)
    z64 = jnp.zeros((seq, QK_NOPE), F32)
    z32 = jnp.zeros((seq, HEAD_PAD - QK_NOPE - QK_ROPE), F32)
    scale = (QK_NOPE + QK_ROPE) ** -0.5 * math.log2(math.e)
    ta = jnp.concatenate([z64, cos, cos, z32], axis=1)
    tb = jnp.concatenate([z64, -sin, sin, z32], axis=1)
    taq = jnp.concatenate([one, cos, cos, z32], axis=1) * scale
    c = jnp.arange(FNET_GROUP_DIM)
    ang64 = (2.0 * math.pi / FNET_GROUP_DIM) * ((c[:, None] * c[None, :]) % FNET_GROUP_DIM).astype(F32)
    eye = jnp.eye(FNET_GROUPS, dtype=F32)
    norm = 1.0 / math.sqrt(seq * FNET_GROUP_DIM)
    cs = jnp.concatenate([jnp.kron(eye, jnp.cos(ang64)), jnp.kron(eye, jnp.sin(ang64))], axis=1) * norm
    r = 64 if seq % 64 == 0 else 1
    k = jnp.arange(seq)[:, None]
    w = 2.0 * math.pi / seq
    a1 = w * ((k * (jnp.arange(seq // r)[None, :] * r)) % seq).astype(F32)
    a2 = w * ((k * jnp.arange(r)[None, :]) % seq).astype(F32)
    c1, s1 = jnp.cos(a1)[:, :, None], jnp.sin(a1)[:, :, None]
    c2, s2 = jnp.cos(a2)[:, None, :], jnp.sin(a2)[:, None, :]
    dft_c = (c1 * c2 - s1 * s2).reshape(seq, seq).astype(BF16)
    dft_s = (s1 * c2 + c1 * s2).reshape(seq, seq).astype(BF16)
    return dict(taq=taq, tbq=tb * scale, tak=ta, tbk=tb, cs=cs.astype(BF16), dft_c=dft_c, dft_s=dft_s)


def _pad_heads(w, width):
    k = w.shape[0]
    w = w.reshape(k, MLA_HEADS, width)
    return jnp.pad(w, ((0, 0), (0, 0), (0, HEAD_PAD - width))).reshape(k, MLA_HEADS * HEAD_PAD)


def _cmul(ar, ai, br, bi):
    return ar * br - ai * bi, ar * bi + ai * br


def _s5_tables(lam_re, lam_im, log_dt, b_re, b_im, c_re, c_im, d_skip):
    t, p, g, ns = S5_CHUNK, S5_GROUP_DIM, S5_GROUPS, S5_STATE
    dt = jnp.exp(log_dt)[..., None]
    ar, ai = lam_re * dt, lam_im * dt
    mag = jnp.exp(ar)
    lbr, lbi = mag * jnp.cos(ai), mag * jnp.sin(ai)
    den = lam_re * lam_re + lam_im * lam_im
    cfr = ((lbr - 1.0) * lam_re + lbi * lam_im) / den
    cfi = (lbi * lam_re - (lbr - 1.0) * lam_im) / den
    bbr, bbi = _cmul(cfr[..., None], cfi[..., None], b_re, b_im)
    d = jnp.arange(t + 1, dtype=F32)
    pmag = jnp.exp(ar[..., None] * d)
    pwr, pwi = pmag * jnp.cos(ai[..., None] * d), pmag * jnp.sin(ai[..., None] * d)
    cpr, cpi = _cmul(c_re[..., None], c_im[..., None], pwr[:, :, None], pwi[:, :, None])
    kern = (jnp.einsum('xgpnd,xgnq->xgdpq', cpr[..., :t], bbr)
            - jnp.einsum('xgpnd,xgnq->xgdpq', cpi[..., :t], bbi))
    s = jnp.arange(t)[:, None]
    tt = jnp.arange(t)[None, :]
    diff = tt - s
    kf = kern[0][:, jnp.clip(diff, 0, t - 1)]
    kb = kern[1][:, jnp.clip(-diff, 0, t - 1)]
    mask = lambda c: c[None, :, :, None, None]
    skip = jnp.eye(p, dtype=F32)[None] * d_skip.reshape(g, p)[:, :, None]
    mg = (jnp.where(mask(diff >= 0), kf, 0.0) + jnp.where(mask(diff <= 0), kb, 0.0)
          + jnp.where(mask(diff == 0), skip[:, None, None], 0.0))
    rev = lambda v: v[..., ::-1]
    sel_r = jnp.stack([rev(pwr[0])[..., 1:], pwr[1][..., :t]])
    sel_i = jnp.stack([rev(pwi[0])[..., 1:], pwi[1][..., :t]])
    bsr, bsi = _cmul(sel_r[..., None], sel_i[..., None], bbr[:, :, :, None, :], bbi[:, :, :, None, :])
    bg = jnp.stack([bsr, bsi], axis=1).transpose(0, 1, 2, 4, 5, 3)
    outr = jnp.stack([cpr[0][..., 1:], rev(cpr[1])[..., :t]])
    outi = jnp.stack([cpi[0][..., 1:], rev(cpi[1])[..., :t]])
    cg = jnp.stack([outr, -outi], axis=1).transpose(0, 1, 2, 4, 5, 3)
    eye = jnp.eye(S5_TILE_GROUPS, dtype=F32)
    tg = (S5_TILES, S5_TILE_GROUPS)
    kw = t * LANES
    m = jnp.einsum('Ggstpq,gh->Gsgqthp', mg.reshape(tg + mg.shape[1:]), eye).reshape(S5_TILES, kw, kw)
    bst = jnp.einsum('xrGgsqn,gh->Gsgqxrhn', bg.reshape((2, 2) + tg + bg.shape[3:]), eye)
    bst = bst.reshape(S5_TILES, kw, 2 * S5_DIR)
    cst = jnp.einsum('xrGgntp,gh->Gxrgnthp', cg.reshape((2, 2) + tg + cg.shape[3:]), eye)
    cst = cst.reshape(S5_TILES, 2 * S5_DIR, kw)
    a = jnp.stack([pwr[..., t], pwi[..., t]], axis=1)
    a = a.reshape((2, 2) + tg + (ns,)).transpose(2, 0, 1, 3, 4).reshape(S5_TILES, 1, 2 * S5_DIR)
    a = jnp.broadcast_to(a, (S5_TILES, S5_BATCH, 2 * S5_DIR))
    return dict(s5_m=m.astype(BF16), s5_bst=bst.astype(BF16), s5_cst=cst.astype(BF16), s5_a=a.astype(F32))


def _layer_weights(i, p):
    w_in = p["w_in"][i]
    k_dim = w_in.shape[0]
    half = QK_ROPE // 2
    kr = w_in[:, OFF_KR:OFF_GATE]
    z64 = jnp.zeros((k_dim, QK_NOPE), F32)
    z32 = jnp.zeros((k_dim, HEAD_PAD - QK_NOPE - QK_ROPE), F32)
    w_a = jnp.concatenate([w_in[:, :OFF_KR], z64, kr, z32, z64, kr[:, half:], kr[:, :half], z32], axis=1)
    wq = p["w_qb"][i].reshape(Q_LORA, MLA_HEADS, QK_NOPE + QK_ROPE)
    rope = wq[:, :, QK_NOPE:]
    wq_b = jnp.concatenate([jnp.zeros_like(wq[:, :, :QK_NOPE]), rope[:, :, half:], rope[:, :, :half]], axis=2)
    wkv = p["w_kvb"][i].reshape(KV_LORA, MLA_HEADS, QK_NOPE + V_DIM)
    row = lambda v: v.reshape(1, -1).astype(F32)
    lw = dict(
        g_mix=row(p["g_mix"][i]), w_a=w_a.astype(BF16), w_gate=w_in[:, OFF_GATE:].astype(BF16),
        g_q=row(p["g_q"][i]), g_kv=row(p["g_kv"][i]),
        wq_a=_pad_heads(p["w_qb"][i], QK_NOPE + QK_ROPE).astype(BF16),
        wq_b=_pad_heads(wq_b.reshape(Q_LORA, -1), QK_NOPE + QK_ROPE).astype(BF16),
        wk=_pad_heads(wkv[:, :, :QK_NOPE].reshape(KV_LORA, -1), QK_NOPE).astype(BF16),
        wv=wkv[:, :, QK_NOPE:].reshape(KV_LORA, -1).astype(BF16),
        w_fnet=p["w_fnet"][i].astype(BF16), w_glu=p["w_glu"][i].astype(BF16),
        w_s5=p["w_s5"][i].astype(BF16), w_o=p["w_o_mla"][i].astype(BF16),
        w_out=p["w_out"][i].astype(BF16), g_mlp=row(p["g_mlp"][i]),
        w_up=p["w_up"][i].astype(BF16), w_down=p["w_down"][i].astype(BF16))
    lw.update(_s5_tables(p["s5_lam_re"][i], p["s5_lam_im"][i], p["s5_log_dt"][i], p["s5_b_re"][i],
                         p["s5_b_im"][i], p["s5_c_re"][i], p["s5_c_im"][i], p["s5_d"][i]))
    return lw


def _trunk(x, layers, tabs, g_final):
    bsz, seq, _ = x.shape
    assert bsz <= S5_BATCH and seq % S5_CHUNK == 0
    x = x.reshape(bsz * seq, D_MODEL)
    for i, lw in enumerate(layers):
        ucs, u5, q, k, v = _inproj(x, lw, tabs, bsz, seq)
        yf = _fnet(ucs, tabs, bsz, seq).reshape(bsz * seq, FNET_W)
        y5 = _s5(u5, lw, seq)
        o = _attention(q, k, v, bsz, seq).reshape(bsz * seq, MLA_HEADS * V_DIM)
        x = _merge(x, yf, y5, o, lw, seq)
        x = _mlp(x, lw, g_final, final=(i == len(layers) - 1))
    return x.reshape(bsz, seq, D_MODEL)


def kernel(x_prompt, x_sample, g_mix, w_in, w_fnet, s5_lam_re, s5_lam_im, s5_log_dt, s5_b_re,
           s5_b_im, s5_c_re, s5_c_im, s5_d, w_glu, w_s5, g_q, w_qb, g_kv, w_kvb, w_o_mla,
           w_out, g_mlp, w_up, w_down, g_final):
    p = dict(g_mix=g_mix, w_in=w_in, w_fnet=w_fnet, s5_lam_re=s5_lam_re, s5_lam_im=s5_lam_im,
             s5_log_dt=s5_log_dt, s5_b_re=s5_b_re, s5_b_im=s5_b_im, s5_c_re=s5_c_re,
             s5_c_im=s5_c_im, s5_d=s5_d, w_glu=w_glu, w_s5=w_s5, g_q=g_q, w_qb=w_qb, g_kv=g_kv,
             w_kvb=w_kvb, w_o_mla=w_o_mla, w_out=w_out, g_mlp=g_mlp, w_up=w_up, w_down=w_down)
    layers = [_layer_weights(i, p) for i in range(g_mix.shape[0])]
    gfin = g_final.reshape(1, -1).astype(F32)
    outs = []
    for x in (x_prompt, x_sample):
        tabs = _tables(x.shape[1])
        outs.append(_trunk(x, layers, tabs, gfin))
    return tuple(outs)
```

```python
import functools
import math

import jax
import jax.numpy as jnp
from jax import lax
from jax.experimental import pallas as pl
from jax.experimental.pallas import tpu as pltpu

F32 = jnp.float32
BF16 = jnp.bfloat16

D_MODEL = 1024
FNET_GROUP_DIM = 64
FNET_GROUPS = 6
FNET_W = 384
S5_GROUP_DIM = 16
S5_GROUPS = 24
S5_W = 384
S5_STATE = 64
MLA_HEADS = 16
QK_NOPE = 64
QK_ROPE = 32
V_DIM = 64
Q_LORA = 384
KV_LORA = 256
ROPE_BASE = 10000.0
N_BRANCH = 3
D_FF = 4 * D_MODEL
EPS = 1e-6

OFF_FNET = 0
OFF_S5 = OFF_FNET + FNET_W
OFF_Q = OFF_S5 + S5_W
OFF_KV = OFF_Q + Q_LORA
OFF_KR = OFF_KV + KV_LORA
OFF_GATE = OFF_KR + QK_ROPE

LANES = 128
SUBLANES = 8
HEAD_PAD = 128
VMEM_LIMIT = 56 * 1024 * 1024

S5_CHUNK = SUBLANES
S5_TILES = S5_W // LANES
S5_TILE_GROUPS = LANES // S5_GROUP_DIM
S5_HALF = S5_TILE_GROUPS * S5_STATE
S5_DIR = 2 * S5_HALF
S5_BATCH = SUBLANES
S5_BLOCK_CHUNKS = 64

ZA_CQ = OFF_Q
ZA_CKV = OFF_KV
ZA_KRA = OFF_KR
ZA_KRB = OFF_KR + LANES
ZA_W = ZA_KRB + LANES

ATTN_ROWS = 1024
ATTN_SUB = 256


def _rms(x, g):
    return x * lax.rsqrt(jnp.mean(x * x, axis=-1, keepdims=True) + EPS) * g


def _dot(a, b):
    return jnp.dot(a, b, preferred_element_type=F32)


def _params(*sem):
    return pltpu.CompilerParams(dimension_semantics=sem, vmem_limit_bytes=VMEM_LIMIT)


def _const_spec(shape):
    nd = len(shape)
    return pl.BlockSpec(shape, lambda *_: (0,) * nd)


def _inproj_kernel(x_ref, gmix_ref, wa_ref, cs_ref, gq_ref, wqa_ref, wqb_ref, gkv_ref, wk_ref,
                   wv_ref, taq_ref, tbq_ref, tak_ref, tbk_ref, vone_ref, *rest):
    ucs_ref, u5_ref, q_ref, k_ref, v_ref = rest[-5:]
    h = _rms(x_ref[...], gmix_ref[...]).astype(BF16)
    z = _dot(h, wa_ref[...])
    ucs_ref[...] = _dot(z[:, OFF_FNET:OFF_S5].astype(BF16), cs_ref[...]).astype(BF16)
    for t in range(S5_TILES):
        for c in range(z.shape[0] // S5_CHUNK):
            u5_ref[t, c, 0] = z[c * S5_CHUNK:(c + 1) * S5_CHUNK, OFF_S5 + t * LANES:OFF_S5 + (t + 1) * LANES]
    hq = _rms(z[:, ZA_CQ:ZA_CQ + Q_LORA], gq_ref[...]).astype(BF16)
    hkv = _rms(z[:, ZA_CKV:ZA_CKV + KV_LORA], gkv_ref[...]).astype(BF16)
    kpe = z[:, ZA_KRA:ZA_KRA + LANES] * tak_ref[...] + z[:, ZA_KRB:ZA_KRB + LANES] * tbk_ref[...]
    kpe2 = jnp.concatenate([kpe, kpe], axis=1)
    taq2 = jnp.concatenate([taq_ref[...]] * 2, axis=1)
    tbq2 = jnp.concatenate([tbq_ref[...]] * 2, axis=1)
    for j in range(MLA_HEADS // 2):
        sl = slice(2 * HEAD_PAD * j, 2 * HEAD_PAD * (j + 1))
        qa = _dot(hq, wqa_ref[:, sl])
        qb = _dot(hq, wqb_ref[:, sl])
        q_ref[:, sl] = (qa * taq2 + qb * tbq2).astype(BF16)
        k_ref[:, sl] = (_dot(hkv, wk_ref[:, sl]) + kpe2).astype(BF16)
    v_ref[...] = (_dot(hkv, wv_ref[...]) + vone_ref[...]).astype(BF16)


def _inproj(x, lw, tabs, bsz, seq):
    n = x.shape[0]
    tm = min(512, seq)
    nt = seq // tm
    nc = seq // S5_CHUNK
    row = lambda w: pl.BlockSpec((tm, w), lambda i: (i, 0))
    tab = pl.BlockSpec((tm, LANES), lambda i: (i % nt, 0))
    hw = MLA_HEADS * HEAD_PAD
    u5_shape = (S5_TILES, nc, S5_BATCH, S5_CHUNK, LANES)
    u5_spec = pl.BlockSpec((S5_TILES, tm // S5_CHUNK, 1, S5_CHUNK, LANES),
                           lambda i: (0, i % nt, i // nt, 0, 0))
    in_specs = [row(D_MODEL), _const_spec((1, D_MODEL)), _const_spec((D_MODEL, ZA_W)),
                _const_spec((FNET_W, 2 * FNET_W)), _const_spec((1, Q_LORA)),
                _const_spec((Q_LORA, hw)), _const_spec((Q_LORA, hw)), _const_spec((1, KV_LORA)),
                _const_spec((KV_LORA, hw)), _const_spec((KV_LORA, hw)),
                tab, tab, tab, tab, _const_spec((1, hw))]
    args = [x, lw["g_mix"], lw["w_a"], tabs["cs"], lw["g_q"], lw["wq_a"], lw["wq_b"], lw["g_kv"],
            lw["wk"], lw["wv"], tabs["taq"], tabs["tbq"], tabs["tak"], tabs["tbk"], tabs["vone"]]
    aliases = {}
    if bsz < S5_BATCH:
        aliases = {len(args): 1}
        in_specs.append(pl.BlockSpec(memory_space=pl.ANY))
        args.append(jnp.zeros(u5_shape, F32))
    return pl.pallas_call(
        _inproj_kernel,
        grid=(n // tm,),
        in_specs=in_specs,
        out_specs=[row(2 * FNET_W), u5_spec, row(hw), row(hw), row(hw)],
        out_shape=[jax.ShapeDtypeStruct((n, 2 * FNET_W), BF16), jax.ShapeDtypeStruct(u5_shape, F32),
                   jax.ShapeDtypeStruct((n, hw), BF16), jax.ShapeDtypeStruct((n, hw), BF16),
                   jax.ShapeDtypeStruct((n, hw), BF16)],
        input_output_aliases=aliases,
        compiler_params=_params("parallel"),
        name="inproj",
    )(*args)


def _fnet_kernel(c_ref, s_ref, ucs_ref, out_ref, acc_ref, *, nk):
    k = pl.program_id(1)
    b = pl.program_id(2)
    u0, u1 = ucs_ref[0], ucs_ref[1]
    uc = jnp.concatenate([u0[:, :FNET_W], u1[:, :FNET_W]], axis=1)
    us = jnp.concatenate([u0[:, FNET_W:], u1[:, FNET_W:]], axis=1)
    part = _dot(c_ref[...], uc) - _dot(s_ref[...], us)

    @pl.when(k == 0)
    def _():
        acc_ref[b] = part

    @pl.when(k > 0)
    def _():
        acc_ref[b] += part

    @pl.when(k == nk - 1)
    def _():
        acc = acc_ref[b]
        out_ref[2 * b] = acc[:, :FNET_W].astype(BF16)
        out_ref[2 * b + 1] = acc[:, FNET_W:].astype(BF16)


def _fnet(ucs, tabs, bsz, seq):
    assert bsz % 2 == 0
    tm = min(1024, seq)
    tk = min(1024, seq)
    nk = seq // tk
    return pl.pallas_call(
        functools.partial(_fnet_kernel, nk=nk),
        grid=(seq // tm, nk, bsz // 2),
        in_specs=[pl.BlockSpec((tm, tk), lambda m, k, b: (m, k)),
                  pl.BlockSpec((tm, tk), lambda m, k, b: (m, k)),
                  pl.BlockSpec((2, tk, 2 * FNET_W), lambda m, k, b: (b, k, 0))],
        out_specs=pl.BlockSpec((bsz, tm, FNET_W), lambda m, k, b: (0, m, 0)),
        out_shape=jax.ShapeDtypeStruct((bsz, seq, FNET_W), BF16),
        scratch_shapes=[pltpu.VMEM((bsz // 2, tm, 2 * FNET_W), F32)],
        compiler_params=_params("parallel", "arbitrary", "arbitrary"),
        name="fnet",
    )(tabs["dft_c"], tabs["dft_s"], ucs.reshape(bsz, seq, 2 * FNET_W))


def _s5_rows(u_ref, rows):
    return jnp.concatenate([u_ref[0, pl.ds(s, rows, stride=S5_CHUNK), :] for s in range(S5_CHUNK)],
                           axis=1).astype(BF16)


def _s5_in_kernel(u_ref, bst_ref, loc_ref, *, rows):
    loc_ref[0] = _dot(_s5_rows(u_ref, rows), bst_ref[0])


def _s5_scan_kernel(loc_ref, a_ref, sp_ref, cre_ref, cim_ref, *, cb):
    d = pl.program_id(1)

    @pl.when(pl.program_id(2) == 0)
    def _():
        cre_ref[...] = jnp.zeros_like(cre_ref)
        cim_ref[...] = jnp.zeros_like(cim_ref)

    are = a_ref[0, :, :S5_HALF]
    aim = a_ref[0, :, S5_HALF:]

    def step(i, carry):
        sr, si = carry
        c = jnp.where(d == 0, i, cb - 1 - i)
        r = pl.ds(pl.multiple_of(c * S5_BATCH, S5_BATCH), S5_BATCH)
        sp_ref[0, r, :S5_HALF] = sr
        sp_ref[0, r, S5_HALF:] = si
        return (are * sr - aim * si + loc_ref[0, r, :S5_HALF],
                are * si + aim * sr + loc_ref[0, r, S5_HALF:])

    sr, si = lax.fori_loop(0, cb, step, (cre_ref[...], cim_ref[...]))
    cre_ref[...] = sr
    cim_ref[...] = si


def _s5_out_kernel(u_ref, m_ref, sp_ref, cst_ref, y_ref, *, rows):
    y = _dot(_s5_rows(u_ref, rows), m_ref[0]) + _dot(sp_ref[0].astype(BF16), cst_ref[0])
    for s in range(S5_CHUNK):
        y_ref[0, pl.ds(s, rows, stride=S5_CHUNK), :] = y[:, s * LANES:(s + 1) * LANES]


def _s5(u5, lw, seq):
    nc = seq // S5_CHUNK
    cb = min(S5_BLOCK_CHUNKS, nc)
    nblk = nc // cb
    rows = cb * S5_BATCH
    kw = S5_CHUNK * LANES
    u2 = u5.reshape(S5_TILES, nc * S5_BATCH * S5_CHUNK, LANES)
    u_spec = pl.BlockSpec((1, rows * S5_CHUNK, LANES), lambda t, j: (t, j, 0))
    wspec = lambda a, b: pl.BlockSpec((1, a, b), lambda t, j: (t, 0, 0))
    loc = pl.pallas_call(
        functools.partial(_s5_in_kernel, rows=rows),
        grid=(S5_TILES, nblk),
        in_specs=[u_spec, wspec(kw, 2 * S5_DIR)],
        out_specs=pl.BlockSpec((1, rows, 2 * S5_DIR), lambda t, j: (t, j, 0)),
        out_shape=jax.ShapeDtypeStruct((S5_TILES, nc * S5_BATCH, 2 * S5_DIR), F32),
        compiler_params=_params("parallel", "parallel"),
        name="s5_in",
    )(u2, lw["s5_bst"])
    blk = lambda t, d, j: (t, j + d * (nblk - 1 - 2 * j), d)
    sp = pl.pallas_call(
        functools.partial(_s5_scan_kernel, cb=cb),
        grid=(S5_TILES, 2, nblk),
        in_specs=[pl.BlockSpec((1, rows, S5_DIR), blk),
                  pl.BlockSpec((1, S5_BATCH, S5_DIR), lambda t, d, j: (t, 0, d))],
        out_specs=pl.BlockSpec((1, rows, S5_DIR), blk),
        out_shape=jax.ShapeDtypeStruct((S5_TILES, nc * S5_BATCH, 2 * S5_DIR), F32),
        scratch_shapes=[pltpu.VMEM((S5_BATCH, S5_HALF), F32)] * 2,
        compiler_params=_params("parallel", "arbitrary", "arbitrary"),
        name="s5_scan",
    )(loc, lw["s5_a"])
    y2 = pl.pallas_call(
        functools.partial(_s5_out_kernel, rows=rows),
        grid=(S5_TILES, nblk),
        in_specs=[u_spec, wspec(kw, kw), pl.BlockSpec((1, rows, 2 * S5_DIR), lambda t, j: (t, j, 0)),
                  wspec(2 * S5_DIR, kw)],
        out_specs=u_spec,
        out_shape=jax.ShapeDtypeStruct(u2.shape, F32),
        compiler_params=_params("parallel", "parallel"),
        name="s5_out",
    )(u2, lw["s5_m"], sp, lw["s5_cst"])
    return y2.reshape(u5.shape)


def _attn_kernel(q_ref, k_ref, v_ref, o_ref, *, sub):
    v = v_ref[0]
    for u in range(q_ref.shape[1] // sub):
        rows = slice(u * sub, (u + 1) * sub)
        outs = []
        for j in range(2):
            sl = slice(HEAD_PAD * j, HEAD_PAD * (j + 1))
            s = lax.dot_general(q_ref[0, rows, sl], k_ref[0, :, sl], (((1,), (1,)), ((), ())),
                                preferred_element_type=F32)
            m = jnp.max(s, axis=-1, keepdims=True)
            ol = _dot(jnp.exp2(s - m).astype(BF16), v)
            outs.append(ol[:, :2 * V_DIM] / ol[:, 2 * V_DIM:])
        lane = lax.broadcasted_iota(jnp.int32, outs[0].shape, 1)
        o_ref[0, rows, :] = jnp.where(lane < V_DIM, outs[0], outs[1]).astype(BF16)


def _attention(q, k, v, bsz, seq):
    tq = min(ATTN_ROWS, seq)
    hw = MLA_HEADS * HEAD_PAD
    return pl.pallas_call(
        functools.partial(_attn_kernel, sub=min(ATTN_SUB, tq)),
        grid=(bsz, MLA_HEADS // 2, seq // tq),
        in_specs=[pl.BlockSpec((1, tq, 2 * HEAD_PAD), lambda b, h, i: (b, i, h)),
                  pl.BlockSpec((1, seq, 2 * HEAD_PAD), lambda b, h, i: (b, 0, h)),
                  pl.BlockSpec((1, seq, 2 * HEAD_PAD), lambda b, h, i: (b, 0, h))],
        out_specs=pl.BlockSpec((1, tq, 2 * V_DIM), lambda b, h, i: (b, i, h)),
        out_shape=jax.ShapeDtypeStruct((bsz, seq, MLA_HEADS * V_DIM), BF16),
        compiler_params=_params("parallel", "parallel", "arbitrary"),
        name="attention",
    )(q.reshape(bsz, seq, hw), k.reshape(bsz, seq, hw), v.reshape(bsz, seq, hw))


def _merge_kernel(x_ref, yf_ref, y5_ref, o_ref, gmix_ref, wfn_ref, wglu_ref, ws5_ref, wo_ref,
                  wg_ref, wout_ref, out_ref):
    x = x_ref[...]
    h = _rms(x, gmix_ref[...]).astype(BF16)
    y_a = _dot(yf_ref[...], wfn_ref[...])
    nchunk = y5_ref.shape[1]
    ys = jnp.concatenate(
        [jnp.concatenate([y5_ref[t, c, 0] for c in range(nchunk)], axis=0) for t in range(S5_TILES)],
        axis=1)
    s = jax.nn.gelu(ys).astype(BF16)
    hg = _dot(s, wglu_ref[...])
    glu = (hg[:, :S5_W] * jax.nn.sigmoid(hg[:, S5_W:])).astype(BF16)
    y_b = _dot(glu, ws5_ref[...])
    y_c = _dot(o_ref[...], wo_ref[...])
    merged = jax.nn.sigmoid(_dot(h, wg_ref[:, :D_MODEL])) * y_a
    merged += jax.nn.sigmoid(_dot(h, wg_ref[:, D_MODEL:2 * D_MODEL])) * y_b
    merged += jax.nn.sigmoid(_dot(h, wg_ref[:, 2 * D_MODEL:])) * y_c
    out_ref[...] = x + _dot(merged.astype(BF16), wout_ref[...])


def _merge(x, yf, y5, o, lw, seq):
    n = x.shape[0]
    tm = min(512, seq)
    nt = seq // tm
    row = lambda w: pl.BlockSpec((tm, w), lambda i: (i, 0))
    y5_spec = pl.BlockSpec((S5_TILES, tm // S5_CHUNK, 1, S5_CHUNK, LANES),
                           lambda i: (0, i % nt, i // nt, 0, 0))
    return pl.pallas_call(
        _merge_kernel,
        grid=(n // tm,),
        in_specs=[row(D_MODEL), row(FNET_W), y5_spec, row(MLA_HEADS * V_DIM),
                  _const_spec((1, D_MODEL)), _const_spec((FNET_W, D_MODEL)),
                  _const_spec((S5_W, 2 * S5_W)), _const_spec((S5_W, D_MODEL)),
                  _const_spec((MLA_HEADS * V_DIM, D_MODEL)),
                  _const_spec((D_MODEL, N_BRANCH * D_MODEL)), _const_spec((D_MODEL, D_MODEL))],
        out_specs=row(D_MODEL),
        out_shape=jax.ShapeDtypeStruct((n, D_MODEL), F32),
        compiler_params=_params("parallel"),
        name="merge",
    )(x, yf, y5, o, lw["g_mix"], lw["w_fnet"], lw["w_glu"], lw["w_s5"], lw["w_o"], lw["w_gate"],
      lw["w_out"])


def _mlp_kernel(x_ref, g_ref, wup_ref, wdn_ref, gfin_ref, out_ref, h_ref, acc_ref, *, nf, final):
    j = pl.program_id(1)

    @pl.when(j == 0)
    def _():
        h_ref[...] = _rms(x_ref[...], g_ref[...]).astype(BF16)

    a = jnp.square(jnp.maximum(_dot(h_ref[...], wup_ref[...]), 0.0)).astype(BF16)
    part = _dot(a, wdn_ref[...])

    @pl.when(j == 0)
    def _():
        acc_ref[...] = part

    @pl.when(j > 0)
    def _():
        acc_ref[...] += part

    @pl.when(j == nf - 1)
    def _():
        y = x_ref[...] + acc_ref[...]
        if final:
            y = _rms(y, gfin_ref[...])
        out_ref[...] = y


def _mlp(x, lw, g_final, final):
    n = x.shape[0]
    tm = min(1024, n)
    tf = 1024
    nf = D_FF // tf
    return pl.pallas_call(
        functools.partial(_mlp_kernel, nf=nf, final=final),
        grid=(n // tm, nf),
        in_specs=[pl.BlockSpec((tm, D_MODEL), lambda i, j: (i, 0)),
                  pl.BlockSpec((1, D_MODEL), lambda i, j: (0, 0)),
                  pl.BlockSpec((D_MODEL, tf), lambda i, j: (0, j)),
                  pl.BlockSpec((tf, D_MODEL), lambda i, j: (j, 0)),
                  pl.BlockSpec((1, D_MODEL), lambda i, j: (0, 0))],
        out_specs=pl.BlockSpec((tm, D_MODEL), lambda i, j: (i, 0)),
        out_shape=jax.ShapeDtypeStruct((n, D_MODEL), F32),
        scratch_shapes=[pltpu.VMEM((tm, D_MODEL), BF16), pltpu.VMEM((tm, D_MODEL), F32)],
        compiler_params=_params("parallel", "arbitrary"),
        name="mlp",
    )(x, lw["g_mlp"], lw["w_up"], lw["w_down"], g_final)


def _tables(seq):
    half = QK_ROPE // 2
    inv = ROPE_BASE ** (-jnp.arange(half, dtype=F32) / half)
    ang = jnp.arange(seq, dtype=F32)[:, None] * inv[None, :]
    cos, sin = jnp.cos(ang), jnp.sin(ang)
    one = jnp.ones((seq, QK_NOPE), F32)
    z64 = jnp.zeros((seq, QK_NOPE), F32)
    z32 = jnp.zeros((seq, HEAD_PAD - QK_NOPE - QK_ROPE), F32)
    scale = (QK_NOPE + QK_ROPE) ** -0.5 * math.log2(math.e)
    ta = jnp.concatenate([z64, cos, cos, z32], axis=1)
    tb = jnp.concatenate([z64, -sin, sin, z32], axis=1)
    taq = jnp.concatenate([one, cos, cos, z32], axis=1) * scale
    c = jnp.arange(FNET_GROUP_DIM)
    ang64 = (2.0 * math.pi / FNET_GROUP_DIM) * ((c[:, None] * c[None, :]) % FNET_GROUP_DIM).astype(F32)
    eye = jnp.eye(FNET_GROUPS, dtype=F32)
    norm = 1.0 / math.sqrt(seq * FNET_GROUP_DIM)
    cs = jnp.concatenate([jnp.kron(eye, jnp.cos(ang64)), jnp.kron(eye, jnp.sin(ang64))], axis=1) * norm
    r = LANES if seq % LANES == 0 else 1
    k = jnp.arange(seq)[:, None]
    w = 2.0 * math.pi / seq
    a1 = w * ((k * (jnp.arange(seq // r)[None, :] * r)) % seq).astype(F32)
    a2 = w * ((k * jnp.arange(r)[None, :]) % seq).astype(F32)
    c1, s1 = jnp.cos(a1)[:, :, None], jnp.sin(a1)[:, :, None]
    c2, s2 = jnp.cos(a2)[:, None, :], jnp.sin(a2)[:, None, :]
    dft_c = (c1 * c2 - s1 * s2).reshape(seq, seq).astype(BF16)
    dft_s = (s1 * c2 + c1 * s2).reshape(seq, seq).astype(BF16)
    vone = jnp.tile(jnp.concatenate([jnp.zeros((1, 2 * V_DIM), F32), jnp.ones((1, 2 * HEAD_PAD - 2 * V_DIM), F32)],
                                    axis=1), (1, MLA_HEADS // 2))
    return dict(taq=taq, tbq=tb * scale, tak=ta, tbk=tb, cs=cs.astype(BF16), dft_c=dft_c, dft_s=dft_s,
                vone=vone)


def _pad_heads(w, width):
    k = w.shape[0]
    w = w.reshape(k, MLA_HEADS, width)
    return jnp.pad(w, ((0, 0), (0, 0), (0, HEAD_PAD - width))).reshape(k, MLA_HEADS * HEAD_PAD)


def _cmul(ar, ai, br, bi):
    return ar * br - ai * bi, ar * bi + ai * br


def _s5_tables(lam_re, lam_im, log_dt, b_re, b_im, c_re, c_im, d_skip):
    t, p, g, ns = S5_CHUNK, S5_GROUP_DIM, S5_GROUPS, S5_STATE
    nt, tg = S5_TILES, S5_TILE_GROUPS
    dt = jnp.exp(log_dt)[..., None]
    ar, ai = lam_re * dt, lam_im * dt
    mag = jnp.exp(ar)
    lbr, lbi = mag * jnp.cos(ai), mag * jnp.sin(ai)
    den = lam_re * lam_re + lam_im * lam_im
    cfr = ((lbr - 1.0) * lam_re + lbi * lam_im) / den
    cfi = (lbi * lam_re - (lbr - 1.0) * lam_im) / den
    bbr, bbi = _cmul(cfr[..., None], cfi[..., None], b_re, b_im)
    d = jnp.arange(t + 1, dtype=F32)
    pmag = jnp.exp(ar[..., None] * d)
    pwr, pwi = pmag * jnp.cos(ai[..., None] * d), pmag * jnp.sin(ai[..., None] * d)
    cpr, cpi = _cmul(c_re[..., None], c_im[..., None], pwr[:, :, None, :, :t], pwi[:, :, None, :, :t])
    kern = (jnp.einsum('xgpnd,xgnq->xgdpq', cpr, bbr)
            - jnp.einsum('xgpnd,xgnq->xgdpq', cpi, bbi))
    skip = jnp.eye(p, dtype=F32)[None] * d_skip.reshape(g, p)[:, :, None]
    k0 = kern[0][:, :1] + kern[1][:, :1] + skip[:, None]
    kfull = jnp.concatenate([kern[1][:, :0:-1], k0, kern[0][:, 1:]], axis=1)
    eye = jnp.eye(tg, dtype=F32)
    kq = kfull.transpose(0, 1, 3, 2).reshape(nt, tg, 2 * t - 1, p, p)
    bd = jnp.einsum('Ggdqp,gh->Gdgqhp', kq, eye).reshape(nt, 2 * t - 1, LANES, LANES)
    lag = jnp.arange(t)[None, :] - jnp.arange(t)[:, None] + (t - 1)
    m = bd[:, lag].transpose(0, 1, 3, 2, 4).reshape(nt, t * LANES, t * LANES)
    rev = lambda v: v[..., ::-1]
    lay_p = lambda v: v.reshape(2, nt, tg * ns, t).transpose(0, 1, 3, 2)
    psr = lay_p(jnp.stack([rev(pwr[0])[..., 1:], pwr[1][..., :t]]))
    psi = lay_p(jnp.stack([rev(pwi[0])[..., 1:], pwi[1][..., :t]]))
    lay_b = lambda v: jnp.einsum('xGhnq,gh->xGgqhn', v.reshape(2, nt, tg, ns, p), eye).reshape(
        2, nt, LANES, tg * ns)
    bmr, bmi = lay_b(bbr), lay_b(bbi)
    bsr, bsi = _cmul(psr[:, :, :, None, :], psi[:, :, :, None, :], bmr[:, :, None], bmi[:, :, None])
    bst = jnp.stack([bsr, bsi], axis=1).transpose(2, 3, 4, 0, 1, 5).reshape(nt, t * LANES, 2 * S5_DIR)
    lay_c = lambda v: jnp.einsum('xGhpn,gh->xGgnhp', v.reshape(2, nt, tg, p, ns), eye).reshape(
        2, nt, tg * ns, LANES)
    cmr, cmi = lay_c(c_re), lay_c(c_im)
    por = jnp.stack([pwr[0][..., 1:], rev(pwr[1])[..., :t]]).reshape(2, nt, tg * ns, t)
    poi = jnp.stack([pwi[0][..., 1:], rev(pwi[1])[..., :t]]).reshape(2, nt, tg * ns, t)
    csr, csi = _cmul(cmr[:, :, :, None, :], cmi[:, :, :, None, :], por[..., None], poi[..., None])
    cst = jnp.stack([csr, -csi], axis=1).transpose(2, 0, 1, 3, 4, 5).reshape(nt, 2 * S5_DIR, t * LANES)
    a = jnp.stack([pwr[..., t], pwi[..., t]], axis=1)
    a = a.reshape(2, 2, nt, tg, ns).transpose(2, 0, 1, 3, 4).reshape(nt, 1, 2 * S5_DIR)
    a = jnp.broadcast_to(a, (nt, S5_BATCH, 2 * S5_DIR))
    return dict(s5_m=m.astype(BF16), s5_bst=bst.astype(BF16), s5_cst=cst.astype(BF16), s5_a=a.astype(F32))


def _layer_weights(i, p):
    w_in = p["w_in"][i]
    k_dim = w_in.shape[0]
    half = QK_ROPE // 2
    kr = w_in[:, OFF_KR:OFF_GATE]
    z64 = jnp.zeros((k_dim, QK_NOPE), F32)
    z32 = jnp.zeros((k_dim, HEAD_PAD - QK_NOPE - QK_ROPE), F32)
    w_a = jnp.concatenate([w_in[:, :OFF_KR], z64, kr, z32, z64, kr[:, half:], kr[:, :half], z32], axis=1)
    wq = p["w_qb"][i].reshape(Q_LORA, MLA_HEADS, QK_NOPE + QK_ROPE)
    rope = wq[:, :, QK_NOPE:]
    wq_b = jnp.concatenate([jnp.zeros_like(wq[:, :, :QK_NOPE]), rope[:, :, half:], rope[:, :, :half]], axis=2)
    wkv = p["w_kvb"][i].reshape(KV_LORA, MLA_HEADS, QK_NOPE + V_DIM)
    row = lambda v: v.reshape(1, -1).astype(F32)
    lw = dict(
        g_mix=row(p["g_mix"][i]), w_a=w_a.astype(BF16), w_gate=w_in[:, OFF_GATE:].astype(BF16),
        g_q=row(p["g_q"][i]), g_kv=row(p["g_kv"][i]),
        wq_a=_pad_heads(p["w_qb"][i], QK_NOPE + QK_ROPE).astype(BF16),
        wq_b=_pad_heads(wq_b.reshape(Q_LORA, -1), QK_NOPE + QK_ROPE).astype(BF16),
        wk=_pad_heads(wkv[:, :, :QK_NOPE].reshape(KV_LORA, -1), QK_NOPE).astype(BF16),
        wv=jnp.pad(wkv[:, :, QK_NOPE:].reshape(KV_LORA, MLA_HEADS // 2, 2 * V_DIM),
                   ((0, 0), (0, 0), (0, 2 * HEAD_PAD - 2 * V_DIM))).reshape(KV_LORA, -1).astype(BF16),
        w_fnet=p["w_fnet"][i].astype(BF16), w_glu=p["w_glu"][i].astype(BF16),
        w_s5=p["w_s5"][i].astype(BF16), w_o=p["w_o_mla"][i].astype(BF16),
        w_out=p["w_out"][i].astype(BF16), g_mlp=row(p["g_mlp"][i]),
        w_up=p["w_up"][i].astype(BF16), w_down=p["w_down"][i].astype(BF16))
    lw.update(_s5_tables(p["s5_lam_re"][i], p["s5_lam_im"][i], p["s5_log_dt"][i], p["s5_b_re"][i],
                         p["s5_b_im"][i], p["s5_c_re"][i], p["s5_c_im"][i], p["s5_d"][i]))
    return lw


def _trunk(x, layers, tabs, g_final):
    bsz, seq, _ = x.shape
    assert bsz <= S5_BATCH and seq % S5_CHUNK == 0
    x = x.reshape(bsz * seq, D_MODEL)
    for i, lw in enumerate(layers):
        ucs, u5, q, k, v = _inproj(x, lw, tabs, bsz, seq)
        yf = _fnet(ucs, tabs, bsz, seq).reshape(bsz * seq, FNET_W)
        y5 = _s5(u5, lw, seq)
        o = _attention(q, k, v, bsz, seq).reshape(bsz * seq, MLA_HEADS * V_DIM)
        x = _merge(x, yf, y5, o, lw, seq)
        x = _mlp(x, lw, g_final, final=(i == len(layers) - 1))
    return x.reshape(bsz, seq, D_MODEL)


def kernel(x_prompt, x_sample, g_mix, w_in, w_fnet, s5_lam_re, s5_lam_im, s5_log_dt, s5_b_re,
           s5_b_im, s5_c_re, s5_c_im, s5_d, w_glu, w_s5, g_q, w_qb, g_kv, w_kvb, w_o_mla,
           w_out, g_mlp, w_up, w_down, g_final):
    p = dict(g_mix=g_mix, w_in=w_in, w_fnet=w_fnet, s5_lam_re=s5_lam_re, s5_lam_im=s5_lam_im,
             s5_log_dt=s5_log_dt, s5_b_re=s5_b_re, s5_b_im=s5_b_im, s5_c_re=s5_c_re,
             s5_c_im=s5_c_im, s5_d=s5_d, w_glu=w_glu, w_s5=w_s5, g_q=g_q, w_qb=w_qb, g_kv=g_kv,
             w_kvb=w_kvb, w_o_mla=w_o_mla, w_out=w_out, g_mlp=g_mlp, w_up=w_up, w_down=w_down)
    layers = [_layer_weights(i, p) for i in range(g_mix.shape[0])]
    gfin = g_final.reshape(1, -1).astype(F32)
    outs = []
    for x in (x_prompt, x_sample):
        tabs = _tables(x.shape[1])
        outs.append(_trunk(x, layers, tabs, gfin))
    return tuple(outs)
```

```python
import functools
import math

import jax
import jax.numpy as jnp
from jax import lax
from jax.experimental import pallas as pl
from jax.experimental.pallas import tpu as pltpu

F32 = jnp.float32
BF16 = jnp.bfloat16

D_MODEL = 1024
FNET_GROUP_DIM = 64
FNET_GROUPS = 6
FNET_W = 384
S5_GROUP_DIM = 16
S5_GROUPS = 24
S5_W = 384
S5_STATE = 64
MLA_HEADS = 16
QK_NOPE = 64
QK_ROPE = 32
V_DIM = 64
Q_LORA = 384
KV_LORA = 256
ROPE_BASE = 10000.0
N_BRANCH = 3
D_FF = 4 * D_MODEL
EPS = 1e-6

OFF_FNET = 0
OFF_S5 = OFF_FNET + FNET_W
OFF_Q = OFF_S5 + S5_W
OFF_KV = OFF_Q + Q_LORA
OFF_KR = OFF_KV + KV_LORA
OFF_GATE = OFF_KR + QK_ROPE

LANES = 128
SUBLANES = 8
HEAD_PAD = 128
VMEM_LIMIT = 56 * 1024 * 1024

S5_CHUNK = SUBLANES
S5_TILES = S5_W // LANES
S5_TILE_GROUPS = LANES // S5_GROUP_DIM
S5_HALF = S5_TILE_GROUPS * S5_STATE
S5_DIR = 2 * S5_HALF
S5_BATCH = SUBLANES
S5_BLOCK_CHUNKS = 64

ZA_CQ = OFF_Q
ZA_CKV = OFF_KV
ZA_KRA = OFF_KR
ZA_KRB = OFF_KR + LANES
ZA_W = ZA_KRB + LANES

ATTN_ROWS = 1024
ATTN_SUB = 512


def _rms(x, g):
    return x * lax.rsqrt(jnp.mean(x * x, axis=-1, keepdims=True) + EPS) * g


def _dot(a, b):
    return jnp.dot(a, b, preferred_element_type=F32)


def _params(*sem):
    return pltpu.CompilerParams(dimension_semantics=sem, vmem_limit_bytes=VMEM_LIMIT)


def _const_spec(shape):
    nd = len(shape)
    return pl.BlockSpec(shape, lambda *_: (0,) * nd)


def _inproj_kernel(x_ref, gmix_ref, wa_ref, cs_ref, gq_ref, wqa_ref, wqb_ref, gkv_ref, wk_ref,
                   wv_ref, taq_ref, tbq_ref, tak_ref, tbk_ref, *rest):
    ucs_ref, u5_ref, q_ref, k_ref, v_ref = rest[-5:]
    h = _rms(x_ref[...], gmix_ref[...]).astype(BF16)
    z = _dot(h, wa_ref[...])
    ucs_ref[...] = _dot(z[:, OFF_FNET:OFF_S5].astype(BF16), cs_ref[...]).astype(BF16)
    for t in range(S5_TILES):
        for c in range(z.shape[0] // S5_CHUNK):
            u5_ref[t, c, 0] = z[c * S5_CHUNK:(c + 1) * S5_CHUNK, OFF_S5 + t * LANES:OFF_S5 + (t + 1) * LANES]
    hq = _rms(z[:, ZA_CQ:ZA_CQ + Q_LORA], gq_ref[...]).astype(BF16)
    hkv = _rms(z[:, ZA_CKV:ZA_CKV + KV_LORA], gkv_ref[...]).astype(BF16)
    kpe = z[:, ZA_KRA:ZA_KRA + LANES] * tak_ref[...] + z[:, ZA_KRB:ZA_KRB + LANES] * tbk_ref[...]
    kpe2 = jnp.concatenate([kpe, kpe], axis=1)
    taq2 = jnp.concatenate([taq_ref[...]] * 2, axis=1)
    tbq2 = jnp.concatenate([tbq_ref[...]] * 2, axis=1)
    for j in range(MLA_HEADS // 2):
        sl = slice(2 * HEAD_PAD * j, 2 * HEAD_PAD * (j + 1))
        qa = _dot(hq, wqa_ref[:, sl])
        qb = _dot(hq, wqb_ref[:, sl])
        q_ref[:, sl] = (qa * taq2 + qb * tbq2).astype(BF16)
        k_ref[:, sl] = (_dot(hkv, wk_ref[:, sl]) + kpe2).astype(BF16)
    vt = lax.dot_general(wv_ref[...], hkv, (((1,), (1,)), ((), ())), preferred_element_type=F32)
    ones_row = (lax.broadcasted_iota(jnp.int32, vt.shape, 0) & V_DIM) != 0
    v_ref[0] = jnp.where(ones_row, 1.0, vt).astype(BF16)


def _inproj(x, lw, tabs, bsz, seq):
    n = x.shape[0]
    tm = min(512, seq)
    nt = seq // tm
    nc = seq // S5_CHUNK
    row = lambda w: pl.BlockSpec((tm, w), lambda i: (i, 0))
    tab = pl.BlockSpec((tm, LANES), lambda i: (i % nt, 0))
    hw = MLA_HEADS * HEAD_PAD
    u5_shape = (S5_TILES, nc, S5_BATCH, S5_CHUNK, LANES)
    u5_spec = pl.BlockSpec((S5_TILES, tm // S5_CHUNK, 1, S5_CHUNK, LANES),
                           lambda i: (0, i % nt, i // nt, 0, 0))
    in_specs = [row(D_MODEL), _const_spec((1, D_MODEL)), _const_spec((D_MODEL, ZA_W)),
                _const_spec((FNET_W, 2 * FNET_W)), _const_spec((1, Q_LORA)),
                _const_spec((Q_LORA, hw)), _const_spec((Q_LORA, hw)), _const_spec((1, KV_LORA)),
                _const_spec((KV_LORA, hw)), _const_spec((hw, KV_LORA)),
                tab, tab, tab, tab]
    args = [x, lw["g_mix"], lw["w_a"], tabs["cs"], lw["g_q"], lw["wq_a"], lw["wq_b"], lw["g_kv"],
            lw["wk"], lw["wv_t"], tabs["taq"], tabs["tbq"], tabs["tak"], tabs["tbk"]]
    aliases = {}
    if bsz < S5_BATCH:
        aliases = {len(args): 1}
        in_specs.append(pl.BlockSpec(memory_space=pl.ANY))
        args.append(jnp.zeros(u5_shape, F32))
    return pl.pallas_call(
        _inproj_kernel,
        grid=(n // tm,),
        in_specs=in_specs,
        out_specs=[row(2 * FNET_W), u5_spec, row(hw), row(hw),
                   pl.BlockSpec((1, hw, tm), lambda i: (i // nt, 0, i % nt))],
        out_shape=[jax.ShapeDtypeStruct((n, 2 * FNET_W), BF16), jax.ShapeDtypeStruct(u5_shape, F32),
                   jax.ShapeDtypeStruct((n, hw), BF16), jax.ShapeDtypeStruct((n, hw), BF16),
                   jax.ShapeDtypeStruct((bsz, hw, seq), BF16)],
        input_output_aliases=aliases,
        compiler_params=_params("parallel"),
        name="inproj",
    )(*args)


def _fnet_kernel(c_ref, s_ref, ucs_ref, out_ref, acc_ref, *, nk):
    k = pl.program_id(1)
    b = pl.program_id(2)
    u0, u1 = ucs_ref[0], ucs_ref[1]
    uc = jnp.concatenate([u0[:, :FNET_W], u1[:, :FNET_W]], axis=1)
    us = jnp.concatenate([u0[:, FNET_W:], u1[:, FNET_W:]], axis=1)
    part = _dot(c_ref[...], uc) - _dot(s_ref[...], us)

    @pl.when(k == 0)
    def _():
        acc_ref[b] = part

    @pl.when(k > 0)
    def _():
        acc_ref[b] += part

    @pl.when(k == nk - 1)
    def _():
        acc = acc_ref[b]
        out_ref[2 * b] = acc[:, :FNET_W].astype(BF16)
        out_ref[2 * b + 1] = acc[:, FNET_W:].astype(BF16)


def _fnet(ucs, tabs, bsz, seq):
    assert bsz % 2 == 0
    tm = min(1024, seq)
    tk = min(1024, seq)
    nk = seq // tk
    return pl.pallas_call(
        functools.partial(_fnet_kernel, nk=nk),
        grid=(seq // tm, nk, bsz // 2),
        in_specs=[pl.BlockSpec((tm, tk), lambda m, k, b: (m, k)),
                  pl.BlockSpec((tm, tk), lambda m, k, b: (m, k)),
                  pl.BlockSpec((2, tk, 2 * FNET_W), lambda m, k, b: (b, k, 0))],
        out_specs=pl.BlockSpec((bsz, tm, FNET_W), lambda m, k, b: (0, m, 0)),
        out_shape=jax.ShapeDtypeStruct((bsz, seq, FNET_W), BF16),
        scratch_shapes=[pltpu.VMEM((bsz // 2, tm, 2 * FNET_W), F32)],
        compiler_params=_params("parallel", "arbitrary", "arbitrary"),
        name="fnet",
    )(tabs["dft_c"], tabs["dft_s"], ucs.reshape(bsz, seq, 2 * FNET_W))


def _s5_rows(u_ref, rows):
    return jnp.concatenate([u_ref[0, pl.ds(s, rows, stride=S5_CHUNK), :] for s in range(S5_CHUNK)],
                           axis=1).astype(BF16)


def _s5_in_kernel(u_ref, bst_ref, loc_ref, *, rows):
    loc_ref[0] = _dot(_s5_rows(u_ref, rows), bst_ref[0])


def _s5_scan_kernel(loc_ref, a_ref, sp_ref, cre_ref, cim_ref, *, cb):
    d = pl.program_id(1)

    @pl.when(pl.program_id(2) == 0)
    def _():
        cre_ref[...] = jnp.zeros_like(cre_ref)
        cim_ref[...] = jnp.zeros_like(cim_ref)

    are = a_ref[0, :, :S5_HALF]
    aim = a_ref[0, :, S5_HALF:]

    def step(i, carry):
        sr, si = carry
        c = jnp.where(d == 0, i, cb - 1 - i)
        r = pl.ds(pl.multiple_of(c * S5_BATCH, S5_BATCH), S5_BATCH)
        sp_ref[0, r, :S5_HALF] = sr
        sp_ref[0, r, S5_HALF:] = si
        return (are * sr - aim * si + loc_ref[0, r, :S5_HALF],
                are * si + aim * sr + loc_ref[0, r, S5_HALF:])

    sr, si = lax.fori_loop(0, cb, step, (cre_ref[...], cim_ref[...]))
    cre_ref[...] = sr
    cim_ref[...] = si


def _s5_out_kernel(u_ref, m_ref, sp_ref, cst_ref, y_ref, *, rows):
    y = _dot(_s5_rows(u_ref, rows), m_ref[0]) + _dot(sp_ref[0].astype(BF16), cst_ref[0])
    for s in range(S5_CHUNK):
        y_ref[0, pl.ds(s, rows, stride=S5_CHUNK), :] = y[:, s * LANES:(s + 1) * LANES]


def _s5(u5, lw, seq):
    nc = seq // S5_CHUNK
    cb = min(S5_BLOCK_CHUNKS, nc)
    nblk = nc // cb
    rows = cb * S5_BATCH
    kw = S5_CHUNK * LANES
    u2 = u5.reshape(S5_TILES, nc * S5_BATCH * S5_CHUNK, LANES)
    u_spec = pl.BlockSpec((1, rows * S5_CHUNK, LANES), lambda t, j: (t, j, 0))
    wspec = lambda a, b: pl.BlockSpec((1, a, b), lambda t, j: (t, 0, 0))
    loc = pl.pallas_call(
        functools.partial(_s5_in_kernel, rows=rows),
        grid=(S5_TILES, nblk),
        in_specs=[u_spec, wspec(kw, 2 * S5_DIR)],
        out_specs=pl.BlockSpec((1, rows, 2 * S5_DIR), lambda t, j: (t, j, 0)),
        out_shape=jax.ShapeDtypeStruct((S5_TILES, nc * S5_BATCH, 2 * S5_DIR), F32),
        compiler_params=_params("parallel", "parallel"),
        name="s5_in",
    )(u2, lw["s5_bst"])
    blk = lambda t, d, j: (t, j + d * (nblk - 1 - 2 * j), d)
    sp = pl.pallas_call(
        functools.partial(_s5_scan_kernel, cb=cb),
        grid=(S5_TILES, 2, nblk),
        in_specs=[pl.BlockSpec((1, rows, S5_DIR), blk),
                  pl.BlockSpec((1, S5_BATCH, S5_DIR), lambda t, d, j: (t, 0, d))],
        out_specs=pl.BlockSpec((1, rows, S5_DIR), blk),
        out_shape=jax.ShapeDtypeStruct((S5_TILES, nc * S5_BATCH, 2 * S5_DIR), F32),
        scratch_shapes=[pltpu.VMEM((S5_BATCH, S5_HALF), F32)] * 2,
        compiler_params=_params("parallel", "arbitrary", "arbitrary"),
        name="s5_scan",
    )(loc, lw["s5_a"])
    y2 = pl.pallas_call(
        functools.partial(_s5_out_kernel, rows=rows),
        grid=(S5_TILES, nblk),
        in_specs=[u_spec, wspec(kw, kw), pl.BlockSpec((1, rows, 2 * S5_DIR), lambda t, j: (t, j, 0)),
                  wspec(2 * S5_DIR, kw)],
        out_specs=u_spec,
        out_shape=jax.ShapeDtypeStruct(u2.shape, F32),
        compiler_params=_params("parallel", "parallel"),
        name="s5_out",
    )(u2, lw["s5_m"], sp, lw["s5_cst"])
    return y2.reshape(u5.shape)


def _attn_kernel(q_ref, k_ref, vt_ref, o_ref, s_ref, m_ref, p_ref, ot_ref, *, sub):
    units = [(u, j) for u in range(q_ref.shape[1] // sub) for j in range(2)]

    def scores(n):
        u, j = units[n]
        sl = slice(HEAD_PAD * j, HEAD_PAD * (j + 1))
        st = lax.dot_general(k_ref[0, :, sl], q_ref[0, u * sub:(u + 1) * sub, sl],
                             (((1,), (1,)), ((), ())), preferred_element_type=F32)
        s_ref[n % 2] = st
        m_ref[n % 2] = jnp.max(st, axis=0, keepdims=True)

    scores(0)
    for n, (u, j) in enumerate(units):
        if n + 1 < len(units):
            scores(n + 1)
        p_ref[...] = jnp.exp2(s_ref[n % 2] - m_ref[n % 2]).astype(BF16)
        ol = _dot(vt_ref[0, HEAD_PAD * j:HEAD_PAD * (j + 1), :], p_ref[...])
        ot_ref[V_DIM * j:V_DIM * (j + 1), u * sub:(u + 1) * sub] = ol[:V_DIM] / ol[V_DIM:]
    o_ref[0] = ot_ref[...].T.astype(BF16)


def _attention(q, k, vt, bsz, seq):
    tq = min(ATTN_ROWS, seq)
    sub = min(ATTN_SUB, tq)
    hw = MLA_HEADS * HEAD_PAD
    return pl.pallas_call(
        functools.partial(_attn_kernel, sub=sub),
        grid=(bsz, MLA_HEADS // 2, seq // tq),
        in_specs=[pl.BlockSpec((1, tq, 2 * HEAD_PAD), lambda b, h, i: (b, i, h)),
                  pl.BlockSpec((1, seq, 2 * HEAD_PAD), lambda b, h, i: (b, 0, h)),
                  pl.BlockSpec((1, 2 * HEAD_PAD, seq), lambda b, h, i: (b, h, 0))],
        out_specs=pl.BlockSpec((1, tq, 2 * V_DIM), lambda b, h, i: (b, i, h)),
        out_shape=jax.ShapeDtypeStruct((bsz, seq, MLA_HEADS * V_DIM), BF16),
        scratch_shapes=[pltpu.VMEM((2, seq, sub), F32), pltpu.VMEM((2, 1, sub), F32),
                        pltpu.VMEM((seq, sub), BF16), pltpu.VMEM((2 * V_DIM, tq), F32)],
        compiler_params=_params("parallel", "parallel", "arbitrary"),
        name="attention",
    )(q.reshape(bsz, seq, hw), k.reshape(bsz, seq, hw), vt)


def _merge_kernel(x_ref, yf_ref, y5_ref, o_ref, gmix_ref, wfn_ref, wglu_ref, ws5_ref, wo_ref,
                  wg_ref, wout_ref, out_ref):
    x = x_ref[...]
    h = _rms(x, gmix_ref[...]).astype(BF16)
    y_a = _dot(yf_ref[...], wfn_ref[...])
    nchunk = y5_ref.shape[1]
    ys = jnp.concatenate(
        [jnp.concatenate([y5_ref[t, c, 0] for c in range(nchunk)], axis=0) for t in range(S5_TILES)],
        axis=1)
    s = jax.nn.gelu(ys).astype(BF16)
    hg = _dot(s, wglu_ref[...])
    glu = (hg[:, :S5_W] * jax.nn.sigmoid(hg[:, S5_W:])).astype(BF16)
    y_b = _dot(glu, ws5_ref[...])
    y_c = _dot(o_ref[...], wo_ref[...])
    merged = jax.nn.sigmoid(_dot(h, wg_ref[:, :D_MODEL])) * y_a
    merged += jax.nn.sigmoid(_dot(h, wg_ref[:, D_MODEL:2 * D_MODEL])) * y_b
    merged += jax.nn.sigmoid(_dot(h, wg_ref[:, 2 * D_MODEL:])) * y_c
    out_ref[...] = x + _dot(merged.astype(BF16), wout_ref[...])


def _merge(x, yf, y5, o, lw, seq):
    n = x.shape[0]
    tm = min(512, seq)
    nt = seq // tm
    row = lambda w: pl.BlockSpec((tm, w), lambda i: (i, 0))
    y5_spec = pl.BlockSpec((S5_TILES, tm // S5_CHUNK, 1, S5_CHUNK, LANES),
                           lambda i: (0, i % nt, i // nt, 0, 0))
    return pl.pallas_call(
        _merge_kernel,
        grid=(n // tm,),
        in_specs=[row(D_MODEL), row(FNET_W), y5_spec, row(MLA_HEADS * V_DIM),
                  _const_spec((1, D_MODEL)), _const_spec((FNET_W, D_MODEL)),
                  _const_spec((S5_W, 2 * S5_W)), _const_spec((S5_W, D_MODEL)),
                  _const_spec((MLA_HEADS * V_DIM, D_MODEL)),
                  _const_spec((D_MODEL, N_BRANCH * D_MODEL)), _const_spec((D_MODEL, D_MODEL))],
        out_specs=row(D_MODEL),
        out_shape=jax.ShapeDtypeStruct((n, D_MODEL), F32),
        compiler_params=_params("parallel"),
        name="merge",
    )(x, yf, y5, o, lw["g_mix"], lw["w_fnet"], lw["w_glu"], lw["w_s5"], lw["w_o"], lw["w_gate"],
      lw["w_out"])


def _mlp_kernel(x_ref, g_ref, wup_ref, wdn_ref, gfin_ref, out_ref, h_ref, acc_ref, *, nf, final):
    j = pl.program_id(1)

    @pl.when(j == 0)
    def _():
        h_ref[...] = _rms(x_ref[...], g_ref[...]).astype(BF16)

    a = jnp.square(jnp.maximum(_dot(h_ref[...], wup_ref[...]), 0.0)).astype(BF16)
    part = _dot(a, wdn_ref[...])

    @pl.when(j == 0)
    def _():
        acc_ref[...] = part

    @pl.when(j > 0)
    def _():
        acc_ref[...] += part

    @pl.when(j == nf - 1)
    def _():
        y = x_ref[...] + acc_ref[...]
        if final:
            y = _rms(y, gfin_ref[...])
        out_ref[...] = y


def _mlp(x, lw, g_final, final):
    n = x.shape[0]
    tm = min(1024, n)
    tf = 1024
    nf = D_FF // tf
    return pl.pallas_call(
        functools.partial(_mlp_kernel, nf=nf, final=final),
        grid=(n // tm, nf),
        in_specs=[pl.BlockSpec((tm, D_MODEL), lambda i, j: (i, 0)),
                  pl.BlockSpec((1, D_MODEL), lambda i, j: (0, 0)),
                  pl.BlockSpec((D_MODEL, tf), lambda i, j: (0, j)),
                  pl.BlockSpec((tf, D_MODEL), lambda i, j: (j, 0)),
                  pl.BlockSpec((1, D_MODEL), lambda i, j: (0, 0))],
        out_specs=pl.BlockSpec((tm, D_MODEL), lambda i, j: (i, 0)),
        out_shape=jax.ShapeDtypeStruct((n, D_MODEL), F32),
        scratch_shapes=[pltpu.VMEM((tm, D_MODEL), BF16), pltpu.VMEM((tm, D_MODEL), F32)],
        compiler_params=_params("parallel", "arbitrary"),
        name="mlp",
    )(x, lw["g_mlp"], lw["w_up"], lw["w_down"], g_final)


def _tables(seq):
    half = QK_ROPE // 2
    inv = ROPE_BASE ** (-jnp.arange(half, dtype=F32) / half)
    ang = jnp.arange(seq, dtype=F32)[:, None] * inv[None, :]
    cos, sin = jnp.cos(ang), jnp.sin(ang)
    one = jnp.ones((seq, QK_NOPE), F32)
    z64 = jnp.zeros((seq, QK_NOPE), F32)
    z32 = jnp.zeros((seq, HEAD_PAD - QK_NOPE - QK_ROPE), F32)
    scale = (QK_NOPE + QK_ROPE) ** -0.5 * math.log2(math.e)
    ta = jnp.concatenate([z64, cos, cos, z32], axis=1)
    tb = jnp.concatenate([z64, -sin, sin, z32], axis=1)
    taq = jnp.concatenate([one, cos, cos, z32], axis=1) * scale
    c = jnp.arange(FNET_GROUP_DIM)
    ang64 = (2.0 * math.pi / FNET_GROUP_DIM) * ((c[:, None] * c[None, :]) % FNET_GROUP_DIM).astype(F32)
    eye = jnp.eye(FNET_GROUPS, dtype=F32)
    norm = 1.0 / math.sqrt(seq * FNET_GROUP_DIM)
    cs = jnp.concatenate([jnp.kron(eye, jnp.cos(ang64)), jnp.kron(eye, jnp.sin(ang64))], axis=1) * norm
    r = LANES if seq % LANES == 0 else 1
    k = jnp.arange(seq)[:, None]
    w = 2.0 * math.pi / seq
    a1 = w * ((k * (jnp.arange(seq // r)[None, :] * r)) % seq).astype(F32)
    a2 = w * ((k * jnp.arange(r)[None, :]) % seq).astype(F32)
    c1, s1 = jnp.cos(a1)[:, :, None], jnp.sin(a1)[:, :, None]
    c2, s2 = jnp.cos(a2)[:, None, :], jnp.sin(a2)[:, None, :]
    dft_c = (c1 * c2 - s1 * s2).reshape(seq, seq).astype(BF16)
    dft_s = (s1 * c2 + c1 * s2).reshape(seq, seq).astype(BF16)
    return dict(taq=taq, tbq=tb * scale, tak=ta, tbk=tb, cs=cs.astype(BF16), dft_c=dft_c, dft_s=dft_s)


def _pad_heads(w, width):
    k = w.shape[0]
    w = w.reshape(k, MLA_HEADS, width)
    return jnp.pad(w, ((0, 0), (0, 0), (0, HEAD_PAD - width))).reshape(k, MLA_HEADS * HEAD_PAD)


def _cmul(ar, ai, br, bi):
    return ar * br - ai * bi, ar * bi + ai * br


def _s5_tables(lam_re, lam_im, log_dt, b_re, b_im, c_re, c_im, d_skip):
    t, p, g, ns = S5_CHUNK, S5_GROUP_DIM, S5_GROUPS, S5_STATE
    nt, tg = S5_TILES, S5_TILE_GROUPS
    dt = jnp.exp(log_dt)[..., None]
    ar, ai = lam_re * dt, lam_im * dt
    mag = jnp.exp(ar)
    lbr, lbi = mag * jnp.cos(ai), mag * jnp.sin(ai)
    den = lam_re * lam_re + lam_im * lam_im
    cfr = ((lbr - 1.0) * lam_re + lbi * lam_im) / den
    cfi = (lbi * lam_re - (lbr - 1.0) * lam_im) / den
    bbr, bbi = _cmul(cfr[..., None], cfi[..., None], b_re, b_im)
    d = jnp.arange(t + 1, dtype=F32)
    pmag = jnp.exp(ar[..., None] * d)
    pwr, pwi = pmag * jnp.cos(ai[..., None] * d), pmag * jnp.sin(ai[..., None] * d)
    cpr, cpi = _cmul(c_re[..., None], c_im[..., None], pwr[:, :, None, :, :t], pwi[:, :, None, :, :t])
    kern = (jnp.einsum('xgpnd,xgnq->xgdpq', cpr, bbr)
            - jnp.einsum('xgpnd,xgnq->xgdpq', cpi, bbi))
    skip = jnp.eye(p, dtype=F32)[None] * d_skip.reshape(g, p)[:, :, None]
    k0 = kern[0][:, :1] + kern[1][:, :1] + skip[:, None]
    kfull = jnp.concatenate([kern[1][:, :0:-1], k0, kern[0][:, 1:]], axis=1)
    eye = jnp.eye(tg, dtype=F32)
    kq = kfull.transpose(0, 1, 3, 2).reshape(nt, tg, 2 * t - 1, p, p)
    bd = jnp.einsum('Ggdqp,gh->Gdgqhp', kq, eye).reshape(nt, 2 * t - 1, LANES, LANES)
    lag = jnp.arange(t)[None, :] - jnp.arange(t)[:, None] + (t - 1)
    m = bd[:, lag].transpose(0, 1, 3, 2, 4).reshape(nt, t * LANES, t * LANES)
    rev = lambda v: v[..., ::-1]
    lay_p = lambda v: v.reshape(2, nt, tg * ns, t).transpose(0, 1, 3, 2)
    psr = lay_p(jnp.stack([rev(pwr[0])[..., 1:], pwr[1][..., :t]]))
    psi = lay_p(jnp.stack([rev(pwi[0])[..., 1:], pwi[1][..., :t]]))
    lay_b = lambda v: jnp.einsum('xGhnq,gh->xGgqhn', v.reshape(2, nt, tg, ns, p), eye).reshape(
        2, nt, LANES, tg * ns)
    bmr, bmi = lay_b(bbr), lay_b(bbi)
    bsr, bsi = _cmul(psr[:, :, :, None, :], psi[:, :, :, None, :], bmr[:, :, None], bmi[:, :, None])
    bst = jnp.stack([bsr, bsi], axis=1).transpose(2, 3, 4, 0, 1, 5).reshape(nt, t * LANES, 2 * S5_DIR)
    lay_c = lambda v: jnp.einsum('xGhpn,gh->xGgnhp', v.reshape(2, nt, tg, p, ns), eye).reshape(
        2, nt, tg * ns, LANES)
    cmr, cmi = lay_c(c_re), lay_c(c_im)
    por = jnp.stack([pwr[0][..., 1:], rev(pwr[1])[..., :t]]).reshape(2, nt, tg * ns, t)
    poi = jnp.stack([pwi[0][..., 1:], rev(pwi[1])[..., :t]]).reshape(2, nt, tg * ns, t)
    csr, csi = _cmul(cmr[:, :, :, None, :], cmi[:, :, :, None, :], por[..., None], poi[..., None])
    cst = jnp.stack([csr, -csi], axis=1).transpose(2, 0, 1, 3, 4, 5).reshape(nt, 2 * S5_DIR, t * LANES)
    a = jnp.stack([pwr[..., t], pwi[..., t]], axis=1)
    a = a.reshape(2, 2, nt, tg, ns).transpose(2, 0, 1, 3, 4).reshape(nt, 1, 2 * S5_DIR)
    a = jnp.broadcast_to(a, (nt, S5_BATCH, 2 * S5_DIR))
    return dict(s5_m=m.astype(BF16), s5_bst=bst.astype(BF16), s5_cst=cst.astype(BF16), s5_a=a.astype(F32))


def _layer_weights(i, p):
    w_in = p["w_in"][i]
    k_dim = w_in.shape[0]
    half = QK_ROPE // 2
    kr = w_in[:, OFF_KR:OFF_GATE]
    z64 = jnp.zeros((k_dim, QK_NOPE), F32)
    z32 = jnp.zeros((k_dim, HEAD_PAD - QK_NOPE - QK_ROPE), F32)
    w_a = jnp.concatenate([w_in[:, :OFF_KR], z64, kr, z32, z64, kr[:, half:], kr[:, :half], z32], axis=1)
    wq = p["w_qb"][i].reshape(Q_LORA, MLA_HEADS, QK_NOPE + QK_ROPE)
    rope = wq[:, :, QK_NOPE:]
    wq_b = jnp.concatenate([jnp.zeros_like(wq[:, :, :QK_NOPE]), rope[:, :, half:], rope[:, :, :half]], axis=2)
    wkv = p["w_kvb"][i].reshape(KV_LORA, MLA_HEADS, QK_NOPE + V_DIM)
    row = lambda v: v.reshape(1, -1).astype(F32)
    lw = dict(
        g_mix=row(p["g_mix"][i]), w_a=w_a.astype(BF16), w_gate=w_in[:, OFF_GATE:].astype(BF16),
        g_q=row(p["g_q"][i]), g_kv=row(p["g_kv"][i]),
        wq_a=_pad_heads(p["w_qb"][i], QK_NOPE + QK_ROPE).astype(BF16),
        wq_b=_pad_heads(wq_b.reshape(Q_LORA, -1), QK_NOPE + QK_ROPE).astype(BF16),
        wk=_pad_heads(wkv[:, :, :QK_NOPE].reshape(KV_LORA, -1), QK_NOPE).astype(BF16),
        wv_t=_pad_heads(wkv[:, :, QK_NOPE:].reshape(KV_LORA, -1), V_DIM).T.astype(BF16),
        w_fnet=p["w_fnet"][i].astype(BF16), w_glu=p["w_glu"][i].astype(BF16),
        w_s5=p["w_s5"][i].astype(BF16), w_o=p["w_o_mla"][i].astype(BF16),
        w_out=p["w_out"][i].astype(BF16), g_mlp=row(p["g_mlp"][i]),
        w_up=p["w_up"][i].astype(BF16), w_down=p["w_down"][i].astype(BF16))
    lw.update(_s5_tables(p["s5_lam_re"][i], p["s5_lam_im"][i], p["s5_log_dt"][i], p["s5_b_re"][i],
                         p["s5_b_im"][i], p["s5_c_re"][i], p["s5_c_im"][i], p["s5_d"][i]))
    return lw


def _trunk(x, layers, tabs, g_final):
    bsz, seq, _ = x.shape
    assert bsz <= S5_BATCH and seq % S5_CHUNK == 0
    x = x.reshape(bsz * seq, D_MODEL)
    for i, lw in enumerate(layers):
        ucs, u5, q, k, vt = _inproj(x, lw, tabs, bsz, seq)
        yf = _fnet(ucs, tabs, bsz, seq).reshape(bsz * seq, FNET_W)
        y5 = _s5(u5, lw, seq)
        o = _attention(q, k, vt, bsz, seq).reshape(bsz * seq, MLA_HEADS * V_DIM)
        x = _merge(x, yf, y5, o, lw, seq)
        x = _mlp(x, lw, g_final, final=(i == len(layers) - 1))
    return x.reshape(bsz, seq, D_MODEL)


def kernel(x_prompt, x_sample, g_mix, w_in, w_fnet, s5_lam_re, s5_lam_im, s5_log_dt, s5_b_re,
           s5_b_im, s5_c_re, s5_c_im, s5_d, w_glu, w_s5, g_q, w_qb, g_kv, w_kvb, w_o_mla,
           w_out, g_mlp, w_up, w_down, g_final):
    p = dict(g_mix=g_mix, w_in=w_in, w_fnet=w_fnet, s5_lam_re=s5_lam_re, s5_lam_im=s5_lam_im,
             s5_log_dt=s5_log_dt, s5_b_re=s5_b_re, s5_b_im=s5_b_im, s5_c_re=s5_c_re,
             s5_c_im=s5_c_im, s5_d=s5_d, w_glu=w_glu, w_s5=w_s5, g_q=g_q, w_qb=w_qb, g_kv=g_kv,
             w_kvb=w_kvb, w_o_mla=w_o_mla, w_out=w_out, g_mlp=g_mlp, w_up=w_up, w_down=w_down)
    layers = [_layer_weights(i, p) for i in range(g_mix.shape[0])]
    gfin = g_final.reshape(1, -1).astype(F32)
    outs = []
    for x in (x_prompt, x_sample):
        tabs = _tables(x.shape[1])
        outs.append(_trunk(x, layers, tabs, gfin))
    return tuple(outs)
```

```python
import functools
import math

import jax
import jax.numpy as jnp
from jax import lax
from jax.experimental import pallas as pl
from jax.experimental.pallas import tpu as pltpu

F32 = jnp.float32
BF16 = jnp.bfloat16

D_MODEL = 1024
FNET_GROUP_DIM = 64
FNET_GROUPS = 6
FNET_W = 384
S5_GROUP_DIM = 16
S5_GROUPS = 24
S5_W = 384
S5_STATE = 64
MLA_HEADS = 16
QK_NOPE = 64
QK_ROPE = 32
V_DIM = 64
Q_LORA = 384
KV_LORA = 256
ROPE_BASE = 10000.0
N_BRANCH = 3
D_FF = 4 * D_MODEL
EPS = 1e-6

OFF_FNET = 0
OFF_S5 = OFF_FNET + FNET_W
OFF_Q = OFF_S5 + S5_W
OFF_KV = OFF_Q + Q_LORA
OFF_KR = OFF_KV + KV_LORA
OFF_GATE = OFF_KR + QK_ROPE

LANES = 128
SUBLANES = 8
HEAD_PAD = 128
VMEM_LIMIT = 56 * 1024 * 1024

S5_CHUNK = SUBLANES
S5_TILES = S5_W // LANES
S5_TILE_GROUPS = LANES // S5_GROUP_DIM
S5_HALF = S5_TILE_GROUPS * S5_STATE
S5_DIR = 2 * S5_HALF
S5_BATCH = SUBLANES
S5_BLOCK_CHUNKS = 64

ZA_CQ = OFF_Q
ZA_CKV = OFF_KV
ZA_KRA = OFF_KR
ZA_KRB = OFF_KR + LANES
ZA_W = ZA_KRB + LANES

ATTN_ROWS = 1024
ATTN_SUB = 512


def _rms(x, g):
    return x * lax.rsqrt(jnp.mean(x * x, axis=-1, keepdims=True) + EPS) * g


def _dot(a, b):
    return jnp.dot(a, b, preferred_element_type=F32)


def _params(*sem):
    return pltpu.CompilerParams(dimension_semantics=sem, vmem_limit_bytes=VMEM_LIMIT)


def _const_spec(shape):
    nd = len(shape)
    return pl.BlockSpec(shape, lambda *_: (0,) * nd)


def _inproj_kernel(x_ref, gmix_ref, wa_ref, cs_ref, gq_ref, wqa_ref, wqb_ref, gkv_ref, wk_ref,
                   wv_ref, taq_ref, tbq_ref, tak_ref, tbk_ref, *rest):
    ucs_ref, u5_ref, q_ref, k_ref, v_ref = rest[-5:]
    h = _rms(x_ref[...], gmix_ref[...]).astype(BF16)
    z = _dot(h, wa_ref[...])
    ucs_ref[...] = _dot(z[:, OFF_FNET:OFF_S5].astype(BF16), cs_ref[...]).astype(BF16)
    for t in range(S5_TILES):
        for c in range(z.shape[0] // S5_CHUNK):
            u5_ref[t, c, 0] = z[c * S5_CHUNK:(c + 1) * S5_CHUNK, OFF_S5 + t * LANES:OFF_S5 + (t + 1) * LANES]
    hq = _rms(z[:, ZA_CQ:ZA_CQ + Q_LORA], gq_ref[...]).astype(BF16)
    hkv = _rms(z[:, ZA_CKV:ZA_CKV + KV_LORA], gkv_ref[...]).astype(BF16)
    kpe = z[:, ZA_KRA:ZA_KRA + LANES] * tak_ref[...] + z[:, ZA_KRB:ZA_KRB + LANES] * tbk_ref[...]
    kpe2 = jnp.concatenate([kpe, kpe], axis=1)
    taq2 = jnp.concatenate([taq_ref[...]] * 2, axis=1)
    tbq2 = jnp.concatenate([tbq_ref[...]] * 2, axis=1)
    for j in range(MLA_HEADS // 2):
        sl = slice(2 * HEAD_PAD * j, 2 * HEAD_PAD * (j + 1))
        qa = _dot(hq, wqa_ref[:, sl])
        qb = _dot(hq, wqb_ref[:, sl])
        q_ref[:, sl] = (qa * taq2 + qb * tbq2).astype(BF16)
        k_ref[:, sl] = (_dot(hkv, wk_ref[:, sl]) + kpe2).astype(BF16)
    vt = lax.dot_general(wv_ref[...], hkv, (((1,), (1,)), ((), ())), preferred_element_type=F32)
    ones_row = (lax.broadcasted_iota(jnp.int32, vt.shape, 0) & V_DIM) != 0
    v_ref[0] = jnp.where(ones_row, 1.0, vt).astype(BF16)


def _inproj(x, lw, tabs, bsz, seq):
    n = x.shape[0]
    tm = min(512, seq)
    nt = seq // tm
    nc = seq // S5_CHUNK
    row = lambda w: pl.BlockSpec((tm, w), lambda i: (i, 0))
    tab = pl.BlockSpec((tm, LANES), lambda i: (i % nt, 0))
    hw = MLA_HEADS * HEAD_PAD
    u5_shape = (S5_TILES, nc, S5_BATCH, S5_CHUNK, LANES)
    u5_spec = pl.BlockSpec((S5_TILES, tm // S5_CHUNK, 1, S5_CHUNK, LANES),
                           lambda i: (0, i % nt, i // nt, 0, 0))
    in_specs = [row(D_MODEL), _const_spec((1, D_MODEL)), _const_spec((D_MODEL, ZA_W)),
                _const_spec((FNET_W, 2 * FNET_W)), _const_spec((1, Q_LORA)),
                _const_spec((Q_LORA, hw)), _const_spec((Q_LORA, hw)), _const_spec((1, KV_LORA)),
                _const_spec((KV_LORA, hw)), _const_spec((hw, KV_LORA)),
                tab, tab, tab, tab]
    args = [x, lw["g_mix"], lw["w_a"], tabs["cs"], lw["g_q"], lw["wq_a"], lw["wq_b"], lw["g_kv"],
            lw["wk"], lw["wv_t"], tabs["taq"], tabs["tbq"], tabs["tak"], tabs["tbk"]]
    aliases = {}
    if bsz < S5_BATCH:
        aliases = {len(args): 1}
        in_specs.append(pl.BlockSpec(memory_space=pl.ANY))
        args.append(jnp.zeros(u5_shape, F32))
    return pl.pallas_call(
        _inproj_kernel,
        grid=(n // tm,),
        in_specs=in_specs,
        out_specs=[row(2 * FNET_W), u5_spec, row(hw), row(hw),
                   pl.BlockSpec((1, hw, tm), lambda i: (i // nt, 0, i % nt))],
        out_shape=[jax.ShapeDtypeStruct((n, 2 * FNET_W), BF16), jax.ShapeDtypeStruct(u5_shape, F32),
                   jax.ShapeDtypeStruct((n, hw), BF16), jax.ShapeDtypeStruct((n, hw), BF16),
                   jax.ShapeDtypeStruct((bsz, hw, seq), BF16)],
        input_output_aliases=aliases,
        compiler_params=_params("parallel"),
        name="inproj",
    )(*args)


def _fnet_kernel(c_ref, s_ref, ucs_ref, out_ref, acc_ref, *, nk):
    k = pl.program_id(1)
    b = pl.program_id(2)
    u0, u1 = ucs_ref[0], ucs_ref[1]
    uc = jnp.concatenate([u0[:, :FNET_W], u1[:, :FNET_W]], axis=1)
    us = jnp.concatenate([u0[:, FNET_W:], u1[:, FNET_W:]], axis=1)
    part = _dot(c_ref[...], uc) - _dot(s_ref[...], us)

    @pl.when(k == 0)
    def _():
        acc_ref[b] = part

    @pl.when(k > 0)
    def _():
        acc_ref[b] += part

    @pl.when(k == nk - 1)
    def _():
        acc = acc_ref[b]
        out_ref[2 * b] = acc[:, :FNET_W].astype(BF16)
        out_ref[2 * b + 1] = acc[:, FNET_W:].astype(BF16)


def _fnet(ucs, tabs, bsz, seq):
    assert bsz % 2 == 0
    tm = min(1024, seq)
    tk = min(1024, seq)
    nk = seq // tk
    return pl.pallas_call(
        functools.partial(_fnet_kernel, nk=nk),
        grid=(seq // tm, nk, bsz // 2),
        in_specs=[pl.BlockSpec((tm, tk), lambda m, k, b: (m, k)),
                  pl.BlockSpec((tm, tk), lambda m, k, b: (m, k)),
                  pl.BlockSpec((2, tk, 2 * FNET_W), lambda m, k, b: (b, k, 0))],
        out_specs=pl.BlockSpec((bsz, tm, FNET_W), lambda m, k, b: (0, m, 0)),
        out_shape=jax.ShapeDtypeStruct((bsz, seq, FNET_W), BF16),
        scratch_shapes=[pltpu.VMEM((bsz // 2, tm, 2 * FNET_W), F32)],
        compiler_params=_params("parallel", "arbitrary", "arbitrary"),
        name="fnet",
    )(tabs["dft_c"], tabs["dft_s"], ucs.reshape(bsz, seq, 2 * FNET_W))


def _s5_rows(u_ref, rows):
    return jnp.concatenate([u_ref[0, pl.ds(s, rows, stride=S5_CHUNK), :] for s in range(S5_CHUNK)],
                           axis=1).astype(BF16)


def _s5_in_kernel(u_ref, bst_ref, loc_ref, *, rows):
    loc_ref[0] = _dot(_s5_rows(u_ref, rows), bst_ref[0])


def _s5_scan_kernel(loc_ref, a_ref, sp_ref, cre_ref, cim_ref, *, cb):
    d = pl.program_id(1)

    @pl.when(pl.program_id(2) == 0)
    def _():
        cre_ref[...] = jnp.zeros_like(cre_ref)
        cim_ref[...] = jnp.zeros_like(cim_ref)

    are = a_ref[0, :, :S5_HALF]
    aim = a_ref[0, :, S5_HALF:]

    def step(i, carry):
        sr, si = carry
        c = jnp.where(d == 0, i, cb - 1 - i)
        r = pl.ds(pl.multiple_of(c * S5_BATCH, S5_BATCH), S5_BATCH)
        sp_ref[0, r, :S5_HALF] = sr
        sp_ref[0, r, S5_HALF:] = si
        return (are * sr - aim * si + loc_ref[0, r, :S5_HALF],
                are * si + aim * sr + loc_ref[0, r, S5_HALF:])

    sr, si = lax.fori_loop(0, cb, step, (cre_ref[...], cim_ref[...]))
    cre_ref[...] = sr
    cim_ref[...] = si


def _s5_out_kernel(u_ref, m_ref, sp_ref, cst_ref, y_ref, *, rows):
    y = _dot(_s5_rows(u_ref, rows), m_ref[0]) + _dot(sp_ref[0].astype(BF16), cst_ref[0])
    for s in range(S5_CHUNK):
        y_ref[0, pl.ds(s, rows, stride=S5_CHUNK), :] = y[:, s * LANES:(s + 1) * LANES]


def _s5(u5, lw, seq):
    nc = seq // S5_CHUNK
    cb = min(S5_BLOCK_CHUNKS, nc)
    nblk = nc // cb
    rows = cb * S5_BATCH
    kw = S5_CHUNK * LANES
    u2 = u5.reshape(S5_TILES, nc * S5_BATCH * S5_CHUNK, LANES)
    u_spec = pl.BlockSpec((1, rows * S5_CHUNK, LANES), lambda t, j: (t, j, 0))
    wspec = lambda a, b: pl.BlockSpec((1, a, b), lambda t, j: (t, 0, 0))
    loc = pl.pallas_call(
        functools.partial(_s5_in_kernel, rows=rows),
        grid=(S5_TILES, nblk),
        in_specs=[u_spec, wspec(kw, 2 * S5_DIR)],
        out_specs=pl.BlockSpec((1, rows, 2 * S5_DIR), lambda t, j: (t, j, 0)),
        out_shape=jax.ShapeDtypeStruct((S5_TILES, nc * S5_BATCH, 2 * S5_DIR), F32),
        compiler_params=_params("parallel", "parallel"),
        name="s5_in",
    )(u2, lw["s5_bst"])
    blk = lambda t, d, j: (t, j + d * (nblk - 1 - 2 * j), d)
    sp = pl.pallas_call(
        functools.partial(_s5_scan_kernel, cb=cb),
        grid=(S5_TILES, 2, nblk),
        in_specs=[pl.BlockSpec((1, rows, S5_DIR), blk),
                  pl.BlockSpec((1, S5_BATCH, S5_DIR), lambda t, d, j: (t, 0, d))],
        out_specs=pl.BlockSpec((1, rows, S5_DIR), blk),
        out_shape=jax.ShapeDtypeStruct((S5_TILES, nc * S5_BATCH, 2 * S5_DIR), F32),
        scratch_shapes=[pltpu.VMEM((S5_BATCH, S5_HALF), F32)] * 2,
        compiler_params=_params("parallel", "arbitrary", "arbitrary"),
        name="s5_scan",
    )(loc, lw["s5_a"])
    y2 = pl.pallas_call(
        functools.partial(_s5_out_kernel, rows=rows),
        grid=(S5_TILES, nblk),
        in_specs=[u_spec, wspec(kw, kw), pl.BlockSpec((1, rows, 2 * S5_DIR), lambda t, j: (t, j, 0)),
                  wspec(2 * S5_DIR, kw)],
        out_specs=u_spec,
        out_shape=jax.ShapeDtypeStruct(u2.shape, F32),
        compiler_params=_params("parallel", "parallel"),
        name="s5_out",
    )(u2, lw["s5_m"], sp, lw["s5_cst"])
    return y2.reshape(u5.shape)


def _attn_kernel(q_ref, k_ref, vt_ref, o_ref, s_ref, m_ref, p_ref, ot_ref, *, sub):
    i = pl.program_id(2)
    tq = ot_ref.shape[1]
    units = [(u, j) for u in range(tq // sub) for j in range(2)]

    def scores(row0, j, slot):
        sl = slice(HEAD_PAD * j, HEAD_PAD * (j + 1))
        q = q_ref[0, pl.ds(pl.multiple_of(row0, sub), sub), sl]
        st = lax.dot_general(k_ref[0, :, sl], q, (((1,), (1,)), ((), ())),
                             preferred_element_type=F32)
        s_ref[slot] = st
        m_ref[slot] = jnp.max(st, axis=0, keepdims=True)

    @pl.when(i == 0)
    def _():
        scores(0, 0, 0)

    for n, (u, j) in enumerate(units):
        if n + 1 < len(units):
            scores(i * tq + units[n + 1][0] * sub, units[n + 1][1], (n + 1) % 2)
        else:
            scores(jnp.minimum(i + 1, pl.num_programs(2) - 1) * tq, 0, 0)
        p_ref[...] = jnp.exp2(s_ref[n % 2] - m_ref[n % 2]).astype(BF16)
        ol = _dot(vt_ref[0, HEAD_PAD * j:HEAD_PAD * (j + 1), :], p_ref[...])
        ot_ref[V_DIM * j:V_DIM * (j + 1), u * sub:(u + 1) * sub] = ol[:V_DIM] / ol[V_DIM:]
    o_ref[0] = ot_ref[...].T.astype(BF16)


def _attention(q, k, vt, bsz, seq):
    tq = min(ATTN_ROWS, seq)
    sub = min(ATTN_SUB, tq)
    hw = MLA_HEADS * HEAD_PAD
    return pl.pallas_call(
        functools.partial(_attn_kernel, sub=sub),
        grid=(bsz, MLA_HEADS // 2, seq // tq),
        in_specs=[pl.BlockSpec((1, seq, 2 * HEAD_PAD), lambda b, h, i: (b, 0, h)),
                  pl.BlockSpec((1, seq, 2 * HEAD_PAD), lambda b, h, i: (b, 0, h)),
                  pl.BlockSpec((1, 2 * HEAD_PAD, seq), lambda b, h, i: (b, h, 0))],
        out_specs=pl.BlockSpec((1, tq, 2 * V_DIM), lambda b, h, i: (b, i, h)),
        out_shape=jax.ShapeDtypeStruct((bsz, seq, MLA_HEADS * V_DIM), BF16),
        scratch_shapes=[pltpu.VMEM((2, seq, sub), F32), pltpu.VMEM((2, 1, sub), F32),
                        pltpu.VMEM((seq, sub), BF16), pltpu.VMEM((2 * V_DIM, tq), F32)],
        compiler_params=_params("parallel", "parallel", "arbitrary"),
        name="attention",
    )(q.reshape(bsz, seq, hw), k.reshape(bsz, seq, hw), vt)


def _merge_kernel(x_ref, yf_ref, y5_ref, o_ref, gmix_ref, wfn_ref, wglu_ref, ws5_ref, wo_ref,
                  wg_ref, wout_ref, out_ref):
    x = x_ref[...]
    h = _rms(x, gmix_ref[...]).astype(BF16)
    y_a = _dot(yf_ref[...], wfn_ref[...])
    nchunk = y5_ref.shape[1]
    ys = jnp.concatenate(
        [jnp.concatenate([y5_ref[t, c, 0] for c in range(nchunk)], axis=0) for t in range(S5_TILES)],
        axis=1)
    s = jax.nn.gelu(ys).astype(BF16)
    hg = _dot(s, wglu_ref[...])
    glu = (hg[:, :S5_W] * jax.nn.sigmoid(hg[:, S5_W:])).astype(BF16)
    y_b = _dot(glu, ws5_ref[...])
    y_c = _dot(o_ref[...], wo_ref[...])
    merged = jax.nn.sigmoid(_dot(h, wg_ref[:, :D_MODEL])) * y_a
    merged += jax.nn.sigmoid(_dot(h, wg_ref[:, D_MODEL:2 * D_MODEL])) * y_b
    merged += jax.nn.sigmoid(_dot(h, wg_ref[:, 2 * D_MODEL:])) * y_c
    out_ref[...] = x + _dot(merged.astype(BF16), wout_ref[...])


def _merge(x, yf, y5, o, lw, seq):
    n = x.shape[0]
    tm = min(512, seq)
    nt = seq // tm
    row = lambda w: pl.BlockSpec((tm, w), lambda i: (i, 0))
    y5_spec = pl.BlockSpec((S5_TILES, tm // S5_CHUNK, 1, S5_CHUNK, LANES),
                           lambda i: (0, i % nt, i // nt, 0, 0))
    return pl.pallas_call(
        _merge_kernel,
        grid=(n // tm,),
        in_specs=[row(D_MODEL), row(FNET_W), y5_spec, row(MLA_HEADS * V_DIM),
                  _const_spec((1, D_MODEL)), _const_spec((FNET_W, D_MODEL)),
                  _const_spec((S5_W, 2 * S5_W)), _const_spec((S5_W, D_MODEL)),
                  _const_spec((MLA_HEADS * V_DIM, D_MODEL)),
                  _const_spec((D_MODEL, N_BRANCH * D_MODEL)), _const_spec((D_MODEL, D_MODEL))],
        out_specs=row(D_MODEL),
        out_shape=jax.ShapeDtypeStruct((n, D_MODEL), F32),
        compiler_params=_params("parallel"),
        name="merge",
    )(x, yf, y5, o, lw["g_mix"], lw["w_fnet"], lw["w_glu"], lw["w_s5"], lw["w_o"], lw["w_gate"],
      lw["w_out"])


def _mlp_kernel(x_ref, g_ref, wup_ref, wdn_ref, gfin_ref, out_ref, *, final):
    x = x_ref[...]
    h = _rms(x, g_ref[...]).astype(BF16)
    a = jnp.square(jnp.maximum(_dot(h, wup_ref[...]), 0.0)).astype(BF16)
    y = x + _dot(a, wdn_ref[...])
    if final:
        y = _rms(y, gfin_ref[...])
    out_ref[...] = y


def _mlp(x, lw, g_final, final):
    n = x.shape[0]
    tm = min(512, n)
    resident = lambda shape: pl.BlockSpec(shape, lambda i: (0, 0), pipeline_mode=pl.Buffered(1))
    return pl.pallas_call(
        functools.partial(_mlp_kernel, final=final),
        grid=(n // tm,),
        in_specs=[pl.BlockSpec((tm, D_MODEL), lambda i: (i, 0)), _const_spec((1, D_MODEL)),
                  resident((D_MODEL, D_FF)), resident((D_FF, D_MODEL)), _const_spec((1, D_MODEL))],
        out_specs=pl.BlockSpec((tm, D_MODEL), lambda i: (i, 0)),
        out_shape=jax.ShapeDtypeStruct((n, D_MODEL), F32),
        compiler_params=_params("parallel"),
        name="mlp",
    )(x, lw["g_mlp"], lw["w_up"], lw["w_down"], g_final)


def _tables(seq):
    half = QK_ROPE // 2
    inv = ROPE_BASE ** (-jnp.arange(half, dtype=F32) / half)
    ang = jnp.arange(seq, dtype=F32)[:, None] * inv[None, :]
    cos, sin = jnp.cos(ang), jnp.sin(ang)
    one = jnp.ones((seq, QK_NOPE), F32)
    z64 = jnp.zeros((seq, QK_NOPE), F32)
    z32 = jnp.zeros((seq, HEAD_PAD - QK_NOPE - QK_ROPE), F32)
    scale = (QK_NOPE + QK_ROPE) ** -0.5 * math.log2(math.e)
    ta = jnp.concatenate([z64, cos, cos, z32], axis=1)
    tb = jnp.concatenate([z64, -sin, sin, z32], axis=1)
    taq = jnp.concatenate([one, cos, cos, z32], axis=1) * scale
    c = jnp.arange(FNET_GROUP_DIM)
    ang64 = (2.0 * math.pi / FNET_GROUP_DIM) * ((c[:, None] * c[None, :]) % FNET_GROUP_DIM).astype(F32)
    eye = jnp.eye(FNET_GROUPS, dtype=F32)
    norm = 1.0 / math.sqrt(seq * FNET_GROUP_DIM)
    cs = jnp.concatenate([jnp.kron(eye, jnp.cos(ang64)), jnp.kron(eye, jnp.sin(ang64))], axis=1) * norm
    r = LANES if seq % LANES == 0 else 1
    k = jnp.arange(seq)[:, None]
    w = 2.0 * math.pi / seq
    a1 = w * ((k * (jnp.arange(seq // r)[None, :] * r)) % seq).astype(F32)
    a2 = w * ((k * jnp.arange(r)[None, :]) % seq).astype(F32)
    c1, s1 = jnp.cos(a1)[:, :, None], jnp.sin(a1)[:, :, None]
    c2, s2 = jnp.cos(a2)[:, None, :], jnp.sin(a2)[:, None, :]
    dft_c = (c1 * c2 - s1 * s2).reshape(seq, seq).astype(BF16)
    dft_s = (s1 * c2 + c1 * s2).reshape(seq, seq).astype(BF16)
    return dict(taq=taq, tbq=tb * scale, tak=ta, tbk=tb, cs=cs.astype(BF16), dft_c=dft_c, dft_s=dft_s)


def _pad_heads(w, width):
    k = w.shape[0]
    w = w.reshape(k, MLA_HEADS, width)
    return jnp.pad(w, ((0, 0), (0, 0), (0, HEAD_PAD - width))).reshape(k, MLA_HEADS * HEAD_PAD)


def _cmul(ar, ai, br, bi):
    return ar * br - ai * bi, ar * bi + ai * br


def _s5_tables(lam_re, lam_im, log_dt, b_re, b_im, c_re, c_im, d_skip):
    t, p, g, ns = S5_CHUNK, S5_GROUP_DIM, S5_GROUPS, S5_STATE
    nt, tg = S5_TILES, S5_TILE_GROUPS
    dt = jnp.exp(log_dt)[..., None]
    ar, ai = lam_re * dt, lam_im * dt
    mag = jnp.exp(ar)
    lbr, lbi = mag * jnp.cos(ai), mag * jnp.sin(ai)
    den = lam_re * lam_re + lam_im * lam_im
    cfr = ((lbr - 1.0) * lam_re + lbi * lam_im) / den
    cfi = (lbi * lam_re - (lbr - 1.0) * lam_im) / den
    bbr, bbi = _cmul(cfr[..., None], cfi[..., None], b_re, b_im)
    d = jnp.arange(t + 1, dtype=F32)
    pmag = jnp.exp(ar[..., None] * d)
    pwr, pwi = pmag * jnp.cos(ai[..., None] * d), pmag * jnp.sin(ai[..., None] * d)
    cpr, cpi = _cmul(c_re[..., None], c_im[..., None], pwr[:, :, None, :, :t], pwi[:, :, None, :, :t])
    kern = (jnp.einsum('xgpnd,xgnq->xgdpq', cpr, bbr)
            - jnp.einsum('xgpnd,xgnq->xgdpq', cpi, bbi))
    skip = jnp.eye(p, dtype=F32)[None] * d_skip.reshape(g, p)[:, :, None]
    k0 = kern[0][:, :1] + kern[1][:, :1] + skip[:, None]
    kfull = jnp.concatenate([kern[1][:, :0:-1], k0, kern[0][:, 1:]], axis=1)
    eye = jnp.eye(tg, dtype=F32)
    kq = kfull.transpose(0, 1, 3, 2).reshape(nt, tg, 2 * t - 1, p, p)
    bd = jnp.einsum('Ggdqp,gh->Gdgqhp', kq, eye).reshape(nt, 2 * t - 1, LANES, LANES)
    lag = jnp.arange(t)[None, :] - jnp.arange(t)[:, None] + (t - 1)
    m = bd[:, lag].transpose(0, 1, 3, 2, 4).reshape(nt, t * LANES, t * LANES)
    rev = lambda v: v[..., ::-1]
    lay_p = lambda v: v.reshape(2, nt, tg * ns, t).transpose(0, 1, 3, 2)
    psr = lay_p(jnp.stack([rev(pwr[0])[..., 1:], pwr[1][..., :t]]))
    psi = lay_p(jnp.stack([rev(pwi[0])[..., 1:], pwi[1][..., :t]]))
    lay_b = lambda v: jnp.einsum('xGhnq,gh->xGgqhn', v.reshape(2, nt, tg, ns, p), eye).reshape(
        2, nt, LANES, tg * ns)
    bmr, bmi = lay_b(bbr), lay_b(bbi)
    bsr, bsi = _cmul(psr[:, :, :, None, :], psi[:, :, :, None, :], bmr[:, :, None], bmi[:, :, None])
    bst = jnp.stack([bsr, bsi], axis=1).transpose(2, 3, 4, 0, 1, 5).reshape(nt, t * LANES, 2 * S5_DIR)
    lay_c = lambda v: jnp.einsum('xGhpn,gh->xGgnhp', v.reshape(2, nt, tg, p, ns), eye).reshape(
        2, nt, tg * ns, LANES)
    cmr, cmi = lay_c(c_re), lay_c(c_im)
    por = jnp.stack([pwr[0][..., 1:], rev(pwr[1])[..., :t]]).reshape(2, nt, tg * ns, t)
    poi = jnp.stack([pwi[0][..., 1:], rev(pwi[1])[..., :t]]).reshape(2, nt, tg * ns, t)
    csr, csi = _cmul(cmr[:, :, :, None, :], cmi[:, :, :, None, :], por[..., None], poi[..., None])
    cst = jnp.stack([csr, -csi], axis=1).transpose(2, 0, 1, 3, 4, 5).reshape(nt, 2 * S5_DIR, t * LANES)
    a = jnp.stack([pwr[..., t], pwi[..., t]], axis=1)
    a = a.reshape(2, 2, nt, tg, ns).transpose(2, 0, 1, 3, 4).reshape(nt, 1, 2 * S5_DIR)
    a = jnp.broadcast_to(a, (nt, S5_BATCH, 2 * S5_DIR))
    return dict(s5_m=m.astype(BF16), s5_bst=bst.astype(BF16), s5_cst=cst.astype(BF16), s5_a=a.astype(F32))


def _layer_weights(i, p):
    w_in = p["w_in"][i]
    k_dim = w_in.shape[0]
    half = QK_ROPE // 2
    kr = w_in[:, OFF_KR:OFF_GATE]
    z64 = jnp.zeros((k_dim, QK_NOPE), F32)
    z32 = jnp.zeros((k_dim, HEAD_PAD - QK_NOPE - QK_ROPE), F32)
    w_a = jnp.concatenate([w_in[:, :OFF_KR], z64, kr, z32, z64, kr[:, half:], kr[:, :half], z32], axis=1)
    wq = p["w_qb"][i].reshape(Q_LORA, MLA_HEADS, QK_NOPE + QK_ROPE)
    rope = wq[:, :, QK_NOPE:]
    wq_b = jnp.concatenate([jnp.zeros_like(wq[:, :, :QK_NOPE]), rope[:, :, half:], rope[:, :, :half]], axis=2)
    wkv = p["w_kvb"][i].reshape(KV_LORA, MLA_HEADS, QK_NOPE + V_DIM)
    row = lambda v: v.reshape(1, -1).astype(F32)
    lw = dict(
        g_mix=row(p["g_mix"][i]), w_a=w_a.astype(BF16), w_gate=w_in[:, OFF_GATE:].astype(BF16),
        g_q=row(p["g_q"][i]), g_kv=row(p["g_kv"][i]),
        wq_a=_pad_heads(p["w_qb"][i], QK_NOPE + QK_ROPE).astype(BF16),
        wq_b=_pad_heads(wq_b.reshape(Q_LORA, -1), QK_NOPE + QK_ROPE).astype(BF16),
        wk=_pad_heads(wkv[:, :, :QK_NOPE].reshape(KV_LORA, -1), QK_NOPE).astype(BF16),
        wv_t=_pad_heads(wkv[:, :, QK_NOPE:].reshape(KV_LORA, -1), V_DIM).T.astype(BF16),
        w_fnet=p["w_fnet"][i].astype(BF16), w_glu=p["w_glu"][i].astype(BF16),
        w_s5=p["w_s5"][i].astype(BF16), w_o=p["w_o_mla"][i].astype(BF16),
        w_out=p["w_out"][i].astype(BF16), g_mlp=row(p["g_mlp"][i]),
        w_up=p["w_up"][i].astype(BF16), w_down=p["w_down"][i].astype(BF16))
    lw.update(_s5_tables(p["s5_lam_re"][i], p["s5_lam_im"][i], p["s5_log_dt"][i], p["s5_b_re"][i],
                         p["s5_b_im"][i], p["s5_c_re"][i], p["s5_c_im"][i], p["s5_d"][i]))
    return lw


def _trunk(x, layers, tabs, g_final):
    bsz, seq, _ = x.shape
    assert bsz <= S5_BATCH and seq % S5_CHUNK == 0
    x = x.reshape(bsz * seq, D_MODEL)
    for i, lw in enumerate(layers):
        ucs, u5, q, k, vt = _inproj(x, lw, tabs, bsz, seq)
        yf = _fnet(ucs, tabs, bsz, seq).reshape(bsz * seq, FNET_W)
        y5 = _s5(u5, lw, seq)
        o = _attention(q, k, vt, bsz, seq).reshape(bsz * seq, MLA_HEADS * V_DIM)
        x = _merge(x, yf, y5, o, lw, seq)
        x = _mlp(x, lw, g_final, final=(i == len(layers) - 1))
    return x.reshape(bsz, seq, D_MODEL)


def kernel(x_prompt, x_sample, g_mix, w_in, w_fnet, s5_lam_re, s5_lam_im, s5_log_dt, s5_b_re,
           s5_b_im, s5_c_re, s5_c_im, s5_d, w_glu, w_s5, g_q, w_qb, g_kv, w_kvb, w_o_mla,
           w_out, g_mlp, w_up, w_down, g_final):
    p = dict(g_mix=g_mix, w_in=w_in, w_fnet=w_fnet, s5_lam_re=s5_lam_re, s5_lam_im=s5_lam_im,
             s5_log_dt=s5_log_dt, s5_b_re=s5_b_re, s5_b_im=s5_b_im, s5_c_re=s5_c_re,
             s5_c_im=s5_c_im, s5_d=s5_d, w_glu=w_glu, w_s5=w_s5, g_q=g_q, w_qb=w_qb, g_kv=g_kv,
             w_kvb=w_kvb, w_o_mla=w_o_mla, w_out=w_out, g_mlp=g_mlp, w_up=w_up, w_down=w_down)
    layers = [_layer_weights(i, p) for i in range(g_mix.shape[0])]
    gfin = g_final.reshape(1, -1).astype(F32)
    outs = []
    for x in (x_prompt, x_sample):
        tabs = _tables(x.shape[1])
        outs.append(_trunk(x, layers, tabs, gfin))
    return tuple(outs)
```

```python
import functools
import math

import jax
import jax.numpy as jnp
from jax import lax
from jax.experimental import pallas as pl
from jax.experimental.pallas import tpu as pltpu

F32 = jnp.float32
BF16 = jnp.bfloat16

D_MODEL = 1024
FNET_GROUP_DIM = 64
FNET_GROUPS = 6
FNET_W = 384
S5_GROUP_DIM = 16
S5_GROUPS = 24
S5_W = 384
S5_STATE = 64
MLA_HEADS = 16
QK_NOPE = 64
QK_ROPE = 32
V_DIM = 64
Q_LORA = 384
KV_LORA = 256
ROPE_BASE = 10000.0
N_BRANCH = 3
D_FF = 4 * D_MODEL
EPS = 1e-6

OFF_FNET = 0
OFF_S5 = OFF_FNET + FNET_W
OFF_Q = OFF_S5 + S5_W
OFF_KV = OFF_Q + Q_LORA
OFF_KR = OFF_KV + KV_LORA
OFF_GATE = OFF_KR + QK_ROPE

LANES = 128
SUBLANES = 8
HEAD_PAD = 128
VMEM_LIMIT = 56 * 1024 * 1024

S5_CHUNK = SUBLANES
S5_TILES = S5_W // LANES
S5_TILE_GROUPS = LANES // S5_GROUP_DIM
S5_HALF = S5_TILE_GROUPS * S5_STATE
S5_DIR = 2 * S5_HALF
S5_BATCH = SUBLANES
S5_BLOCK_CHUNKS = 64
FNET_R2 = FNET_GROUP_DIM

ZA_CQ = OFF_Q
ZA_CKV = OFF_KV
ZA_KRA = OFF_KR
ZA_KRB = OFF_KR + LANES
ZA_W = ZA_KRB + LANES

ATTN_ROWS = 1024
ATTN_SUB = 512


def _rms(x, g):
    return x * lax.rsqrt(jnp.mean(x * x, axis=-1, keepdims=True) + EPS) * g


def _dot(a, b):
    return jnp.dot(a, b, preferred_element_type=F32)


def _params(*sem):
    return pltpu.CompilerParams(dimension_semantics=sem, vmem_limit_bytes=VMEM_LIMIT)


def _const_spec(shape):
    nd = len(shape)
    return pl.BlockSpec(shape, lambda *_: (0,) * nd)


def _inproj_kernel(x_ref, gmix_ref, wa_ref, cs_ref, gq_ref, wqa_ref, wqb_ref, gkv_ref, wk_ref,
                   wv_ref, taq_ref, tbq_ref, tak_ref, tbk_ref, *rest):
    ucs_ref, u5_ref, q_ref, k_ref, v_ref = rest[-5:]
    h = _rms(x_ref[...], gmix_ref[...]).astype(BF16)
    z = _dot(h, wa_ref[...])
    ucs_ref[...] = _dot(z[:, OFF_FNET:OFF_S5].astype(BF16), cs_ref[...])
    for t in range(S5_TILES):
        for c in range(z.shape[0] // S5_CHUNK):
            u5_ref[t, c, 0] = z[c * S5_CHUNK:(c + 1) * S5_CHUNK, OFF_S5 + t * LANES:OFF_S5 + (t + 1) * LANES]
    hq = _rms(z[:, ZA_CQ:ZA_CQ + Q_LORA], gq_ref[...]).astype(BF16)
    hkv = _rms(z[:, ZA_CKV:ZA_CKV + KV_LORA], gkv_ref[...]).astype(BF16)
    kpe = z[:, ZA_KRA:ZA_KRA + LANES] * tak_ref[...] + z[:, ZA_KRB:ZA_KRB + LANES] * tbk_ref[...]
    kpe2 = jnp.concatenate([kpe, kpe], axis=1)
    taq2 = jnp.concatenate([taq_ref[...]] * 2, axis=1)
    tbq2 = jnp.concatenate([tbq_ref[...]] * 2, axis=1)
    for j in range(MLA_HEADS // 2):
        sl = slice(2 * HEAD_PAD * j, 2 * HEAD_PAD * (j + 1))
        qa = _dot(hq, wqa_ref[:, sl])
        qb = _dot(hq, wqb_ref[:, sl])
        q_ref[:, sl] = (qa * taq2 + qb * tbq2).astype(BF16)
        k_ref[:, sl] = (_dot(hkv, wk_ref[:, sl]) + kpe2).astype(BF16)
    vt = lax.dot_general(wv_ref[...], hkv, (((1,), (1,)), ((), ())), preferred_element_type=F32)
    ones_row = (lax.broadcasted_iota(jnp.int32, vt.shape, 0) & V_DIM) != 0
    v_ref[0] = jnp.where(ones_row, 1.0, vt).astype(BF16)


def _inproj(x, lw, tabs, bsz, seq):
    n = x.shape[0]
    tm = min(512, seq)
    nt = seq // tm
    nc = seq // S5_CHUNK
    row = lambda w: pl.BlockSpec((tm, w), lambda i: (i, 0))
    tab = pl.BlockSpec((tm, LANES), lambda i: (i % nt, 0))
    hw = MLA_HEADS * HEAD_PAD
    u5_shape = (S5_TILES, nc, S5_BATCH, S5_CHUNK, LANES)
    u5_spec = pl.BlockSpec((S5_TILES, tm // S5_CHUNK, 1, S5_CHUNK, LANES),
                           lambda i: (0, i % nt, i // nt, 0, 0))
    in_specs = [row(D_MODEL), _const_spec((1, D_MODEL)), _const_spec((D_MODEL, ZA_W)),
                _const_spec((FNET_W, 2 * FNET_W)), _const_spec((1, Q_LORA)),
                _const_spec((Q_LORA, hw)), _const_spec((Q_LORA, hw)), _const_spec((1, KV_LORA)),
                _const_spec((KV_LORA, hw)), _const_spec((hw, KV_LORA)),
                tab, tab, tab, tab]
    args = [x, lw["g_mix"], lw["w_a"], tabs["cs"], lw["g_q"], lw["wq_a"], lw["wq_b"], lw["g_kv"],
            lw["wk"], lw["wv_t"], tabs["taq"], tabs["tbq"], tabs["tak"], tabs["tbk"]]
    aliases = {}
    if bsz < S5_BATCH:
        aliases = {len(args): 1}
        in_specs.append(pl.BlockSpec(memory_space=pl.ANY))
        args.append(jnp.zeros(u5_shape, F32))
    return pl.pallas_call(
        _inproj_kernel,
        grid=(n // tm,),
        in_specs=in_specs,
        out_specs=[row(2 * FNET_W), u5_spec, row(hw), row(hw),
                   pl.BlockSpec((1, hw, tm), lambda i: (i // nt, 0, i % nt))],
        out_shape=[jax.ShapeDtypeStruct((n, 2 * FNET_W), F32), jax.ShapeDtypeStruct(u5_shape, F32),
                   jax.ShapeDtypeStruct((n, hw), BF16), jax.ShapeDtypeStruct((n, hw), BF16),
                   jax.ShapeDtypeStruct((bsz, hw, seq), BF16)],
        input_output_aliases=aliases,
        compiler_params=_params("parallel"),
        name="inproj",
    )(*args)


def _fnet_kernel(uc_ref, us_ref, f1_ref, f2_ref, twc_ref, tws_ref, out_ref, ar_ref, ai_ref, *, r1):
    f1 = f1_ref[...]
    for n2 in range(FNET_R2):
        rows = pl.ds(n2, r1, stride=FNET_R2)
        u = jnp.concatenate([uc_ref[0, rows, :], us_ref[0, rows, :]], axis=1).astype(BF16)
        pr = _dot(f1, u)
        ar = pr[:r1, :LANES] - pr[r1:, LANES:]
        ai = -(pr[:r1, LANES:] + pr[r1:, :LANES])
        c, s = twc_ref[n2], tws_ref[n2]
        ar_ref[n2 * r1:(n2 + 1) * r1, :] = ar * c + ai * s
        ai_ref[n2 * r1:(n2 + 1) * r1, :] = ai * c - ar * s
    f2 = f2_ref[...]
    for k1 in range(r1):
        rows = pl.ds(k1, FNET_R2, stride=r1)
        gk = jnp.concatenate([ar_ref[rows, :], ai_ref[rows, :]], axis=0).astype(BF16)
        out_ref[0, rows, :] = _dot(f2, gk)


def _fnet(ucs, tabs, bsz, seq):
    r1 = seq // FNET_R2
    nt = FNET_W // LANES
    blk = lambda off: pl.BlockSpec((1, seq, LANES), lambda b, c: (b, 0, c + off))
    return pl.pallas_call(
        functools.partial(_fnet_kernel, r1=r1),
        grid=(bsz, nt),
        in_specs=[blk(0), blk(nt), _const_spec((2 * r1, r1)), _const_spec((FNET_R2, 2 * FNET_R2)),
                  _const_spec((FNET_R2, r1, LANES)), _const_spec((FNET_R2, r1, LANES))],
        out_specs=blk(0),
        out_shape=jax.ShapeDtypeStruct((bsz, seq, FNET_W), F32),
        scratch_shapes=[pltpu.VMEM((seq, LANES), F32)] * 2,
        compiler_params=_params("parallel", "parallel"),
        name="fnet",
    )(ucs, ucs, tabs["f1"], tabs["f2"], tabs["twc"], tabs["tws"])


def _s5_rows(u_ref, rows):
    return jnp.concatenate([u_ref[0, pl.ds(s, rows, stride=S5_CHUNK), :] for s in range(S5_CHUNK)],
                           axis=1).astype(BF16)


def _s5_in_kernel(u_ref, bst_ref, loc_ref, *, rows):
    loc_ref[0] = _dot(_s5_rows(u_ref, rows), bst_ref[0])


def _s5_scan_kernel(loc_ref, a_ref, sp_ref, cre_ref, cim_ref, *, cb):
    d = pl.program_id(1)

    @pl.when(pl.program_id(2) == 0)
    def _():
        cre_ref[...] = jnp.zeros_like(cre_ref)
        cim_ref[...] = jnp.zeros_like(cim_ref)

    are = a_ref[0, :, :S5_HALF]
    aim = a_ref[0, :, S5_HALF:]

    def step(i, carry):
        sr, si = carry
        c = jnp.where(d == 0, i, cb - 1 - i)
        r = pl.ds(pl.multiple_of(c * S5_BATCH, S5_BATCH), S5_BATCH)
        sp_ref[0, r, :S5_HALF] = sr
        sp_ref[0, r, S5_HALF:] = si
        return (are * sr - aim * si + loc_ref[0, r, :S5_HALF],
                are * si + aim * sr + loc_ref[0, r, S5_HALF:])

    sr, si = lax.fori_loop(0, cb, step, (cre_ref[...], cim_ref[...]))
    cre_ref[...] = sr
    cim_ref[...] = si


def _s5_out_kernel(u_ref, m_ref, sp_ref, cst_ref, y_ref, *, rows):
    y = _dot(_s5_rows(u_ref, rows), m_ref[0]) + _dot(sp_ref[0].astype(BF16), cst_ref[0])
    for s in range(S5_CHUNK):
        y_ref[0, pl.ds(s, rows, stride=S5_CHUNK), :] = y[:, s * LANES:(s + 1) * LANES]


def _s5(u5, lw, seq):
    nc = seq // S5_CHUNK
    cb = min(S5_BLOCK_CHUNKS, nc)
    nblk = nc // cb
    rows = cb * S5_BATCH
    kw = S5_CHUNK * LANES
    u2 = u5.reshape(S5_TILES, nc * S5_BATCH * S5_CHUNK, LANES)
    u_spec = pl.BlockSpec((1, rows * S5_CHUNK, LANES), lambda t, j: (t, j, 0))
    wspec = lambda a, b: pl.BlockSpec((1, a, b), lambda t, j: (t, 0, 0))
    loc = pl.pallas_call(
        functools.partial(_s5_in_kernel, rows=rows),
        grid=(S5_TILES, nblk),
        in_specs=[u_spec, wspec(kw, 2 * S5_DIR)],
        out_specs=pl.BlockSpec((1, rows, 2 * S5_DIR), lambda t, j: (t, j, 0)),
        out_shape=jax.ShapeDtypeStruct((S5_TILES, nc * S5_BATCH, 2 * S5_DIR), F32),
        compiler_params=_params("parallel", "parallel"),
        name="s5_in",
    )(u2, lw["s5_bst"])
    blk = lambda t, d, j: (t, j + d * (nblk - 1 - 2 * j), d)
    sp = pl.pallas_call(
        functools.partial(_s5_scan_kernel, cb=cb),
        grid=(S5_TILES, 2, nblk),
        in_specs=[pl.BlockSpec((1, rows, S5_DIR), blk),
                  pl.BlockSpec((1, S5_BATCH, S5_DIR), lambda t, d, j: (t, 0, d))],
        out_specs=pl.BlockSpec((1, rows, S5_DIR), blk),
        out_shape=jax.ShapeDtypeStruct((S5_TILES, nc * S5_BATCH, 2 * S5_DIR), F32),
        scratch_shapes=[pltpu.VMEM((S5_BATCH, S5_HALF), F32)] * 2,
        compiler_params=_params("parallel", "arbitrary", "arbitrary"),
        name="s5_scan",
    )(loc, lw["s5_a"])
    y2 = pl.pallas_call(
        functools.partial(_s5_out_kernel, rows=rows),
        grid=(S5_TILES, nblk),
        in_specs=[u_spec, wspec(kw, kw), pl.BlockSpec((1, rows, 2 * S5_DIR), lambda t, j: (t, j, 0)),
                  wspec(2 * S5_DIR, kw)],
        out_specs=u_spec,
        out_shape=jax.ShapeDtypeStruct(u2.shape, F32),
        compiler_params=_params("parallel", "parallel"),
        name="s5_out",
    )(u2, lw["s5_m"], sp, lw["s5_cst"])
    return y2.reshape(u5.shape)


def _attn_kernel(q_ref, k_ref, vt_ref, o_ref, s_ref, m_ref, p_ref, ot_ref, *, sub):
    units = [(u, j) for u in range(q_ref.shape[1] // sub) for j in range(2)]

    def scores(n):
        u, j = units[n]
        sl = slice(HEAD_PAD * j, HEAD_PAD * (j + 1))
        st = lax.dot_general(k_ref[0, :, sl], q_ref[0, u * sub:(u + 1) * sub, sl],
                             (((1,), (1,)), ((), ())), preferred_element_type=F32)
        s_ref[n % 2] = st
        m_ref[n % 2] = jnp.max(st, axis=0, keepdims=True)

    scores(0)
    for n, (u, j) in enumerate(units):
        if n + 1 < len(units):
            scores(n + 1)
        p_ref[...] = jnp.exp2(s_ref[n % 2] - m_ref[n % 2]).astype(BF16)
        ol = _dot(vt_ref[0, HEAD_PAD * j:HEAD_PAD * (j + 1), :], p_ref[...])
        ot_ref[V_DIM * j:V_DIM * (j + 1), u * sub:(u + 1) * sub] = ol[:V_DIM] / ol[V_DIM:]
    o_ref[0] = ot_ref[...].T.astype(BF16)


def _attention(q, k, vt, bsz, seq):
    tq = min(ATTN_ROWS, seq)
    sub = min(ATTN_SUB, tq)
    hw = MLA_HEADS * HEAD_PAD
    return pl.pallas_call(
        functools.partial(_attn_kernel, sub=sub),
        grid=(bsz, MLA_HEADS // 2, seq // tq),
        in_specs=[pl.BlockSpec((1, tq, 2 * HEAD_PAD), lambda b, h, i: (b, i, h)),
                  pl.BlockSpec((1, seq, 2 * HEAD_PAD), lambda b, h, i: (b, 0, h)),
                  pl.BlockSpec((1, 2 * HEAD_PAD, seq), lambda b, h, i: (b, h, 0))],
        out_specs=pl.BlockSpec((1, tq, 2 * V_DIM), lambda b, h, i: (b, i, h)),
        out_shape=jax.ShapeDtypeStruct((bsz, seq, MLA_HEADS * V_DIM), BF16),
        scratch_shapes=[pltpu.VMEM((2, seq, sub), F32), pltpu.VMEM((2, 1, sub), F32),
                        pltpu.VMEM((seq, sub), BF16), pltpu.VMEM((2 * V_DIM, tq), F32)],
        compiler_params=_params("parallel", "parallel", "arbitrary"),
        name="attention",
    )(q.reshape(bsz, seq, hw), k.reshape(bsz, seq, hw), vt)


def _merge_kernel(x_ref, yf_ref, y5_ref, o_ref, gmix_ref, wfn_ref, wglu_ref, ws5_ref, wo_ref,
                  wg_ref, wout_ref, out_ref):
    x = x_ref[...]
    h = _rms(x, gmix_ref[...]).astype(BF16)
    y_a = _dot(yf_ref[...].astype(BF16), wfn_ref[...])
    nchunk = y5_ref.shape[1]
    ys = jnp.concatenate(
        [jnp.concatenate([y5_ref[t, c, 0] for c in range(nchunk)], axis=0) for t in range(S5_TILES)],
        axis=1)
    s = jax.nn.gelu(ys).astype(BF16)
    hg = _dot(s, wglu_ref[...])
    glu = (hg[:, :S5_W] * jax.nn.sigmoid(hg[:, S5_W:])).astype(BF16)
    y_b = _dot(glu, ws5_ref[...])
    y_c = _dot(o_ref[...], wo_ref[...])
    merged = jax.nn.sigmoid(_dot(h, wg_ref[:, :D_MODEL])) * y_a
    merged += jax.nn.sigmoid(_dot(h, wg_ref[:, D_MODEL:2 * D_MODEL])) * y_b
    merged += jax.nn.sigmoid(_dot(h, wg_ref[:, 2 * D_MODEL:])) * y_c
    out_ref[...] = x + _dot(merged.astype(BF16), wout_ref[...])


def _merge(x, yf, y5, o, lw, seq):
    n = x.shape[0]
    tm = min(512, seq)
    nt = seq // tm
    row = lambda w: pl.BlockSpec((tm, w), lambda i: (i, 0))
    y5_spec = pl.BlockSpec((S5_TILES, tm // S5_CHUNK, 1, S5_CHUNK, LANES),
                           lambda i: (0, i % nt, i // nt, 0, 0))
    return pl.pallas_call(
        _merge_kernel,
        grid=(n // tm,),
        in_specs=[row(D_MODEL), row(FNET_W), y5_spec, row(MLA_HEADS * V_DIM),
                  _const_spec((1, D_MODEL)), _const_spec((FNET_W, D_MODEL)),
                  _const_spec((S5_W, 2 * S5_W)), _const_spec((S5_W, D_MODEL)),
                  _const_spec((MLA_HEADS * V_DIM, D_MODEL)),
                  _const_spec((D_MODEL, N_BRANCH * D_MODEL)), _const_spec((D_MODEL, D_MODEL))],
        out_specs=row(D_MODEL),
        out_shape=jax.ShapeDtypeStruct((n, D_MODEL), F32),
        compiler_params=_params("parallel"),
        name="merge",
    )(x, yf, y5, o, lw["g_mix"], lw["w_fnet"], lw["w_glu"], lw["w_s5"], lw["w_o"], lw["w_gate"],
      lw["w_out"])


def _mlp_kernel(x_ref, g_ref, wup_ref, wdn_ref, gfin_ref, out_ref, *, final):
    x = x_ref[...]
    h = _rms(x, g_ref[...]).astype(BF16)
    a = jnp.square(jnp.maximum(_dot(h, wup_ref[...]), 0.0)).astype(BF16)
    y = x + _dot(a, wdn_ref[...])
    if final:
        y = _rms(y, gfin_ref[...])
    out_ref[...] = y


def _mlp(x, lw, g_final, final):
    n = x.shape[0]
    tm = min(512, n)
    resident = lambda shape: pl.BlockSpec(shape, lambda i: (0, 0), pipeline_mode=pl.Buffered(1))
    return pl.pallas_call(
        functools.partial(_mlp_kernel, final=final),
        grid=(n // tm,),
        in_specs=[pl.BlockSpec((tm, D_MODEL), lambda i: (i, 0)), _const_spec((1, D_MODEL)),
                  resident((D_MODEL, D_FF)), resident((D_FF, D_MODEL)), _const_spec((1, D_MODEL))],
        out_specs=pl.BlockSpec((tm, D_MODEL), lambda i: (i, 0)),
        out_shape=jax.ShapeDtypeStruct((n, D_MODEL), F32),
        compiler_params=_params("parallel"),
        name="mlp",
    )(x, lw["g_mlp"], lw["w_up"], lw["w_down"], g_final)


def _tables(seq):
    half = QK_ROPE // 2
    inv = ROPE_BASE ** (-jnp.arange(half, dtype=F32) / half)
    ang = jnp.arange(seq, dtype=F32)[:, None] * inv[None, :]
    cos, sin = jnp.cos(ang), jnp.sin(ang)
    one = jnp.ones((seq, QK_NOPE), F32)
    z64 = jnp.zeros((seq, QK_NOPE), F32)
    z32 = jnp.zeros((seq, HEAD_PAD - QK_NOPE - QK_ROPE), F32)
    scale = (QK_NOPE + QK_ROPE) ** -0.5 * math.log2(math.e)
    ta = jnp.concatenate([z64, cos, cos, z32], axis=1)
    tb = jnp.concatenate([z64, -sin, sin, z32], axis=1)
    taq = jnp.concatenate([one, cos, cos, z32], axis=1) * scale
    c = jnp.arange(FNET_GROUP_DIM)
    ang64 = (2.0 * math.pi / FNET_GROUP_DIM) * ((c[:, None] * c[None, :]) % FNET_GROUP_DIM).astype(F32)
    eye = jnp.eye(FNET_GROUPS, dtype=F32)
    norm = 1.0 / math.sqrt(seq * FNET_GROUP_DIM)
    cs = jnp.concatenate([jnp.kron(eye, jnp.cos(ang64)), jnp.kron(eye, jnp.sin(ang64))], axis=1) * norm
    r1 = seq // FNET_R2
    i1 = jnp.arange(r1)
    a1 = (2.0 * math.pi / r1) * ((i1[:, None] * i1[None, :]) % r1).astype(F32)
    f1 = jnp.concatenate([jnp.cos(a1), jnp.sin(a1)], axis=0).astype(BF16)
    f2 = jnp.concatenate([jnp.cos(ang64), jnp.sin(ang64)], axis=1).astype(BF16)
    atw = (2.0 * math.pi / seq) * (c[:, None] * i1[None, :]).astype(F32)
    twc = jnp.broadcast_to(jnp.cos(atw)[:, :, None], (FNET_R2, r1, LANES))
    tws = jnp.broadcast_to(jnp.sin(atw)[:, :, None], (FNET_R2, r1, LANES))
    return dict(taq=taq, tbq=tb * scale, tak=ta, tbk=tb, cs=cs.astype(BF16), f1=f1, f2=f2, twc=twc, tws=tws)


def _pad_heads(w, width):
    k = w.shape[0]
    w = w.reshape(k, MLA_HEADS, width)
    return jnp.pad(w, ((0, 0), (0, 0), (0, HEAD_PAD - width))).reshape(k, MLA_HEADS * HEAD_PAD)


def _cmul(ar, ai, br, bi):
    return ar * br - ai * bi, ar * bi + ai * br


def _s5_tables(lam_re, lam_im, log_dt, b_re, b_im, c_re, c_im, d_skip):
    t, p, g, ns = S5_CHUNK, S5_GROUP_DIM, S5_GROUPS, S5_STATE
    nt, tg = S5_TILES, S5_TILE_GROUPS
    dt = jnp.exp(log_dt)[..., None]
    ar, ai = lam_re * dt, lam_im * dt
    mag = jnp.exp(ar)
    lbr, lbi = mag * jnp.cos(ai), mag * jnp.sin(ai)
    den = lam_re * lam_re + lam_im * lam_im
    cfr = ((lbr - 1.0) * lam_re + lbi * lam_im) / den
    cfi = (lbi * lam_re - (lbr - 1.0) * lam_im) / den
    bbr, bbi = _cmul(cfr[..., None], cfi[..., None], b_re, b_im)
    d = jnp.arange(t + 1, dtype=F32)
    pmag = jnp.exp(ar[..., None] * d)
    pwr, pwi = pmag * jnp.cos(ai[..., None] * d), pmag * jnp.sin(ai[..., None] * d)
    cpr, cpi = _cmul(c_re[..., None], c_im[..., None], pwr[:, :, None, :, :t], pwi[:, :, None, :, :t])
    kern = (jnp.einsum('xgpnd,xgnq->xgdpq', cpr, bbr)
            - jnp.einsum('xgpnd,xgnq->xgdpq', cpi, bbi))
    skip = jnp.eye(p, dtype=F32)[None] * d_skip.reshape(g, p)[:, :, None]
    k0 = kern[0][:, :1] + kern[1][:, :1] + skip[:, None]
    kfull = jnp.concatenate([kern[1][:, :0:-1], k0, kern[0][:, 1:]], axis=1)
    eye = jnp.eye(tg, dtype=F32)
    kq = kfull.transpose(0, 1, 3, 2).reshape(nt, tg, 2 * t - 1, p, p)
    bd = jnp.einsum('Ggdqp,gh->Gdgqhp', kq, eye).reshape(nt, 2 * t - 1, LANES, LANES)
    lag = jnp.arange(t)[None, :] - jnp.arange(t)[:, None] + (t - 1)
    m = bd[:, lag].transpose(0, 1, 3, 2, 4).reshape(nt, t * LANES, t * LANES)
    rev = lambda v: v[..., ::-1]
    lay_p = lambda v: v.reshape(2, nt, tg * ns, t).transpose(0, 1, 3, 2)
    psr = lay_p(jnp.stack([rev(pwr[0])[..., 1:], pwr[1][..., :t]]))
    psi = lay_p(jnp.stack([rev(pwi[0])[..., 1:], pwi[1][..., :t]]))
    lay_b = lambda v: jnp.einsum('xGhnq,gh->xGgqhn', v.reshape(2, nt, tg, ns, p), eye).reshape(
        2, nt, LANES, tg * ns)
    bmr, bmi = lay_b(bbr), lay_b(bbi)
    bsr, bsi = _cmul(psr[:, :, :, None, :], psi[:, :, :, None, :], bmr[:, :, None], bmi[:, :, None])
    bst = jnp.stack([bsr, bsi], axis=1).transpose(2, 3, 4, 0, 1, 5).reshape(nt, t * LANES, 2 * S5_DIR)
    lay_c = lambda v: jnp.einsum('xGhpn,gh->xGgnhp', v.reshape(2, nt, tg, p, ns), eye).reshape(
        2, nt, tg * ns, LANES)
    cmr, cmi = lay_c(c_re), lay_c(c_im)
    por = jnp.stack([pwr[0][..., 1:], rev(pwr[1])[..., :t]]).reshape(2, nt, tg * ns, t)
    poi = jnp.stack([pwi[0][..., 1:], rev(pwi[1])[..., :t]]).reshape(2, nt, tg * ns, t)
    csr, csi = _cmul(cmr[:, :, :, None, :], cmi[:, :, :, None, :], por[..., None], poi[..., None])
    cst = jnp.stack([csr, -csi], axis=1).transpose(2, 0, 1, 3, 4, 5).reshape(nt, 2 * S5_DIR, t * LANES)
    a = jnp.stack([pwr[..., t], pwi[..., t]], axis=1)
    a = a.reshape(2, 2, nt, tg, ns).transpose(2, 0, 1, 3, 4).reshape(nt, 1, 2 * S5_DIR)
    a = jnp.broadcast_to(a, (nt, S5_BATCH, 2 * S5_DIR))
    return dict(s5_m=m.astype(BF16), s5_bst=bst.astype(BF16), s5_cst=cst.astype(BF16), s5_a=a.astype(F32))


def _layer_weights(i, p):
    w_in = p["w_in"][i]
    k_dim = w_in.shape[0]
    half = QK_ROPE // 2
    kr = w_in[:, OFF_KR:OFF_GATE]
    z64 = jnp.zeros((k_dim, QK_NOPE), F32)
    z32 = jnp.zeros((k_dim, HEAD_PAD - QK_NOPE - QK_ROPE), F32)
    w_a = jnp.concatenate([w_in[:, :OFF_KR], z64, kr, z32, z64, kr[:, half:], kr[:, :half], z32], axis=1)
    wq = p["w_qb"][i].reshape(Q_LORA, MLA_HEADS, QK_NOPE + QK_ROPE)
    rope = wq[:, :, QK_NOPE:]
    wq_b = jnp.concatenate([jnp.zeros_like(wq[:, :, :QK_NOPE]), rope[:, :, half:], rope[:, :, :half]], axis=2)
    wkv = p["w_kvb"][i].reshape(KV_LORA, MLA_HEADS, QK_NOPE + V_DIM)
    row = lambda v: v.reshape(1, -1).astype(F32)
    lw = dict(
        g_mix=row(p["g_mix"][i]), w_a=w_a.astype(BF16), w_gate=w_in[:, OFF_GATE:].astype(BF16),
        g_q=row(p["g_q"][i]), g_kv=row(p["g_kv"][i]),
        wq_a=_pad_heads(p["w_qb"][i], QK_NOPE + QK_ROPE).astype(BF16),
        wq_b=_pad_heads(wq_b.reshape(Q_LORA, -1), QK_NOPE + QK_ROPE).astype(BF16),
        wk=_pad_heads(wkv[:, :, :QK_NOPE].reshape(KV_LORA, -1), QK_NOPE).astype(BF16),
        wv_t=_pad_heads(wkv[:, :, QK_NOPE:].reshape(KV_LORA, -1), V_DIM).T.astype(BF16),
        w_fnet=p["w_fnet"][i].astype(BF16), w_glu=p["w_glu"][i].astype(BF16),
        w_s5=p["w_s5"][i].astype(BF16), w_o=p["w_o_mla"][i].astype(BF16),
        w_out=p["w_out"][i].astype(BF16), g_mlp=row(p["g_mlp"][i]),
        w_up=p["w_up"][i].astype(BF16), w_down=p["w_down"][i].astype(BF16))
    lw.update(_s5_tables(p["s5_lam_re"][i], p["s5_lam_im"][i], p["s5_log_dt"][i], p["s5_b_re"][i],
                         p["s5_b_im"][i], p["s5_c_re"][i], p["s5_c_im"][i], p["s5_d"][i]))
    return lw


def _trunk(x, layers, tabs, g_final):
    bsz, seq, _ = x.shape
    assert bsz <= S5_BATCH and seq % S5_CHUNK == 0
    x = x.reshape(bsz * seq, D_MODEL)
    for i, lw in enumerate(layers):
        ucs, u5, q, k, vt = _inproj(x, lw, tabs, bsz, seq)
        yf = _fnet(ucs.reshape(bsz, seq, 2 * FNET_W), tabs, bsz, seq).reshape(bsz * seq, FNET_W)
        y5 = _s5(u5, lw, seq)
        o = _attention(q, k, vt, bsz, seq).reshape(bsz * seq, MLA_HEADS * V_DIM)
        x = _merge(x, yf, y5, o, lw, seq)
        x = _mlp(x, lw, g_final, final=(i == len(layers) - 1))
    return x.reshape(bsz, seq, D_MODEL)


def kernel(x_prompt, x_sample, g_mix, w_in, w_fnet, s5_lam_re, s5_lam_im, s5_log_dt, s5_b_re,
           s5_b_im, s5_c_re, s5_c_im, s5_d, w_glu, w_s5, g_q, w_qb, g_kv, w_kvb, w_o_mla,
           w_out, g_mlp, w_up, w_down, g_final):
    p = dict(g_mix=g_mix, w_in=w_in, w_fnet=w_fnet, s5_lam_re=s5_lam_re, s5_lam_im=s5_lam_im,
             s5_log_dt=s5_log_dt, s5_b_re=s5_b_re, s5_b_im=s5_b_im, s5_c_re=s5_c_re,
             s5_c_im=s5_c_im, s5_d=s5_d, w_glu=w_glu, w_s5=w_s5, g_q=g_q, w_qb=w_qb, g_kv=g_kv,
             w_kvb=w_kvb, w_o_mla=w_o_mla, w_out=w_out, g_mlp=g_mlp, w_up=w_up, w_down=w_down)
    layers = [_layer_weights(i, p) for i in range(g_mix.shape[0])]
    gfin = g_final.reshape(1, -1).astype(F32)
    outs = []
    for x in (x_prompt, x_sample):
        tabs = _tables(x.shape[1])
        outs.append(_trunk(x, layers, tabs, gfin))
    return tuple(outs)
```

```python
import functools
import math

import jax
import jax.numpy as jnp
from jax import lax
from jax.experimental import pallas as pl
from jax.experimental.pallas import tpu as pltpu

F32 = jnp.float32
BF16 = jnp.bfloat16

D_MODEL = 1024
FNET_GROUP_DIM = 64
FNET_GROUPS = 6
FNET_W = 384
S5_GROUP_DIM = 16
S5_GROUPS = 24
S5_W = 384
S5_STATE = 64
MLA_HEADS = 16
QK_NOPE = 64
QK_ROPE = 32
V_DIM = 64
Q_LORA = 384
KV_LORA = 256
ROPE_BASE = 10000.0
N_BRANCH = 3
D_FF = 4 * D_MODEL
EPS = 1e-6

OFF_FNET = 0
OFF_S5 = OFF_FNET + FNET_W
OFF_Q = OFF_S5 + S5_W
OFF_KV = OFF_Q + Q_LORA
OFF_KR = OFF_KV + KV_LORA
OFF_GATE = OFF_KR + QK_ROPE

LANES = 128
SUBLANES = 8
HEAD_PAD = 128
VMEM_LIMIT = 56 * 1024 * 1024

S5_CHUNK = SUBLANES
S5_TILES = S5_W // LANES
S5_TILE_GROUPS = LANES // S5_GROUP_DIM
S5_HALF = S5_TILE_GROUPS * S5_STATE
S5_DIR = 2 * S5_HALF
S5_BATCH = SUBLANES
S5_BLOCK_CHUNKS = 64
FNET_R2 = FNET_GROUP_DIM

ZA_CQ = OFF_Q
ZA_CKV = OFF_KV
ZA_KRA = OFF_KR
ZA_KRB = OFF_KR + LANES
ZA_W = ZA_KRB + LANES

ATTN_ROWS = 1024
ATTN_SUB = 512


def _rms(x, g):
    return x * lax.rsqrt(jnp.mean(x * x, axis=-1, keepdims=True) + EPS) * g


def _dot(a, b):
    return jnp.dot(a, b, preferred_element_type=F32)


def _params(*sem):
    return pltpu.CompilerParams(dimension_semantics=sem, vmem_limit_bytes=VMEM_LIMIT)


def _const_spec(shape):
    nd = len(shape)
    return pl.BlockSpec(shape, lambda *_: (0,) * nd, pipeline_mode=pl.Buffered(1))


def _inproj_kernel(x_ref, gmix_ref, wa_ref, cs_ref, gq_ref, wqa_ref, gkv_ref, wk_ref,
                   wv_ref, taq_ref, tbq_ref, tak_ref, tbk_ref, *rest):
    ucs_ref, u5_ref, q_ref, k_ref, v_ref = rest[-5:]
    h = _rms(x_ref[...], gmix_ref[...]).astype(BF16)
    z = _dot(h, wa_ref[...])
    ucs_ref[...] = _dot(z[:, OFF_FNET:OFF_S5].astype(BF16), cs_ref[...])
    for t in range(S5_TILES):
        for c in range(z.shape[0] // S5_CHUNK):
            u5_ref[t, c, 0] = z[c * S5_CHUNK:(c + 1) * S5_CHUNK, OFF_S5 + t * LANES:OFF_S5 + (t + 1) * LANES]
    hq = _rms(z[:, ZA_CQ:ZA_CQ + Q_LORA], gq_ref[...]).astype(BF16)
    hkv = _rms(z[:, ZA_CKV:ZA_CKV + KV_LORA], gkv_ref[...]).astype(BF16)
    kpe = z[:, ZA_KRA:ZA_KRA + LANES] * tak_ref[...] + z[:, ZA_KRB:ZA_KRB + LANES] * tbk_ref[...]
    kpe2 = jnp.concatenate([kpe, kpe], axis=1)
    taq2 = jnp.concatenate([taq_ref[...]] * 2, axis=1)
    tbq2 = jnp.concatenate([tbq_ref[...]] * 2, axis=1)
    first_half = (lax.broadcasted_iota(jnp.int32, taq2.shape, 1) & (HEAD_PAD - 1)) < QK_NOPE + QK_ROPE // 2
    for j in range(MLA_HEADS // 2):
        sl = slice(2 * HEAD_PAD * j, 2 * HEAD_PAD * (j + 1))
        qa = _dot(hq, wqa_ref[:, sl])
        qb = jnp.where(first_half, pltpu.roll(qa, 2 * HEAD_PAD - QK_ROPE // 2, 1), pltpu.roll(qa, QK_ROPE // 2, 1))
        q_ref[:, sl] = (qa * taq2 + qb * tbq2).astype(BF16)
        k_ref[:, sl] = (_dot(hkv, wk_ref[:, sl]) + kpe2).astype(BF16)
    vt = lax.dot_general(wv_ref[...], hkv, (((1,), (1,)), ((), ())), preferred_element_type=F32)
    ones_row = (lax.broadcasted_iota(jnp.int32, vt.shape, 0) & V_DIM) != 0
    v_ref[0] = jnp.where(ones_row, 1.0, vt).astype(BF16)


def _inproj(x, lw, tabs, bsz, seq):
    n = x.shape[0]
    tm = min(512, seq)
    nt = seq // tm
    nc = seq // S5_CHUNK
    row = lambda w: pl.BlockSpec((tm, w), lambda i: (i, 0))
    tab = pl.BlockSpec((tm, LANES), lambda i: (i % nt, 0))
    hw = MLA_HEADS * HEAD_PAD
    u5_shape = (S5_TILES, nc, S5_BATCH, S5_CHUNK, LANES)
    u5_spec = pl.BlockSpec((S5_TILES, tm // S5_CHUNK, 1, S5_CHUNK, LANES),
                           lambda i: (0, i % nt, i // nt, 0, 0))
    in_specs = [row(D_MODEL), _const_spec((1, D_MODEL)), _const_spec((D_MODEL, ZA_W)),
                _const_spec((FNET_W, 2 * FNET_W)), _const_spec((1, Q_LORA)),
                _const_spec((Q_LORA, hw)), _const_spec((1, KV_LORA)),
                _const_spec((KV_LORA, hw)), _const_spec((hw, KV_LORA)),
                tab, tab, tab, tab]
    args = [x, lw["g_mix"], lw["w_a"], tabs["cs"], lw["g_q"], lw["wq_a"], lw["g_kv"],
            lw["wk"], lw["wv_t"], tabs["taq"], tabs["tbq"], tabs["tak"], tabs["tbk"]]
    aliases = {}
    if bsz < S5_BATCH:
        aliases = {len(args): 1}
        in_specs.append(pl.BlockSpec(memory_space=pl.ANY))
        args.append(jnp.zeros(u5_shape, F32))
    return pl.pallas_call(
        _inproj_kernel,
        grid=(n // tm,),
        in_specs=in_specs,
        out_specs=[row(2 * FNET_W), u5_spec, row(hw), row(hw),
                   pl.BlockSpec((1, hw, tm), lambda i: (i // nt, 0, i % nt))],
        out_shape=[jax.ShapeDtypeStruct((n, 2 * FNET_W), F32), jax.ShapeDtypeStruct(u5_shape, F32),
                   jax.ShapeDtypeStruct((n, hw), BF16), jax.ShapeDtypeStruct((n, hw), BF16),
                   jax.ShapeDtypeStruct((bsz, hw, seq), BF16)],
        input_output_aliases=aliases,
        compiler_params=_params("parallel"),
        name="inproj",
    )(*args)


def _fnet_kernel(uc_ref, us_ref, f1_ref, f2_ref, twc_ref, tws_ref, out_ref, ar_ref, ai_ref, *, r1):
    f1 = f1_ref[...]
    for n2 in range(FNET_R2):
        rows = pl.ds(n2, r1, stride=FNET_R2)
        u = jnp.concatenate([uc_ref[0, rows, :], us_ref[0, rows, :]], axis=1).astype(BF16)
        pr = _dot(f1, u)
        ar = pr[:r1, :LANES] - pr[r1:, LANES:]
        ai = -(pr[:r1, LANES:] + pr[r1:, :LANES])
        c, s = twc_ref[n2], tws_ref[n2]
        ar_ref[n2 * r1:(n2 + 1) * r1, :] = ar * c + ai * s
        ai_ref[n2 * r1:(n2 + 1) * r1, :] = ai * c - ar * s
    f2 = f2_ref[...]
    for k1 in range(r1):
        rows = pl.ds(k1, FNET_R2, stride=r1)
        gk = jnp.concatenate([ar_ref[rows, :], ai_ref[rows, :]], axis=0).astype(BF16)
        out_ref[0, rows, :] = _dot(f2, gk)


def _fnet(ucs, tabs, bsz, seq):
    r1 = seq // FNET_R2
    nt = FNET_W // LANES
    blk = lambda off: pl.BlockSpec((1, seq, LANES), lambda b, c: (b, 0, c + off))
    return pl.pallas_call(
        functools.partial(_fnet_kernel, r1=r1),
        grid=(bsz, nt),
        in_specs=[blk(0), blk(nt), _const_spec((2 * r1, r1)), _const_spec((FNET_R2, 2 * FNET_R2)),
                  _const_spec((FNET_R2, r1, LANES)), _const_spec((FNET_R2, r1, LANES))],
        out_specs=blk(0),
        out_shape=jax.ShapeDtypeStruct((bsz, seq, FNET_W), F32),
        scratch_shapes=[pltpu.VMEM((seq, LANES), F32)] * 2,
        compiler_params=_params("parallel", "parallel"),
        name="fnet",
    )(ucs, ucs, tabs["f1"], tabs["f2"], tabs["twc"], tabs["tws"])


def _s5_rows(u_ref, rows):
    return jnp.concatenate([u_ref[0, pl.ds(s, rows, stride=S5_CHUNK), :] for s in range(S5_CHUNK)],
                           axis=1).astype(BF16)


def _s5_in_kernel(u_ref, bst_ref, loc_ref, *, rows):
    loc_ref[0] = _dot(_s5_rows(u_ref, rows), bst_ref[0])


def _s5_scan_kernel(lf_ref, lb_ref, a_ref, spf_ref, spb_ref, carry_ref, *, cb):
    @pl.when(pl.program_id(1) == 0)
    def _():
        carry_ref[...] = jnp.zeros_like(carry_ref)

    afr, afi = a_ref[0, :, :S5_HALF], a_ref[0, :, S5_HALF:S5_DIR]
    abr, abi = a_ref[0, :, S5_DIR:S5_DIR + S5_HALF], a_ref[0, :, S5_DIR + S5_HALF:]

    def step(i, carry):
        fr, fi, br, bi = carry
        rf = pl.ds(pl.multiple_of(i * S5_BATCH, S5_BATCH), S5_BATCH)
        rb = pl.ds(pl.multiple_of((cb - 1 - i) * S5_BATCH, S5_BATCH), S5_BATCH)
        spf_ref[0, rf, :S5_HALF] = fr
        spf_ref[0, rf, S5_HALF:] = fi
        spb_ref[0, rb, :S5_HALF] = br
        spb_ref[0, rb, S5_HALF:] = bi
        return (afr * fr - afi * fi + lf_ref[0, rf, :S5_HALF], afr * fi + afi * fr + lf_ref[0, rf, S5_HALF:],
                abr * br - abi * bi + lb_ref[0, rb, :S5_HALF], abr * bi + abi * br + lb_ref[0, rb, S5_HALF:])

    carry = lax.fori_loop(0, cb, step, tuple(carry_ref[k] for k in range(4)), unroll=2)
    for k in range(4):
        carry_ref[k] = carry[k]


def _s5_out_kernel(u_ref, m_ref, spf_ref, spb_ref, cst_ref, y_ref, *, rows):
    y = (_dot(_s5_rows(u_ref, rows), m_ref[0]) + _dot(spf_ref[0].astype(BF16), cst_ref[0, :S5_DIR, :])
         + _dot(spb_ref[0].astype(BF16), cst_ref[0, S5_DIR:, :]))
    for s in range(S5_CHUNK):
        y_ref[0, pl.ds(s, rows, stride=S5_CHUNK), :] = y[:, s * LANES:(s + 1) * LANES]


def _s5(u5, lw, seq):
    nc = seq // S5_CHUNK
    cb = min(S5_BLOCK_CHUNKS, nc)
    nblk = nc // cb
    rows = cb * S5_BATCH
    kw = S5_CHUNK * LANES
    u2 = u5.reshape(S5_TILES, nc * S5_BATCH * S5_CHUNK, LANES)
    u_spec = pl.BlockSpec((1, rows * S5_CHUNK, LANES), lambda t, j: (t, j, 0))
    wspec = lambda a, b: pl.BlockSpec((1, a, b), lambda t, j: (t, 0, 0))
    loc = pl.pallas_call(
        functools.partial(_s5_in_kernel, rows=rows),
        grid=(S5_TILES, nblk),
        in_specs=[u_spec, wspec(kw, 2 * S5_DIR)],
        out_specs=pl.BlockSpec((1, rows, 2 * S5_DIR), lambda t, j: (t, j, 0)),
        out_shape=jax.ShapeDtypeStruct((S5_TILES, nc * S5_BATCH, 2 * S5_DIR), F32),
        compiler_params=_params("parallel", "parallel"),
        name="s5_in",
    )(u2, lw["s5_bst"])
    fwd = lambda lane: pl.BlockSpec((1, rows, S5_DIR), lambda t, j: (t, j, lane))
    bwd = lambda lane: pl.BlockSpec((1, rows, S5_DIR), lambda t, j: (t, nblk - 1 - j, lane))
    sp_shape = jax.ShapeDtypeStruct((S5_TILES, nc * S5_BATCH, S5_DIR), F32)
    spf, spb = pl.pallas_call(
        functools.partial(_s5_scan_kernel, cb=cb),
        grid=(S5_TILES, nblk),
        in_specs=[fwd(0), bwd(1), wspec(S5_BATCH, 2 * S5_DIR)],
        out_specs=[fwd(0), bwd(0)],
        out_shape=[sp_shape, sp_shape],
        scratch_shapes=[pltpu.VMEM((4, S5_BATCH, S5_HALF), F32)],
        compiler_params=_params("parallel", "arbitrary"),
        name="s5_scan",
    )(loc, loc, lw["s5_a"])
    y2 = pl.pallas_call(
        functools.partial(_s5_out_kernel, rows=rows),
        grid=(S5_TILES, nblk),
        in_specs=[u_spec, wspec(kw, kw), fwd(0), fwd(0), wspec(2 * S5_DIR, kw)],
        out_specs=u_spec,
        out_shape=jax.ShapeDtypeStruct(u2.shape, F32),
        compiler_params=_params("parallel", "parallel"),
        name="s5_out",
    )(u2, lw["s5_m"], spf, spb, lw["s5_cst"])
    return y2.reshape(u5.shape)


def _attn_kernel(q_ref, k_ref, vt_ref, o_ref, s_ref, m_ref, p_ref, ot_ref, *, sub):
    units = [(u, j) for u in range(q_ref.shape[1] // sub) for j in range(2)]

    def scores(n):
        u, j = units[n]
        sl = slice(HEAD_PAD * j, HEAD_PAD * (j + 1))
        st = lax.dot_general(k_ref[0, :, sl], q_ref[0, u * sub:(u + 1) * sub, sl],
                             (((1,), (1,)), ((), ())), preferred_element_type=F32)
        s_ref[n % 2] = st
        m_ref[n % 2] = jnp.max(st, axis=0, keepdims=True)

    scores(0)
    for n, (u, j) in enumerate(units):
        if n + 1 < len(units):
            scores(n + 1)
        p_ref[...] = jnp.exp2(s_ref[n % 2] - m_ref[n % 2]).astype(BF16)
        ol = _dot(vt_ref[0, HEAD_PAD * j:HEAD_PAD * (j + 1), :], p_ref[...])
        ot_ref[V_DIM * j:V_DIM * (j + 1), u * sub:(u + 1) * sub] = ol[:V_DIM] / ol[V_DIM:]
    o_ref[0] = ot_ref[...].T.astype(BF16)


def _attention(q, k, vt, bsz, seq):
    tq = min(ATTN_ROWS, seq)
    sub = min(ATTN_SUB, tq)
    hw = MLA_HEADS * HEAD_PAD
    return pl.pallas_call(
        functools.partial(_attn_kernel, sub=sub),
        grid=(bsz, MLA_HEADS // 2, seq // tq),
        in_specs=[pl.BlockSpec((1, tq, 2 * HEAD_PAD), lambda b, h, i: (b, i, h)),
                  pl.BlockSpec((1, seq, 2 * HEAD_PAD), lambda b, h, i: (b, 0, h)),
                  pl.BlockSpec((1, 2 * HEAD_PAD, seq), lambda b, h, i: (b, h, 0))],
        out_specs=pl.BlockSpec((1, tq, 2 * V_DIM), lambda b, h, i: (b, i, h)),
        out_shape=jax.ShapeDtypeStruct((bsz, seq, MLA_HEADS * V_DIM), BF16),
        scratch_shapes=[pltpu.VMEM((2, seq, sub), F32), pltpu.VMEM((2, 1, sub), F32),
                        pltpu.VMEM((seq, sub), BF16), pltpu.VMEM((2 * V_DIM, tq), F32)],
        compiler_params=_params("parallel", "parallel", "arbitrary"),
        name="attention",
    )(q.reshape(bsz, seq, hw), k.reshape(bsz, seq, hw), vt)


def _merge_kernel(x_ref, yf_ref, y5_ref, o_ref, gmix_ref, wfn_ref, wglu_ref, ws5_ref, wo_ref,
                  wg_ref, wout_ref, out_ref):
    x = x_ref[...]
    h = _rms(x, gmix_ref[...]).astype(BF16)
    y_a = _dot(yf_ref[...].astype(BF16), wfn_ref[...])
    nchunk = y5_ref.shape[1]
    ys = jnp.concatenate(
        [jnp.concatenate([y5_ref[t, c, 0] for c in range(nchunk)], axis=0) for t in range(S5_TILES)],
        axis=1)
    s = jax.nn.gelu(ys).astype(BF16)
    hg = _dot(s, wglu_ref[...])
    glu = (hg[:, :S5_W] * jax.nn.sigmoid(hg[:, S5_W:])).astype(BF16)
    y_b = _dot(glu, ws5_ref[...])
    y_c = _dot(o_ref[...], wo_ref[...])
    merged = jax.nn.sigmoid(_dot(h, wg_ref[:, :D_MODEL])) * y_a
    merged += jax.nn.sigmoid(_dot(h, wg_ref[:, D_MODEL:2 * D_MODEL])) * y_b
    merged += jax.nn.sigmoid(_dot(h, wg_ref[:, 2 * D_MODEL:])) * y_c
    out_ref[...] = x + _dot(merged.astype(BF16), wout_ref[...])


def _merge(x, yf, y5, o, lw, seq):
    n = x.shape[0]
    tm = min(1024, seq)
    nt = seq // tm
    row = lambda w: pl.BlockSpec((tm, w), lambda i: (i, 0))
    y5_spec = pl.BlockSpec((S5_TILES, tm // S5_CHUNK, 1, S5_CHUNK, LANES),
                           lambda i: (0, i % nt, i // nt, 0, 0))
    return pl.pallas_call(
        _merge_kernel,
        grid=(n // tm,),
        in_specs=[row(D_MODEL), row(FNET_W), y5_spec, row(MLA_HEADS * V_DIM),
                  _const_spec((1, D_MODEL)), _const_spec((FNET_W, D_MODEL)),
                  _const_spec((S5_W, 2 * S5_W)), _const_spec((S5_W, D_MODEL)),
                  _const_spec((MLA_HEADS * V_DIM, D_MODEL)),
                  _const_spec((D_MODEL, N_BRANCH * D_MODEL)), _const_spec((D_MODEL, D_MODEL))],
        out_specs=row(D_MODEL),
        out_shape=jax.ShapeDtypeStruct((n, D_MODEL), F32),
        compiler_params=_params("parallel"),
        name="merge",
    )(x, yf, y5, o, lw["g_mix"], lw["w_fnet"], lw["w_glu"], lw["w_s5"], lw["w_o"], lw["w_gate"],
      lw["w_out"])


def _mlp_kernel(x_ref, g_ref, wup_ref, wdn_ref, gfin_ref, out_ref, *, final):
    x = x_ref[...]
    h = _rms(x, g_ref[...]).astype(BF16)
    a = jnp.square(jnp.maximum(_dot(h, wup_ref[...]), 0.0)).astype(BF16)
    y = x + _dot(a, wdn_ref[...])
    if final:
        y = _rms(y, gfin_ref[...])
    out_ref[...] = y


def _mlp(x, lw, g_final, final):
    n = x.shape[0]
    tm = min(512, n)
    resident = lambda shape: pl.BlockSpec(shape, lambda i: (0, 0), pipeline_mode=pl.Buffered(1))
    return pl.pallas_call(
        functools.partial(_mlp_kernel, final=final),
        grid=(n // tm,),
        in_specs=[pl.BlockSpec((tm, D_MODEL), lambda i: (i, 0)), _const_spec((1, D_MODEL)),
                  resident((D_MODEL, D_FF)), resident((D_FF, D_MODEL)), _const_spec((1, D_MODEL))],
        out_specs=pl.BlockSpec((tm, D_MODEL), lambda i: (i, 0)),
        out_shape=jax.ShapeDtypeStruct((n, D_MODEL), F32),
        compiler_params=_params("parallel"),
        name="mlp",
    )(x, lw["g_mlp"], lw["w_up"], lw["w_down"], g_final)


def _tables(seq):
    half = QK_ROPE // 2
    inv = ROPE_BASE ** (-jnp.arange(half, dtype=F32) / half)
    ang = jnp.arange(seq, dtype=F32)[:, None] * inv[None, :]
    cos, sin = jnp.cos(ang), jnp.sin(ang)
    one = jnp.ones((seq, QK_NOPE), F32)
    z64 = jnp.zeros((seq, QK_NOPE), F32)
    z32 = jnp.zeros((seq, HEAD_PAD - QK_NOPE - QK_ROPE), F32)
    scale = (QK_NOPE + QK_ROPE) ** -0.5 * math.log2(math.e)
    ta = jnp.concatenate([z64, cos, cos, z32], axis=1)
    tb = jnp.concatenate([z64, -sin, sin, z32], axis=1)
    taq = jnp.concatenate([one, cos, cos, z32], axis=1) * scale
    c = jnp.arange(FNET_GROUP_DIM)
    ang64 = (2.0 * math.pi / FNET_GROUP_DIM) * ((c[:, None] * c[None, :]) % FNET_GROUP_DIM).astype(F32)
    eye = jnp.eye(FNET_GROUPS, dtype=F32)
    norm = 1.0 / math.sqrt(seq * FNET_GROUP_DIM)
    cs = jnp.concatenate([jnp.kron(eye, jnp.cos(ang64)), jnp.kron(eye, jnp.sin(ang64))], axis=1) * norm
    r1 = seq // FNET_R2
    i1 = jnp.arange(r1)
    a1 = (2.0 * math.pi / r1) * ((i1[:, None] * i1[None, :]) % r1).astype(F32)
    f1 = jnp.concatenate([jnp.cos(a1), jnp.sin(a1)], axis=0).astype(BF16)
    f2 = jnp.concatenate([jnp.cos(ang64), jnp.sin(ang64)], axis=1).astype(BF16)
    atw = (2.0 * math.pi / seq) * (c[:, None] * i1[None, :]).astype(F32)
    twc = jnp.broadcast_to(jnp.cos(atw)[:, :, None], (FNET_R2, r1, LANES))
    tws = jnp.broadcast_to(jnp.sin(atw)[:, :, None], (FNET_R2, r1, LANES))
    return dict(taq=taq, tbq=tb * scale, tak=ta, tbk=tb, cs=cs.astype(BF16), f1=f1, f2=f2, twc=twc, tws=tws)


def _pad_heads(w, width):
    k = w.shape[0]
    w = w.reshape(k, MLA_HEADS, width)
    return jnp.pad(w, ((0, 0), (0, 0), (0, HEAD_PAD - width))).reshape(k, MLA_HEADS * HEAD_PAD)


def _cmul(ar, ai, br, bi):
    return ar * br - ai * bi, ar * bi + ai * br


def _s5_tables(lam_re, lam_im, log_dt, b_re, b_im, c_re, c_im, d_skip):
    t, p, g, ns = S5_CHUNK, S5_GROUP_DIM, S5_GROUPS, S5_STATE
    nt, tg = S5_TILES, S5_TILE_GROUPS
    dt = jnp.exp(log_dt)[..., None]
    ar, ai = lam_re * dt, lam_im * dt
    mag = jnp.exp(ar)
    lbr, lbi = mag * jnp.cos(ai), mag * jnp.sin(ai)
    den = lam_re * lam_re + lam_im * lam_im
    cfr = ((lbr - 1.0) * lam_re + lbi * lam_im) / den
    cfi = (lbi * lam_re - (lbr - 1.0) * lam_im) / den
    bbr, bbi = _cmul(cfr[..., None], cfi[..., None], b_re, b_im)
    d = jnp.arange(t + 1, dtype=F32)
    pmag = jnp.exp(ar[..., None] * d)
    pwr, pwi = pmag * jnp.cos(ai[..., None] * d), pmag * jnp.sin(ai[..., None] * d)
    cpr, cpi = _cmul(c_re[..., None], c_im[..., None], pwr[:, :, None, :, :t], pwi[:, :, None, :, :t])
    kern = (jnp.einsum('xgpnd,xgnq->xgdpq', cpr, bbr)
            - jnp.einsum('xgpnd,xgnq->xgdpq', cpi, bbi))
    skip = jnp.eye(p, dtype=F32)[None] * d_skip.reshape(g, p)[:, :, None]
    k0 = kern[0][:, :1] + kern[1][:, :1] + skip[:, None]
    kfull = jnp.concatenate([kern[1][:, :0:-1], k0, kern[0][:, 1:]], axis=1)
    eye = jnp.eye(tg, dtype=F32)
    kq = kfull.transpose(0, 1, 3, 2).reshape(nt, tg, 2 * t - 1, p, p)
    bd = jnp.einsum('Ggdqp,gh->Gdgqhp', kq, eye).reshape(nt, 2 * t - 1, LANES, LANES)
    lag = jnp.arange(t)[None, :] - jnp.arange(t)[:, None] + (t - 1)
    m = bd[:, lag].transpose(0, 1, 3, 2, 4).reshape(nt, t * LANES, t * LANES)
    rev = lambda v: v[..., ::-1]
    lay_p = lambda v: v.reshape(2, nt, tg * ns, t).transpose(0, 1, 3, 2)
    psr = lay_p(jnp.stack([rev(pwr[0])[..., 1:], pwr[1][..., :t]]))
    psi = lay_p(jnp.stack([rev(pwi[0])[..., 1:], pwi[1][..., :t]]))
    lay_b = lambda v: jnp.einsum('xGhnq,gh->xGgqhn', v.reshape(2, nt, tg, ns, p), eye).reshape(
        2, nt, LANES, tg * ns)
    bmr, bmi = lay_b(bbr), lay_b(bbi)
    bsr, bsi = _cmul(psr[:, :, :, None, :], psi[:, :, :, None, :], bmr[:, :, None], bmi[:, :, None])
    bst = jnp.stack([bsr, bsi], axis=1).transpose(2, 3, 4, 0, 1, 5).reshape(nt, t * LANES, 2 * S5_DIR)
    lay_c = lambda v: jnp.einsum('xGhpn,gh->xGgnhp', v.reshape(2, nt, tg, p, ns), eye).reshape(
        2, nt, tg * ns, LANES)
    cmr, cmi = lay_c(c_re), lay_c(c_im)
    por = jnp.stack([pwr[0][..., 1:], rev(pwr[1])[..., :t]]).reshape(2, nt, tg * ns, t)
    poi = jnp.stack([pwi[0][..., 1:], rev(pwi[1])[..., :t]]).reshape(2, nt, tg * ns, t)
    csr, csi = _cmul(cmr[:, :, :, None, :], cmi[:, :, :, None, :], por[..., None], poi[..., None])
    cst = jnp.stack([csr, -csi], axis=1).transpose(2, 0, 1, 3, 4, 5).reshape(nt, 2 * S5_DIR, t * LANES)
    a = jnp.stack([pwr[..., t], pwi[..., t]], axis=1)
    a = a.reshape(2, 2, nt, tg, ns).transpose(2, 0, 1, 3, 4).reshape(nt, 1, 2 * S5_DIR)
    a = jnp.broadcast_to(a, (nt, S5_BATCH, 2 * S5_DIR))
    return dict(s5_m=m.astype(BF16), s5_bst=bst.astype(BF16), s5_cst=cst.astype(BF16), s5_a=a.astype(F32))


def _layer_weights(i, p):
    w_in = p["w_in"][i]
    k_dim = w_in.shape[0]
    half = QK_ROPE // 2
    kr = w_in[:, OFF_KR:OFF_GATE]
    z64 = jnp.zeros((k_dim, QK_NOPE), F32)
    z32 = jnp.zeros((k_dim, HEAD_PAD - QK_NOPE - QK_ROPE), F32)
    w_a = jnp.concatenate([w_in[:, :OFF_KR], z64, kr, z32, z64, kr[:, half:], kr[:, :half], z32], axis=1)
    wkv = p["w_kvb"][i].reshape(KV_LORA, MLA_HEADS, QK_NOPE + V_DIM)
    row = lambda v: v.reshape(1, -1).astype(F32)
    lw = dict(
        g_mix=row(p["g_mix"][i]), w_a=w_a.astype(BF16), w_gate=w_in[:, OFF_GATE:].astype(BF16),
        g_q=row(p["g_q"][i]), g_kv=row(p["g_kv"][i]),
        wq_a=_pad_heads(p["w_qb"][i], QK_NOPE + QK_ROPE).astype(BF16),
        wk=_pad_heads(wkv[:, :, :QK_NOPE].reshape(KV_LORA, -1), QK_NOPE).astype(BF16),
        wv_t=_pad_heads(wkv[:, :, QK_NOPE:].reshape(KV_LORA, -1), V_DIM).T.astype(BF16),
        w_fnet=p["w_fnet"][i].astype(BF16), w_glu=p["w_glu"][i].astype(BF16),
        w_s5=p["w_s5"][i].astype(BF16), w_o=p["w_o_mla"][i].astype(BF16),
        w_out=p["w_out"][i].astype(BF16), g_mlp=row(p["g_mlp"][i]),
        w_up=p["w_up"][i].astype(BF16), w_down=p["w_down"][i].astype(BF16))
    lw.update(_s5_tables(p["s5_lam_re"][i], p["s5_lam_im"][i], p["s5_log_dt"][i], p["s5_b_re"][i],
                         p["s5_b_im"][i], p["s5_c_re"][i], p["s5_c_im"][i], p["s5_d"][i]))
    return lw


def _trunk(x, layers, tabs, g_final):
    bsz, seq, _ = x.shape
    assert bsz <= S5_BATCH and seq % S5_CHUNK == 0
    x = x.reshape(bsz * seq, D_MODEL)
    for i, lw in enumerate(layers):
        ucs, u5, q, k, vt = _inproj(x, lw, tabs, bsz, seq)
        yf = _fnet(ucs.reshape(bsz, seq, 2 * FNET_W), tabs, bsz, seq).reshape(bsz * seq, FNET_W)
        y5 = _s5(u5, lw, seq)
        o = _attention(q, k, vt, bsz, seq).reshape(bsz * seq, MLA_HEADS * V_DIM)
        x = _merge(x, yf, y5, o, lw, seq)
        x = _mlp(x, lw, g_final, final=(i == len(layers) - 1))
    return x.reshape(bsz, seq, D_MODEL)


def kernel(x_prompt, x_sample, g_mix, w_in, w_fnet, s5_lam_re, s5_lam_im, s5_log_dt, s5_b_re,
           s5_b_im, s5_c_re, s5_c_im, s5_d, w_glu, w_s5, g_q, w_qb, g_kv, w_kvb, w_o_mla,
           w_out, g_mlp, w_up, w_down, g_final):
    p = dict(g_mix=g_mix, w_in=w_in, w_fnet=w_fnet, s5_lam_re=s5_lam_re, s5_lam_im=s5_lam_im,
             s5_log_dt=s5_log_dt, s5_b_re=s5_b_re, s5_b_im=s5_b_im, s5_c_re=s5_c_re,
             s5_c_im=s5_c_im, s5_d=s5_d, w_glu=w_glu, w_s5=w_s5, g_q=g_q, w_qb=w_qb, g_kv=g_kv,
             w_kvb=w_kvb, w_o_mla=w_o_mla, w_out=w_out, g_mlp=g_mlp, w_up=w_up, w_down=w_down)
    layers = [_layer_weights(i, p) for i in range(g_mix.shape[0])]
    gfin = g_final.reshape(1, -1).astype(F32)
    outs = []
    for x in (x_prompt, x_sample):
        tabs = _tables(x.shape[1])
        outs.append(_trunk(x, layers, tabs, gfin))
    return tuple(outs)
```

```python
import functools
import math

import jax
import jax.numpy as jnp
from jax import lax
from jax.experimental import pallas as pl
from jax.experimental.pallas import tpu as pltpu

F32 = jnp.float32
BF16 = jnp.bfloat16

D_MODEL = 1024
FNET_GROUP_DIM = 64
FNET_GROUPS = 6
FNET_W = 384
S5_GROUP_DIM = 16
S5_GROUPS = 24
S5_W = 384
S5_STATE = 64
MLA_HEADS = 16
QK_NOPE = 64
QK_ROPE = 32
V_DIM = 64
Q_LORA = 384
KV_LORA = 256
ROPE_BASE = 10000.0
N_BRANCH = 3
D_FF = 4 * D_MODEL
EPS = 1e-6

OFF_FNET = 0
OFF_S5 = OFF_FNET + FNET_W
OFF_Q = OFF_S5 + S5_W
OFF_KV = OFF_Q + Q_LORA
OFF_KR = OFF_KV + KV_LORA
OFF_GATE = OFF_KR + QK_ROPE

LANES = 128
SUBLANES = 8
HEAD_PAD = 128
VMEM_LIMIT = 56 * 1024 * 1024

S5_CHUNK = SUBLANES
S5_TILES = S5_W // LANES
S5_TILE_GROUPS = LANES // S5_GROUP_DIM
S5_HALF = S5_TILE_GROUPS * S5_STATE
S5_DIR = 2 * S5_HALF
S5_BATCH = SUBLANES
S5_BLOCK_CHUNKS = 64
FNET_R2 = FNET_GROUP_DIM

ZA_CQ = OFF_Q
ZA_CKV = OFF_KV
ZA_KRA = OFF_KR
ZA_KRB = OFF_KR + LANES
ZA_W = ZA_KRB + LANES

ATTN_SUB = 512


def _rms(x, g):
    return x * lax.rsqrt(jnp.mean(x * x, axis=-1, keepdims=True) + EPS) * g


def _dot(a, b):
    return jnp.dot(a, b, preferred_element_type=F32)


def _params(*sem):
    return pltpu.CompilerParams(dimension_semantics=sem, vmem_limit_bytes=VMEM_LIMIT)


def _const_spec(shape):
    nd = len(shape)
    return pl.BlockSpec(shape, lambda *_: (0,) * nd, pipeline_mode=pl.Buffered(1))


def _inproj_kernel(x_ref, gmix_ref, wa_ref, cs_ref, gq_ref, wqa_ref, gkv_ref, wk_ref,
                   wv_ref, taq_ref, tbq_ref, tak_ref, tbk_ref, *rest):
    ucs_ref, u5_ref, q_ref, k_ref, v_ref = rest[-5:]
    h = _rms(x_ref[...], gmix_ref[...]).astype(BF16)
    z = _dot(h, wa_ref[...])
    ucs_ref[...] = _dot(z[:, OFF_FNET:OFF_S5].astype(BF16), cs_ref[...])
    for t in range(S5_TILES):
        for c in range(z.shape[0] // S5_CHUNK):
            u5_ref[t, c, 0] = z[c * S5_CHUNK:(c + 1) * S5_CHUNK, OFF_S5 + t * LANES:OFF_S5 + (t + 1) * LANES]
    hq = _rms(z[:, ZA_CQ:ZA_CQ + Q_LORA], gq_ref[...]).astype(BF16)
    hkv = _rms(z[:, ZA_CKV:ZA_CKV + KV_LORA], gkv_ref[...]).astype(BF16)
    kpe = z[:, ZA_KRA:ZA_KRA + LANES] * tak_ref[...] + z[:, ZA_KRB:ZA_KRB + LANES] * tbk_ref[...]
    kpe2 = jnp.concatenate([kpe, kpe], axis=1)
    taq2 = jnp.concatenate([taq_ref[...]] * 2, axis=1)
    tbq2 = jnp.concatenate([tbq_ref[...]] * 2, axis=1)
    first_half = (lax.broadcasted_iota(jnp.int32, taq2.shape, 1) & (HEAD_PAD - 1)) < QK_NOPE + QK_ROPE // 2
    for j in range(MLA_HEADS // 2):
        sl = slice(2 * HEAD_PAD * j, 2 * HEAD_PAD * (j + 1))
        qa = _dot(hq, wqa_ref[:, sl])
        qb = jnp.where(first_half, pltpu.roll(qa, 2 * HEAD_PAD - QK_ROPE // 2, 1), pltpu.roll(qa, QK_ROPE // 2, 1))
        q_ref[:, sl] = (qa * taq2 + qb * tbq2).astype(BF16)
        k_ref[:, sl] = (_dot(hkv, wk_ref[:, sl]) + kpe2).astype(BF16)
    vt = lax.dot_general(wv_ref[...], hkv, (((1,), (1,)), ((), ())), preferred_element_type=F32)
    ones_row = (lax.broadcasted_iota(jnp.int32, vt.shape, 0) & V_DIM) != 0
    v_ref[0] = jnp.where(ones_row, 1.0, vt).astype(BF16)


def _inproj(x, lw, tabs, bsz, seq):
    n = x.shape[0]
    tm = min(512, seq)
    nt = seq // tm
    nc = seq // S5_CHUNK
    row = lambda w: pl.BlockSpec((tm, w), lambda i: (i, 0))
    tab = pl.BlockSpec((tm, LANES), lambda i: (i % nt, 0))
    hw = MLA_HEADS * HEAD_PAD
    u5_shape = (S5_TILES, nc, S5_BATCH, S5_CHUNK, LANES)
    u5_spec = pl.BlockSpec((S5_TILES, tm // S5_CHUNK, 1, S5_CHUNK, LANES),
                           lambda i: (0, i % nt, i // nt, 0, 0))
    in_specs = [row(D_MODEL), _const_spec((1, D_MODEL)), _const_spec((D_MODEL, ZA_W)),
                _const_spec((FNET_W, 2 * FNET_W)), _const_spec((1, Q_LORA)),
                _const_spec((Q_LORA, hw)), _const_spec((1, KV_LORA)),
                _const_spec((KV_LORA, hw)), _const_spec((hw, KV_LORA)),
                tab, tab, tab, tab]
    args = [x, lw["g_mix"], lw["w_a"], tabs["cs"], lw["g_q"], lw["wq_a"], lw["g_kv"],
            lw["wk"], lw["wv_t"], tabs["taq"], tabs["tbq"], tabs["tak"], tabs["tbk"]]
    aliases = {}
    if bsz < S5_BATCH:
        aliases = {len(args): 1}
        in_specs.append(pl.BlockSpec(memory_space=pl.ANY))
        args.append(jnp.zeros(u5_shape, F32))
    return pl.pallas_call(
        _inproj_kernel,
        grid=(n // tm,),
        in_specs=in_specs,
        out_specs=[row(2 * FNET_W), u5_spec, row(hw), row(hw),
                   pl.BlockSpec((1, hw, tm), lambda i: (i // nt, 0, i % nt))],
        out_shape=[jax.ShapeDtypeStruct((n, 2 * FNET_W), F32), jax.ShapeDtypeStruct(u5_shape, F32),
                   jax.ShapeDtypeStruct((n, hw), BF16), jax.ShapeDtypeStruct((n, hw), BF16),
                   jax.ShapeDtypeStruct((bsz, hw, seq), BF16)],
        input_output_aliases=aliases,
        compiler_params=_params("parallel"),
        name="inproj",
    )(*args)


def _fnet_kernel(uc_ref, us_ref, f1_ref, f2_ref, twc_ref, tws_ref, out_ref, ar_ref, ai_ref, *, r1):
    f1 = f1_ref[...]
    for n2 in range(FNET_R2):
        rows = pl.ds(n2, r1, stride=FNET_R2)
        u = jnp.concatenate([uc_ref[0, rows, :], us_ref[0, rows, :]], axis=1).astype(BF16)
        pr = _dot(f1, u)
        ar = pr[:r1, :LANES] - pr[r1:, LANES:]
        ai = -(pr[:r1, LANES:] + pr[r1:, :LANES])
        c, s = twc_ref[n2], tws_ref[n2]
        ar_ref[n2 * r1:(n2 + 1) * r1, :] = ar * c + ai * s
        ai_ref[n2 * r1:(n2 + 1) * r1, :] = ai * c - ar * s
    f2 = f2_ref[...]
    for k1 in range(r1):
        rows = pl.ds(k1, FNET_R2, stride=r1)
        gk = jnp.concatenate([ar_ref[rows, :], ai_ref[rows, :]], axis=0).astype(BF16)
        out_ref[0, rows, :] = _dot(f2, gk)


def _fnet(ucs, tabs, bsz, seq):
    r1 = seq // FNET_R2
    nt = FNET_W // LANES
    blk = lambda off: pl.BlockSpec((1, seq, LANES), lambda b, c: (b, 0, c + off))
    return pl.pallas_call(
        functools.partial(_fnet_kernel, r1=r1),
        grid=(bsz, nt),
        in_specs=[blk(0), blk(nt), _const_spec((2 * r1, r1)), _const_spec((FNET_R2, 2 * FNET_R2)),
                  _const_spec((FNET_R2, r1, LANES)), _const_spec((FNET_R2, r1, LANES))],
        out_specs=blk(0),
        out_shape=jax.ShapeDtypeStruct((bsz, seq, FNET_W), F32),
        scratch_shapes=[pltpu.VMEM((seq, LANES), F32)] * 2,
        compiler_params=_params("parallel", "parallel"),
        name="fnet",
    )(ucs, ucs, tabs["f1"], tabs["f2"], tabs["twc"], tabs["tws"])


def _s5_rows(u_ref, rows):
    return jnp.concatenate([u_ref[0, pl.ds(s, rows, stride=S5_CHUNK), :] for s in range(S5_CHUNK)],
                           axis=1).astype(BF16)


def _s5_in_kernel(u_ref, bst_ref, loc_ref, *, rows):
    loc_ref[0] = _dot(_s5_rows(u_ref, rows), bst_ref[0])


def _s5_scan_kernel(lf_ref, lb_ref, a_ref, spf_ref, spb_ref, carry_ref, *, cb):
    @pl.when(pl.program_id(1) == 0)
    def _():
        carry_ref[...] = jnp.zeros_like(carry_ref)

    afr, afi = a_ref[0, :, :S5_HALF], a_ref[0, :, S5_HALF:S5_DIR]
    abr, abi = a_ref[0, :, S5_DIR:S5_DIR + S5_HALF], a_ref[0, :, S5_DIR + S5_HALF:]

    def step(i, carry):
        fr, fi, br, bi = carry
        rf = pl.ds(pl.multiple_of(i * S5_BATCH, S5_BATCH), S5_BATCH)
        rb = pl.ds(pl.multiple_of((cb - 1 - i) * S5_BATCH, S5_BATCH), S5_BATCH)
        spf_ref[0, rf, :S5_HALF] = fr
        spf_ref[0, rf, S5_HALF:] = fi
        spb_ref[0, rb, :S5_HALF] = br
        spb_ref[0, rb, S5_HALF:] = bi
        return (afr * fr - afi * fi + lf_ref[0, rf, :S5_HALF], afr * fi + afi * fr + lf_ref[0, rf, S5_HALF:],
                abr * br - abi * bi + lb_ref[0, rb, :S5_HALF], abr * bi + abi * br + lb_ref[0, rb, S5_HALF:])

    carry = lax.fori_loop(0, cb, step, tuple(carry_ref[k] for k in range(4)), unroll=2)
    for k in range(4):
        carry_ref[k] = carry[k]


def _s5_out_kernel(u_ref, m_ref, spf_ref, spb_ref, cst_ref, y_ref, *, rows):
    y = (_dot(_s5_rows(u_ref, rows), m_ref[0]) + _dot(spf_ref[0].astype(BF16), cst_ref[0, :S5_DIR, :])
         + _dot(spb_ref[0].astype(BF16), cst_ref[0, S5_DIR:, :]))
    for s in range(S5_CHUNK):
        y_ref[0, pl.ds(s, rows, stride=S5_CHUNK), :] = y[:, s * LANES:(s + 1) * LANES]


def _s5(u5, lw, seq):
    nc = seq // S5_CHUNK
    cb = min(S5_BLOCK_CHUNKS, nc)
    nblk = nc // cb
    rows = cb * S5_BATCH
    kw = S5_CHUNK * LANES
    u2 = u5.reshape(S5_TILES, nc * S5_BATCH * S5_CHUNK, LANES)
    u_spec = pl.BlockSpec((1, rows * S5_CHUNK, LANES), lambda t, j: (t, j, 0))
    wspec = lambda a, b: pl.BlockSpec((1, a, b), lambda t, j: (t, 0, 0))
    loc = pl.pallas_call(
        functools.partial(_s5_in_kernel, rows=rows),
        grid=(S5_TILES, nblk),
        in_specs=[u_spec, wspec(kw, 2 * S5_DIR)],
        out_specs=pl.BlockSpec((1, rows, 2 * S5_DIR), lambda t, j: (t, j, 0)),
        out_shape=jax.ShapeDtypeStruct((S5_TILES, nc * S5_BATCH, 2 * S5_DIR), F32),
        compiler_params=_params("parallel", "parallel"),
        name="s5_in",
    )(u2, lw["s5_bst"])
    fwd = lambda lane: pl.BlockSpec((1, rows, S5_DIR), lambda t, j: (t, j, lane))
    bwd = lambda lane: pl.BlockSpec((1, rows, S5_DIR), lambda t, j: (t, nblk - 1 - j, lane))
    sp_shape = jax.ShapeDtypeStruct((S5_TILES, nc * S5_BATCH, S5_DIR), F32)
    spf, spb = pl.pallas_call(
        functools.partial(_s5_scan_kernel, cb=cb),
        grid=(S5_TILES, nblk),
        in_specs=[fwd(0), bwd(1), wspec(S5_BATCH, 2 * S5_DIR)],
        out_specs=[fwd(0), bwd(0)],
        out_shape=[sp_shape, sp_shape],
        scratch_shapes=[pltpu.VMEM((4, S5_BATCH, S5_HALF), F32)],
        compiler_params=_params("parallel", "arbitrary"),
        name="s5_scan",
    )(loc, loc, lw["s5_a"])
    y2 = pl.pallas_call(
        functools.partial(_s5_out_kernel, rows=rows),
        grid=(S5_TILES, nblk),
        in_specs=[u_spec, wspec(kw, kw), fwd(0), fwd(0), wspec(2 * S5_DIR, kw)],
        out_specs=u_spec,
        out_shape=jax.ShapeDtypeStruct(u2.shape, F32),
        compiler_params=_params("parallel", "parallel"),
        name="s5_out",
    )(u2, lw["s5_m"], spf, spb, lw["s5_cst"])
    return y2.reshape(u5.shape)


def _attn_kernel(q_ref, k_ref, vt_ref, o_ref, s_ref, m_ref, p_ref, *, sub):
    nsub = q_ref.shape[1] // sub

    def scores(u, j, slot):
        sl = slice(HEAD_PAD * j, HEAD_PAD * (j + 1))
        q = q_ref[0, pl.ds(pl.multiple_of(u * sub, sub), sub), sl]
        st = lax.dot_general(k_ref[0, :, sl], q, (((1,), (1,)), ((), ())),
                             preferred_element_type=F32)
        s_ref[slot] = st
        m_ref[slot] = jnp.max(st, axis=0, keepdims=True)

    def finish(j, slot):
        p_ref[...] = jnp.exp2(s_ref[slot] - m_ref[slot]).astype(BF16)
        ol = _dot(vt_ref[0, HEAD_PAD * j:HEAD_PAD * (j + 1), :], p_ref[...])
        return ol[:V_DIM] / ol[V_DIM:]

    scores(0, 0, 0)

    def body(u, carry):
        scores(u, 1, 1)
        o0 = finish(0, 0)
        scores(jnp.minimum(u + 1, nsub - 1), 0, 0)
        o1 = finish(1, 1)
        rows = pl.ds(pl.multiple_of(u * sub, sub), sub)
        o_ref[0, rows, :] = jnp.concatenate([o0, o1], axis=0).T.astype(BF16)
        return carry

    lax.fori_loop(0, nsub, body, 0)


def _attention(q, k, vt, bsz, seq):
    sub = min(ATTN_SUB, seq)
    hw = MLA_HEADS * HEAD_PAD
    blk = lambda w: pl.BlockSpec((1, seq, w), lambda b, h: (b, 0, h))
    return pl.pallas_call(
        functools.partial(_attn_kernel, sub=sub),
        grid=(bsz, MLA_HEADS // 2),
        in_specs=[blk(2 * HEAD_PAD), blk(2 * HEAD_PAD),
                  pl.BlockSpec((1, 2 * HEAD_PAD, seq), lambda b, h: (b, h, 0))],
        out_specs=blk(2 * V_DIM),
        out_shape=jax.ShapeDtypeStruct((bsz, seq, MLA_HEADS * V_DIM), BF16),
        scratch_shapes=[pltpu.VMEM((2, seq, sub), F32), pltpu.VMEM((2, 1, sub), F32),
                        pltpu.VMEM((seq, sub), BF16)],
        compiler_params=_params("parallel", "parallel"),
        name="attention",
    )(q.reshape(bsz, seq, hw), k.reshape(bsz, seq, hw), vt)


def _merge_kernel(x_ref, yf_ref, y5_ref, o_ref, gmix_ref, wfn_ref, wglu_ref, ws5_ref, wo_ref,
                  wg_ref, wout_ref, out_ref):
    x = x_ref[...]
    h = _rms(x, gmix_ref[...]).astype(BF16)
    y_a = _dot(yf_ref[...].astype(BF16), wfn_ref[...])
    nchunk = y5_ref.shape[1]
    ys = jnp.concatenate(
        [jnp.concatenate([y5_ref[t, c, 0] for c in range(nchunk)], axis=0) for t in range(S5_TILES)],
        axis=1)
    s = jax.nn.gelu(ys).astype(BF16)
    hg = _dot(s, wglu_ref[...])
    glu = (hg[:, :S5_W] * jax.nn.sigmoid(hg[:, S5_W:])).astype(BF16)
    y_b = _dot(glu, ws5_ref[...])
    y_c = _dot(o_ref[...], wo_ref[...])
    merged = jax.nn.sigmoid(_dot(h, wg_ref[:, :D_MODEL])) * y_a
    merged += jax.nn.sigmoid(_dot(h, wg_ref[:, D_MODEL:2 * D_MODEL])) * y_b
    merged += jax.nn.sigmoid(_dot(h, wg_ref[:, 2 * D_MODEL:])) * y_c
    out_ref[...] = x + _dot(merged.astype(BF16), wout_ref[...])


def _merge(x, yf, y5, o, lw, seq):
    n = x.shape[0]
    tm = min(1024, seq)
    nt = seq // tm
    row = lambda w: pl.BlockSpec((tm, w), lambda i: (i, 0))
    y5_spec = pl.BlockSpec((S5_TILES, tm // S5_CHUNK, 1, S5_CHUNK, LANES),
                           lambda i: (0, i % nt, i // nt, 0, 0))
    return pl.pallas_call(
        _merge_kernel,
        grid=(n // tm,),
        in_specs=[row(D_MODEL), row(FNET_W), y5_spec, row(MLA_HEADS * V_DIM),
                  _const_spec((1, D_MODEL)), _const_spec((FNET_W, D_MODEL)),
                  _const_spec((S5_W, 2 * S5_W)), _const_spec((S5_W, D_MODEL)),
                  _const_spec((MLA_HEADS * V_DIM, D_MODEL)),
                  _const_spec((D_MODEL, N_BRANCH * D_MODEL)), _const_spec((D_MODEL, D_MODEL))],
        out_specs=row(D_MODEL),
        out_shape=jax.ShapeDtypeStruct((n, D_MODEL), F32),
        compiler_params=_params("parallel"),
        name="merge",
    )(x, yf, y5, o, lw["g_mix"], lw["w_fnet"], lw["w_glu"], lw["w_s5"], lw["w_o"], lw["w_gate"],
      lw["w_out"])


def _mlp_kernel(x_ref, g_ref, wup_ref, wdn_ref, gfin_ref, out_ref, *, final):
    x = x_ref[...]
    h = _rms(x, g_ref[...]).astype(BF16)
    a = jnp.square(jnp.maximum(_dot(h, wup_ref[...]), 0.0)).astype(BF16)
    y = x + _dot(a, wdn_ref[...])
    if final:
        y = _rms(y, gfin_ref[...])
    out_ref[...] = y


def _mlp(x, lw, g_final, final):
    n = x.shape[0]
    tm = min(512, n)
    resident = lambda shape: pl.BlockSpec(shape, lambda i: (0, 0), pipeline_mode=pl.Buffered(1))
    return pl.pallas_call(
        functools.partial(_mlp_kernel, final=final),
        grid=(n // tm,),
        in_specs=[pl.BlockSpec((tm, D_MODEL), lambda i: (i, 0)), _const_spec((1, D_MODEL)),
                  resident((D_MODEL, D_FF)), resident((D_FF, D_MODEL)), _const_spec((1, D_MODEL))],
        out_specs=pl.BlockSpec((tm, D_MODEL), lambda i: (i, 0)),
        out_shape=jax.ShapeDtypeStruct((n, D_MODEL), F32),
        compiler_params=_params("parallel"),
        name="mlp",
    )(x, lw["g_mlp"], lw["w_up"], lw["w_down"], g_final)


def _tables(seq):
    half = QK_ROPE // 2
    inv = ROPE_BASE ** (-jnp.arange(half, dtype=F32) / half)
    ang = jnp.arange(seq, dtype=F32)[:, None] * inv[None, :]
    cos, sin = jnp.cos(ang), jnp.sin(ang)
    one = jnp.ones((seq, QK_NOPE), F32)
    z64 = jnp.zeros((seq, QK_NOPE), F32)
    z32 = jnp.zeros((seq, HEAD_PAD - QK_NOPE - QK_ROPE), F32)
    scale = (QK_NOPE + QK_ROPE) ** -0.5 * math.log2(math.e)
    ta = jnp.concatenate([z64, cos, cos, z32], axis=1)
    tb = jnp.concatenate([z64, -sin, sin, z32], axis=1)
    taq = jnp.concatenate([one, cos, cos, z32], axis=1) * scale
    c = jnp.arange(FNET_GROUP_DIM)
    ang64 = (2.0 * math.pi / FNET_GROUP_DIM) * ((c[:, None] * c[None, :]) % FNET_GROUP_DIM).astype(F32)
    eye = jnp.eye(FNET_GROUPS, dtype=F32)
    norm = 1.0 / math.sqrt(seq * FNET_GROUP_DIM)
    cs = jnp.concatenate([jnp.kron(eye, jnp.cos(ang64)), jnp.kron(eye, jnp.sin(ang64))], axis=1) * norm
    r1 = seq // FNET_R2
    i1 = jnp.arange(r1)
    a1 = (2.0 * math.pi / r1) * ((i1[:, None] * i1[None, :]) % r1).astype(F32)
    f1 = jnp.concatenate([jnp.cos(a1), jnp.sin(a1)], axis=0).astype(BF16)
    f2 = jnp.concatenate([jnp.cos(ang64), jnp.sin(ang64)], axis=1).astype(BF16)
    atw = (2.0 * math.pi / seq) * (c[:, None] * i1[None, :]).astype(F32)
    twc = jnp.broadcast_to(jnp.cos(atw)[:, :, None], (FNET_R2, r1, LANES))
    tws = jnp.broadcast_to(jnp.sin(atw)[:, :, None], (FNET_R2, r1, LANES))
    return dict(taq=taq, tbq=tb * scale, tak=ta, tbk=tb, cs=cs.astype(BF16), f1=f1, f2=f2, twc=twc, tws=tws)


def _pad_heads(w, width):
    k = w.shape[0]
    w = w.reshape(k, MLA_HEADS, width)
    return jnp.pad(w, ((0, 0), (0, 0), (0, HEAD_PAD - width))).reshape(k, MLA_HEADS * HEAD_PAD)


def _cmul(ar, ai, br, bi):
    return ar * br - ai * bi, ar * bi + ai * br


def _s5_tables(lam_re, lam_im, log_dt, b_re, b_im, c_re, c_im, d_skip):
    t, p, g, ns = S5_CHUNK, S5_GROUP_DIM, S5_GROUPS, S5_STATE
    nt, tg = S5_TILES, S5_TILE_GROUPS
    dt = jnp.exp(log_dt)[..., None]
    ar, ai = lam_re * dt, lam_im * dt
    mag = jnp.exp(ar)
    lbr, lbi = mag * jnp.cos(ai), mag * jnp.sin(ai)
    den = lam_re * lam_re + lam_im * lam_im
    cfr = ((lbr - 1.0) * lam_re + lbi * lam_im) / den
    cfi = (lbi * lam_re - (lbr - 1.0) * lam_im) / den
    bbr, bbi = _cmul(cfr[..., None], cfi[..., None], b_re, b_im)
    d = jnp.arange(t + 1, dtype=F32)
    pmag = jnp.exp(ar[..., None] * d)
    pwr, pwi = pmag * jnp.cos(ai[..., None] * d), pmag * jnp.sin(ai[..., None] * d)
    cpr, cpi = _cmul(c_re[..., None], c_im[..., None], pwr[:, :, None, :, :t], pwi[:, :, None, :, :t])
    kern = (jnp.einsum('xgpnd,xgnq->xgdpq', cpr, bbr)
            - jnp.einsum('xgpnd,xgnq->xgdpq', cpi, bbi))
    skip = jnp.eye(p, dtype=F32)[None] * d_skip.reshape(g, p)[:, :, None]
    k0 = kern[0][:, :1] + kern[1][:, :1] + skip[:, None]
    kfull = jnp.concatenate([kern[1][:, :0:-1], k0, kern[0][:, 1:]], axis=1)
    eye = jnp.eye(tg, dtype=F32)
    kq = kfull.transpose(0, 1, 3, 2).reshape(nt, tg, 2 * t - 1, p, p)
    bd = jnp.einsum('Ggdqp,gh->Gdgqhp', kq, eye).reshape(nt, 2 * t - 1, LANES, LANES)
    lag = jnp.arange(t)[None, :] - jnp.arange(t)[:, None] + (t - 1)
    m = bd[:, lag].transpose(0, 1, 3, 2, 4).reshape(nt, t * LANES, t * LANES)
    rev = lambda v: v[..., ::-1]
    lay_p = lambda v: v.reshape(2, nt, tg * ns, t).transpose(0, 1, 3, 2)
    psr = lay_p(jnp.stack([rev(pwr[0])[..., 1:], pwr[1][..., :t]]))
    psi = lay_p(jnp.stack([rev(pwi[0])[..., 1:], pwi[1][..., :t]]))
    lay_b = lambda v: jnp.einsum('xGhnq,gh->xGgqhn', v.reshape(2, nt, tg, ns, p), eye).reshape(
        2, nt, LANES, tg * ns)
    bmr, bmi = lay_b(bbr), lay_b(bbi)
    bsr, bsi = _cmul(psr[:, :, :, None, :], psi[:, :, :, None, :], bmr[:, :, None], bmi[:, :, None])
    bst = jnp.stack([bsr, bsi], axis=1).transpose(2, 3, 4, 0, 1, 5).reshape(nt, t * LANES, 2 * S5_DIR)
    lay_c = lambda v: jnp.einsum('xGhpn,gh->xGgnhp', v.reshape(2, nt, tg, p, ns), eye).reshape(
        2, nt, tg * ns, LANES)
    cmr, cmi = lay_c(c_re), lay_c(c_im)
    por = jnp.stack([pwr[0][..., 1:], rev(pwr[1])[..., :t]]).reshape(2, nt, tg * ns, t)
    poi = jnp.stack([pwi[0][..., 1:], rev(pwi[1])[..., :t]]).reshape(2, nt, tg * ns, t)
    csr, csi = _cmul(cmr[:, :, :, None, :], cmi[:, :, :, None, :], por[..., None], poi[..., None])
    cst = jnp.stack([csr, -csi], axis=1).transpose(2, 0, 1, 3, 4, 5).reshape(nt, 2 * S5_DIR, t * LANES)
    a = jnp.stack([pwr[..., t], pwi[..., t]], axis=1)
    a = a.reshape(2, 2, nt, tg, ns).transpose(2, 0, 1, 3, 4).reshape(nt, 1, 2 * S5_DIR)
    a = jnp.broadcast_to(a, (nt, S5_BATCH, 2 * S5_DIR))
    return dict(s5_m=m.astype(BF16), s5_bst=bst.astype(BF16), s5_cst=cst.astype(BF16), s5_a=a.astype(F32))


def _layer_weights(i, p):
    w_in = p["w_in"][i]
    k_dim = w_in.shape[0]
    half = QK_ROPE // 2
    kr = w_in[:, OFF_KR:OFF_GATE]
    z64 = jnp.zeros((k_dim, QK_NOPE), F32)
    z32 = jnp.zeros((k_dim, HEAD_PAD - QK_NOPE - QK_ROPE), F32)
    w_a = jnp.concatenate([w_in[:, :OFF_KR], z64, kr, z32, z64, kr[:, half:], kr[:, :half], z32], axis=1)
    wkv = p["w_kvb"][i].reshape(KV_LORA, MLA_HEADS, QK_NOPE + V_DIM)
    row = lambda v: v.reshape(1, -1).astype(F32)
    lw = dict(
        g_mix=row(p["g_mix"][i]), w_a=w_a.astype(BF16), w_gate=w_in[:, OFF_GATE:].astype(BF16),
        g_q=row(p["g_q"][i]), g_kv=row(p["g_kv"][i]),
        wq_a=_pad_heads(p["w_qb"][i], QK_NOPE + QK_ROPE).astype(BF16),
        wk=_pad_heads(wkv[:, :, :QK_NOPE].reshape(KV_LORA, -1), QK_NOPE).astype(BF16),
        wv_t=_pad_heads(wkv[:, :, QK_NOPE:].reshape(KV_LORA, -1), V_DIM).T.astype(BF16),
        w_fnet=p["w_fnet"][i].astype(BF16), w_glu=p["w_glu"][i].astype(BF16),
        w_s5=p["w_s5"][i].astype(BF16), w_o=p["w_o_mla"][i].astype(BF16),
        w_out=p["w_out"][i].astype(BF16), g_mlp=row(p["g_mlp"][i]),
        w_up=p["w_up"][i].astype(BF16), w_down=p["w_down"][i].astype(BF16))
    lw.update(_s5_tables(p["s5_lam_re"][i], p["s5_lam_im"][i], p["s5_log_dt"][i], p["s5_b_re"][i],
                         p["s5_b_im"][i], p["s5_c_re"][i], p["s5_c_im"][i], p["s5_d"][i]))
    return lw


def _trunk(x, layers, tabs, g_final):
    bsz, seq, _ = x.shape
    assert bsz <= S5_BATCH and seq % S5_CHUNK == 0
    x = x.reshape(bsz * seq, D_MODEL)
    for i, lw in enumerate(layers):
        ucs, u5, q, k, vt = _inproj(x, lw, tabs, bsz, seq)
        yf = _fnet(ucs.reshape(bsz, seq, 2 * FNET_W), tabs, bsz, seq).reshape(bsz * seq, FNET_W)
        y5 = _s5(u5, lw, seq)
        o = _attention(q, k, vt, bsz, seq).reshape(bsz * seq, MLA_HEADS * V_DIM)
        x = _merge(x, yf, y5, o, lw, seq)
        x = _mlp(x, lw, g_final, final=(i == len(layers) - 1))
    return x.reshape(bsz, seq, D_MODEL)


def kernel(x_prompt, x_sample, g_mix, w_in, w_fnet, s5_lam_re, s5_lam_im, s5_log_dt, s5_b_re,
           s5_b_im, s5_c_re, s5_c_im, s5_d, w_glu, w_s5, g_q, w_qb, g_kv, w_kvb, w_o_mla,
           w_out, g_mlp, w_up, w_down, g_final):
    p = dict(g_mix=g_mix, w_in=w_in, w_fnet=w_fnet, s5_lam_re=s5_lam_re, s5_lam_im=s5_lam_im,
             s5_log_dt=s5_log_dt, s5_b_re=s5_b_re, s5_b_im=s5_b_im, s5_c_re=s5_c_re,
             s5_c_im=s5_c_im, s5_d=s5_d, w_glu=w_glu, w_s5=w_s5, g_q=g_q, w_qb=w_qb, g_kv=g_kv,
             w_kvb=w_kvb, w_o_mla=w_o_mla, w_out=w_out, g_mlp=g_mlp, w_up=w_up, w_down=w_down)
    layers = [_layer_weights(i, p) for i in range(g_mix.shape[0])]
    gfin = g_final.reshape(1, -1).astype(F32)
    outs = []
    for x in (x_prompt, x_sample):
        tabs = _tables(x.shape[1])
        outs.append(_trunk(x, layers, tabs, gfin))
    return tuple(outs)
```

```python
import functools
import math

import jax
import jax.numpy as jnp
from jax import lax
from jax.experimental import pallas as pl
from jax.experimental.pallas import tpu as pltpu

F32 = jnp.float32
BF16 = jnp.bfloat16

D_MODEL = 1024
FNET_GROUP_DIM = 64
FNET_GROUPS = 6
FNET_W = 384
S5_GROUP_DIM = 16
S5_GROUPS = 24
S5_W = 384
S5_STATE = 64
MLA_HEADS = 16
QK_NOPE = 64
QK_ROPE = 32
V_DIM = 64
Q_LORA = 384
KV_LORA = 256
ROPE_BASE = 10000.0
N_BRANCH = 3
D_FF = 4 * D_MODEL
EPS = 1e-6

OFF_FNET = 0
OFF_S5 = OFF_FNET + FNET_W
OFF_Q = OFF_S5 + S5_W
OFF_KV = OFF_Q + Q_LORA
OFF_KR = OFF_KV + KV_LORA
OFF_GATE = OFF_KR + QK_ROPE

LANES = 128
SUBLANES = 8
HEAD_PAD = 128
VMEM_LIMIT = 56 * 1024 * 1024

S5_CHUNK = SUBLANES
S5_TILES = S5_W // LANES
S5_TILE_GROUPS = LANES // S5_GROUP_DIM
S5_HALF = S5_TILE_GROUPS * S5_STATE
S5_DIR = 2 * S5_HALF
S5_BLOCK_ROWS = 512
FNET_R2 = FNET_GROUP_DIM
FNET_PITCH = FNET_R2 + SUBLANES

ZA_CQ = OFF_Q
ZA_CKV = OFF_KV
ZA_KRA = OFF_KR
ZA_KRB = OFF_KR + LANES
ZA_W = ZA_KRB + LANES

ATTN_SUB = 512


def _rms(x, g):
    return x * lax.rsqrt(jnp.mean(x * x, axis=-1, keepdims=True) + EPS) * g


def _dot(a, b):
    return jnp.dot(a, b, preferred_element_type=F32)


def _params(*sem):
    return pltpu.CompilerParams(dimension_semantics=sem, vmem_limit_bytes=VMEM_LIMIT)


def _const_spec(shape):
    nd = len(shape)
    return pl.BlockSpec(shape, lambda *_: (0,) * nd, pipeline_mode=pl.Buffered(1))


def _inproj_kernel(x_ref, gmix_ref, wa_ref, cs_ref, gq_ref, wqa_ref, gkv_ref, wk_ref,
                   wv_ref, taq_ref, tbq_ref, tak_ref, tbk_ref, *rest):
    ucs_ref, u5_ref, q_ref, k_ref, v_ref = rest[-5:]
    h = _rms(x_ref[...], gmix_ref[...]).astype(BF16)
    z = _dot(h, wa_ref[...])
    ucs = _dot(z[:, OFF_FNET:OFF_S5].astype(BF16), cs_ref[...])
    for blk in range(ucs.shape[0] // FNET_R2):
        ucs_ref[blk, :FNET_R2, :] = ucs[blk * FNET_R2:(blk + 1) * FNET_R2]
        ucs_ref[blk, FNET_R2:, :] = jnp.zeros((FNET_PITCH - FNET_R2, 2 * FNET_W), F32)
    for t in range(S5_TILES):
        for c in range(z.shape[0] // S5_CHUNK):
            u5_ref[t, c, 0] = z[c * S5_CHUNK:(c + 1) * S5_CHUNK, OFF_S5 + t * LANES:OFF_S5 + (t + 1) * LANES]
    hq = _rms(z[:, ZA_CQ:ZA_CQ + Q_LORA], gq_ref[...]).astype(BF16)
    hkv = _rms(z[:, ZA_CKV:ZA_CKV + KV_LORA], gkv_ref[...]).astype(BF16)
    kpe = z[:, ZA_KRA:ZA_KRA + LANES] * tak_ref[...] + z[:, ZA_KRB:ZA_KRB + LANES] * tbk_ref[...]
    kpe2 = jnp.concatenate([kpe, kpe], axis=1)
    taq2 = jnp.concatenate([taq_ref[...]] * 2, axis=1)
    tbq2 = jnp.concatenate([tbq_ref[...]] * 2, axis=1)
    first_half = (lax.broadcasted_iota(jnp.int32, taq2.shape, 1) & (HEAD_PAD - 1)) < QK_NOPE + QK_ROPE // 2
    for j in range(MLA_HEADS // 2):
        sl = slice(2 * HEAD_PAD * j, 2 * HEAD_PAD * (j + 1))
        qa = _dot(hq, wqa_ref[:, sl])
        qb = jnp.where(first_half, pltpu.roll(qa, 2 * HEAD_PAD - QK_ROPE // 2, 1), pltpu.roll(qa, QK_ROPE // 2, 1))
        q_ref[:, sl] = (qa * taq2 + qb * tbq2).astype(BF16)
        k_ref[:, sl] = (_dot(hkv, wk_ref[:, sl]) + kpe2).astype(BF16)
    vt = lax.dot_general(wv_ref[...], hkv, (((1,), (1,)), ((), ())), preferred_element_type=F32)
    ones_row = (lax.broadcasted_iota(jnp.int32, vt.shape, 0) & V_DIM) != 0
    v_ref[0] = jnp.where(ones_row, 1.0, vt).astype(BF16)


def _inproj(x, lw, tabs, bsz, seq):
    n = x.shape[0]
    tm = min(512, seq)
    nt = seq // tm
    nc = seq // S5_CHUNK
    row = lambda w: pl.BlockSpec((tm, w), lambda i: (i, 0))
    tab = pl.BlockSpec((tm, LANES), lambda i: (i % nt, 0))
    hw = MLA_HEADS * HEAD_PAD
    bs = _s5_batch(bsz)
    u5_shape = (S5_TILES, nc, bs, S5_CHUNK, LANES)
    u5_spec = pl.BlockSpec((S5_TILES, tm // S5_CHUNK, 1, S5_CHUNK, LANES),
                           lambda i: (0, i % nt, i // nt, 0, 0))
    in_specs = [row(D_MODEL), _const_spec((1, D_MODEL)), _const_spec((D_MODEL, ZA_W)),
                _const_spec((FNET_W, 2 * FNET_W)), _const_spec((1, Q_LORA)),
                _const_spec((Q_LORA, hw)), _const_spec((1, KV_LORA)),
                _const_spec((KV_LORA, hw)), _const_spec((hw, KV_LORA)),
                tab, tab, tab, tab]
    args = [x, lw["g_mix"], lw["w_a"], tabs["cs"], lw["g_q"], lw["wq_a"], lw["g_kv"],
            lw["wk"], lw["wv_t"], tabs["taq"], tabs["tbq"], tabs["tak"], tabs["tbk"]]
    aliases = {}
    if bsz < bs:
        aliases = {len(args): 1}
        in_specs.append(pl.BlockSpec(memory_space=pl.ANY))
        args.append(jnp.zeros(u5_shape, F32))
    return pl.pallas_call(
        _inproj_kernel,
        grid=(n // tm,),
        in_specs=in_specs,
        out_specs=[pl.BlockSpec((tm // FNET_R2, FNET_PITCH, 2 * FNET_W), lambda i: (i, 0, 0)), u5_spec, row(hw), row(hw),
                   pl.BlockSpec((1, hw, tm), lambda i: (i // nt, 0, i % nt))],
        out_shape=[jax.ShapeDtypeStruct((n // FNET_R2, FNET_PITCH, 2 * FNET_W), F32), jax.ShapeDtypeStruct(u5_shape, F32),
                   jax.ShapeDtypeStruct((n, hw), BF16), jax.ShapeDtypeStruct((n, hw), BF16),
                   jax.ShapeDtypeStruct((bsz, hw, seq), BF16)],
        input_output_aliases=aliases,
        compiler_params=_params("parallel"),
        name="inproj",
    )(*args)


def _fnet_kernel(uc_ref, us_ref, f1_ref, f2_ref, twc_ref, tws_ref, out_ref, ar_ref, ai_ref, *, r1):
    f1 = f1_ref[...]
    for n2 in range(FNET_R2):
        rows = pl.ds(n2, r1, stride=FNET_PITCH)
        u = jnp.concatenate([uc_ref[0, rows, :], us_ref[0, rows, :]], axis=1).astype(BF16)
        pr = _dot(f1, u)
        ar = pr[:r1, :LANES] - pr[r1:, LANES:]
        ai = -(pr[:r1, LANES:] + pr[r1:, :LANES])
        c, s = twc_ref[n2], tws_ref[n2]
        ar_ref[n2 * r1:(n2 + 1) * r1, :] = ar * c + ai * s
        ai_ref[n2 * r1:(n2 + 1) * r1, :] = ai * c - ar * s
    f2 = f2_ref[...]
    for k1 in range(r1):
        rows = pl.ds(k1, FNET_R2, stride=r1)
        gk = jnp.concatenate([ar_ref[rows, :], ai_ref[rows, :]], axis=0).astype(BF16)
        out_ref[0, rows, :] = _dot(f2, gk)


def _fnet(ucs, tabs, bsz, seq):
    r1 = seq // FNET_R2
    nt = FNET_W // LANES
    blk = lambda off: pl.BlockSpec((1, seq, LANES), lambda b, c: (b, 0, c + off))
    padded = lambda off: pl.BlockSpec((1, r1 * FNET_PITCH, LANES), lambda b, c: (b, 0, c + off))
    return pl.pallas_call(
        functools.partial(_fnet_kernel, r1=r1),
        grid=(bsz, nt),
        in_specs=[padded(0), padded(nt), _const_spec((2 * r1, r1)), _const_spec((FNET_R2, 2 * FNET_R2)),
                  _const_spec((FNET_R2, r1, LANES)), _const_spec((FNET_R2, r1, LANES))],
        out_specs=blk(0),
        out_shape=jax.ShapeDtypeStruct((bsz, seq, FNET_W), F32),
        scratch_shapes=[pltpu.VMEM((seq, LANES), F32)] * 2,
        compiler_params=_params("parallel", "parallel"),
        name="fnet",
    )(ucs, ucs, tabs["f1"], tabs["f2"], tabs["twc"], tabs["tws"])


def _s5_rows(u_ref, rows):
    return jnp.concatenate([u_ref[0, pl.ds(s, rows, stride=S5_CHUNK), :] for s in range(S5_CHUNK)],
                           axis=1).astype(BF16)


def _s5_in_kernel(u_ref, bst_ref, loc_ref, *, rows):
    loc_ref[0] = _dot(_s5_rows(u_ref, rows), bst_ref[0])


def _s5_batch(bsz):
    assert bsz <= SUBLANES
    return SUBLANES // 2 if bsz <= SUBLANES // 2 else SUBLANES


def _s5_scan_kernel(lf_ref, lb_ref, a_ref, spf_ref, spb_ref, carry_ref, *, tiles, bs):
    @pl.when(pl.program_id(1) == 0)
    def _():
        carry_ref[...] = jnp.zeros_like(carry_ref)

    afr, afi = a_ref[0, :, :S5_HALF], a_ref[0, :, S5_HALF:S5_DIR]
    abr, abi = a_ref[0, :, S5_DIR:S5_DIR + S5_HALF], a_ref[0, :, S5_DIR + S5_HALF:]

    low = lax.broadcasted_iota(jnp.int32, (SUBLANES, S5_HALF), 0) < SUBLANES // 2
    swap = lambda v: pltpu.roll(v, SUBLANES // 2, 0)

    def advance(ar, ai, xr, xi, lr, li):
        return ar * xr - ai * xi + lr, ar * xi + ai * xr + li

    def step(i, carry):
        fr, fi, br, bi = carry
        rf = pl.ds(pl.multiple_of(i * SUBLANES, SUBLANES), SUBLANES)
        rb = pl.ds(pl.multiple_of((tiles - 1 - i) * SUBLANES, SUBLANES), SUBLANES)
        lfr, lfi = lf_ref[0, rf, :S5_HALF], lf_ref[0, rf, S5_HALF:]
        lbr, lbi = lb_ref[0, rb, :S5_HALF], lb_ref[0, rb, S5_HALF:]
        nfr, nfi = advance(afr, afi, fr, fi, lfr, lfi)
        nbr, nbi = advance(abr, abi, br, bi, lbr, lbi)
        if bs == SUBLANES:
            spf_ref[0, rf, :S5_HALF], spf_ref[0, rf, S5_HALF:] = fr, fi
            spb_ref[0, rb, :S5_HALF], spb_ref[0, rb, S5_HALF:] = br, bi
            return nfr, nfi, nbr, nbi
        mfr, mfi = swap(nfr), swap(nfi)
        spf_ref[0, rf, :S5_HALF] = jnp.where(low, fr, mfr)
        spf_ref[0, rf, S5_HALF:] = jnp.where(low, fi, mfi)
        nfr, nfi = advance(afr, afi, mfr, mfi, lfr, lfi)
        mbr, mbi = swap(nbr), swap(nbi)
        spb_ref[0, rb, :S5_HALF] = jnp.where(low, mbr, br)
        spb_ref[0, rb, S5_HALF:] = jnp.where(low, mbi, bi)
        nbr, nbi = advance(abr, abi, mbr, mbi, lbr, lbi)
        return (jnp.where(low, swap(nfr), nfr), jnp.where(low, swap(nfi), nfi),
                jnp.where(low, nbr, swap(nbr)), jnp.where(low, nbi, swap(nbi)))

    carry = lax.fori_loop(0, tiles, step, tuple(carry_ref[k] for k in range(4)), unroll=2)
    for k in range(4):
        carry_ref[k] = carry[k]


def _s5_out_kernel(u_ref, m_ref, spf_ref, spb_ref, cst_ref, y_ref, *, rows):
    y = (_dot(_s5_rows(u_ref, rows), m_ref[0]) + _dot(spf_ref[0].astype(BF16), cst_ref[0, :S5_DIR, :])
         + _dot(spb_ref[0].astype(BF16), cst_ref[0, S5_DIR:, :]))
    for s in range(S5_CHUNK):
        y_ref[0, pl.ds(s, rows, stride=S5_CHUNK), :] = y[:, s * LANES:(s + 1) * LANES]


def _s5(u5, lw, seq):
    nc, bs = u5.shape[1], u5.shape[2]
    rows = min(S5_BLOCK_ROWS, nc * bs)
    nblk = nc * bs // rows
    kw = S5_CHUNK * LANES
    u2 = u5.reshape(S5_TILES, nc * bs * S5_CHUNK, LANES)
    u_spec = pl.BlockSpec((1, rows * S5_CHUNK, LANES), lambda t, j: (t, j, 0))
    wspec = lambda a, b: pl.BlockSpec((1, a, b), lambda t, j: (t, 0, 0))
    loc = pl.pallas_call(
        functools.partial(_s5_in_kernel, rows=rows),
        grid=(S5_TILES, nblk),
        in_specs=[u_spec, wspec(kw, 2 * S5_DIR)],
        out_specs=pl.BlockSpec((1, rows, 2 * S5_DIR), lambda t, j: (t, j, 0)),
        out_shape=jax.ShapeDtypeStruct((S5_TILES, nc * bs, 2 * S5_DIR), F32),
        compiler_params=_params("parallel", "parallel"),
        name="s5_in",
    )(u2, lw["s5_bst"])
    fwd = lambda lane: pl.BlockSpec((1, rows, S5_DIR), lambda t, j: (t, j, lane))
    bwd = lambda lane: pl.BlockSpec((1, rows, S5_DIR), lambda t, j: (t, nblk - 1 - j, lane))
    sp_shape = jax.ShapeDtypeStruct((S5_TILES, nc * bs, S5_DIR), F32)
    spf, spb = pl.pallas_call(
        functools.partial(_s5_scan_kernel, tiles=rows // SUBLANES, bs=bs),
        grid=(S5_TILES, nblk),
        in_specs=[fwd(0), bwd(1), wspec(SUBLANES, 2 * S5_DIR)],
        out_specs=[fwd(0), bwd(0)],
        out_shape=[sp_shape, sp_shape],
        scratch_shapes=[pltpu.VMEM((4, SUBLANES, S5_HALF), F32)],
        compiler_params=_params("parallel", "arbitrary"),
        name="s5_scan",
    )(loc, loc, lw["s5_a"])
    y2 = pl.pallas_call(
        functools.partial(_s5_out_kernel, rows=rows),
        grid=(S5_TILES, nblk),
        in_specs=[u_spec, wspec(kw, kw), fwd(0), fwd(0), wspec(2 * S5_DIR, kw)],
        out_specs=u_spec,
        out_shape=jax.ShapeDtypeStruct(u2.shape, F32),
        compiler_params=_params("parallel", "parallel"),
        name="s5_out",
    )(u2, lw["s5_m"], spf, spb, lw["s5_cst"])
    return y2.reshape(u5.shape)


def _attn_kernel(q_ref, k_ref, vt_ref, o_ref, s_ref, m_ref, p_ref, *, sub):
    nsub = q_ref.shape[1] // sub

    def scores(u, j, slot):
        sl = slice(HEAD_PAD * j, HEAD_PAD * (j + 1))
        q = q_ref[0, pl.ds(pl.multiple_of(u * sub, sub), sub), sl]
        st = lax.dot_general(k_ref[0, :, sl], q, (((1,), (1,)), ((), ())),
                             preferred_element_type=F32)
        s_ref[slot] = st
        m_ref[slot] = jnp.max(st, axis=0, keepdims=True)

    def finish(j, slot):
        p_ref[...] = jnp.exp2(s_ref[slot] - m_ref[slot]).astype(BF16)
        ol = _dot(vt_ref[0, HEAD_PAD * j:HEAD_PAD * (j + 1), :], p_ref[...])
        return ol[:V_DIM] / ol[V_DIM:]

    scores(0, 0, 0)

    def body(u, carry):
        scores(u, 1, 1)
        o0 = finish(0, 0)
        scores(jnp.minimum(u + 1, nsub - 1), 0, 0)
        o1 = finish(1, 1)
        rows = pl.ds(pl.multiple_of(u * sub, sub), sub)
        o_ref[0, rows, :] = jnp.concatenate([o0, o1], axis=0).T.astype(BF16)
        return carry

    lax.fori_loop(0, nsub, body, 0)


def _attention(q, k, vt, bsz, seq):
    sub = min(ATTN_SUB, seq)
    hw = MLA_HEADS * HEAD_PAD
    blk = lambda w: pl.BlockSpec((1, seq, w), lambda b, h: (b, 0, h))
    return pl.pallas_call(
        functools.partial(_attn_kernel, sub=sub),
        grid=(bsz, MLA_HEADS // 2),
        in_specs=[blk(2 * HEAD_PAD), blk(2 * HEAD_PAD),
                  pl.BlockSpec((1, 2 * HEAD_PAD, seq), lambda b, h: (b, h, 0))],
        out_specs=blk(2 * V_DIM),
        out_shape=jax.ShapeDtypeStruct((bsz, seq, MLA_HEADS * V_DIM), BF16),
        scratch_shapes=[pltpu.VMEM((2, seq, sub), F32), pltpu.VMEM((2, 1, sub), F32),
                        pltpu.VMEM((seq, sub), BF16)],
        compiler_params=_params("parallel", "parallel"),
        name="attention",
    )(q.reshape(bsz, seq, hw), k.reshape(bsz, seq, hw), vt)


def _merge_kernel(x_ref, yf_ref, y5_ref, o_ref, gmix_ref, wfn_ref, wglu_ref, ws5_ref, wo_ref,
                  wg_ref, wout_ref, out_ref):
    x = x_ref[...]
    h = _rms(x, gmix_ref[...]).astype(BF16)
    y_a = _dot(yf_ref[...].astype(BF16), wfn_ref[...])
    nchunk = y5_ref.shape[1]
    ys = jnp.concatenate(
        [jnp.concatenate([y5_ref[t, c, 0] for c in range(nchunk)], axis=0) for t in range(S5_TILES)],
        axis=1)
    s = jax.nn.gelu(ys).astype(BF16)
    hg = _dot(s, wglu_ref[...])
    glu = (hg[:, :S5_W] * jax.nn.sigmoid(hg[:, S5_W:])).astype(BF16)
    y_b = _dot(glu, ws5_ref[...])
    y_c = _dot(o_ref[...], wo_ref[...])
    merged = jax.nn.sigmoid(_dot(h, wg_ref[:, :D_MODEL])) * y_a
    merged += jax.nn.sigmoid(_dot(h, wg_ref[:, D_MODEL:2 * D_MODEL])) * y_b
    merged += jax.nn.sigmoid(_dot(h, wg_ref[:, 2 * D_MODEL:])) * y_c
    out_ref[...] = x + _dot(merged.astype(BF16), wout_ref[...])


def _merge(x, yf, y5, o, lw, seq):
    n = x.shape[0]
    tm = min(1024, seq)
    nt = seq // tm
    row = lambda w: pl.BlockSpec((tm, w), lambda i: (i, 0))
    y5_spec = pl.BlockSpec((S5_TILES, tm // S5_CHUNK, 1, S5_CHUNK, LANES),
                           lambda i: (0, i % nt, i // nt, 0, 0))
    return pl.pallas_call(
        _merge_kernel,
        grid=(n // tm,),
        in_specs=[row(D_MODEL), row(FNET_W), y5_spec, row(MLA_HEADS * V_DIM),
                  _const_spec((1, D_MODEL)), _const_spec((FNET_W, D_MODEL)),
                  _const_spec((S5_W, 2 * S5_W)), _const_spec((S5_W, D_MODEL)),
                  _const_spec((MLA_HEADS * V_DIM, D_MODEL)),
                  _const_spec((D_MODEL, N_BRANCH * D_MODEL)), _const_spec((D_MODEL, D_MODEL))],
        out_specs=row(D_MODEL),
        out_shape=jax.ShapeDtypeStruct((n, D_MODEL), F32),
        compiler_params=_params("parallel"),
        name="merge",
    )(x, yf, y5, o, lw["g_mix"], lw["w_fnet"], lw["w_glu"], lw["w_s5"], lw["w_o"], lw["w_gate"],
      lw["w_out"])


def _mlp_kernel(x_ref, g_ref, wup_ref, wdn_ref, gfin_ref, out_ref, *, final):
    x = x_ref[...]
    h = _rms(x, g_ref[...]).astype(BF16)
    a = jnp.square(jnp.maximum(_dot(h, wup_ref[...]), 0.0)).astype(BF16)
    y = x + _dot(a, wdn_ref[...])
    if final:
        y = _rms(y, gfin_ref[...])
    out_ref[...] = y


def _mlp(x, lw, g_final, final):
    n = x.shape[0]
    tm = min(512, n)
    resident = lambda shape: pl.BlockSpec(shape, lambda i: (0, 0), pipeline_mode=pl.Buffered(1))
    return pl.pallas_call(
        functools.partial(_mlp_kernel, final=final),
        grid=(n // tm,),
        in_specs=[pl.BlockSpec((tm, D_MODEL), lambda i: (i, 0)), _const_spec((1, D_MODEL)),
                  resident((D_MODEL, D_FF)), resident((D_FF, D_MODEL)), _const_spec((1, D_MODEL))],
        out_specs=pl.BlockSpec((tm, D_MODEL), lambda i: (i, 0)),
        out_shape=jax.ShapeDtypeStruct((n, D_MODEL), F32),
        compiler_params=_params("parallel"),
        name="mlp",
    )(x, lw["g_mlp"], lw["w_up"], lw["w_down"], g_final)


def _tables(seq):
    half = QK_ROPE // 2
    inv = ROPE_BASE ** (-jnp.arange(half, dtype=F32) / half)
    ang = jnp.arange(seq, dtype=F32)[:, None] * inv[None, :]
    cos, sin = jnp.cos(ang), jnp.sin(ang)
    one = jnp.ones((seq, QK_NOPE), F32)
    z64 = jnp.zeros((seq, QK_NOPE), F32)
    z32 = jnp.zeros((seq, HEAD_PAD - QK_NOPE - QK_ROPE), F32)
    scale = (QK_NOPE + QK_ROPE) ** -0.5 * math.log2(math.e)
    ta = jnp.concatenate([z64, cos, cos, z32], axis=1)
    tb = jnp.concatenate([z64, -sin, sin, z32], axis=1)
    taq = jnp.concatenate([one, cos, cos, z32], axis=1) * scale
    c = jnp.arange(FNET_GROUP_DIM)
    ang64 = (2.0 * math.pi / FNET_GROUP_DIM) * ((c[:, None] * c[None, :]) % FNET_GROUP_DIM).astype(F32)
    eye = jnp.eye(FNET_GROUPS, dtype=F32)
    norm = 1.0 / math.sqrt(seq * FNET_GROUP_DIM)
    cs = jnp.concatenate([jnp.kron(eye, jnp.cos(ang64)), jnp.kron(eye, jnp.sin(ang64))], axis=1) * norm
    r1 = seq // FNET_R2
    i1 = jnp.arange(r1)
    a1 = (2.0 * math.pi / r1) * ((i1[:, None] * i1[None, :]) % r1).astype(F32)
    f1 = jnp.concatenate([jnp.cos(a1), jnp.sin(a1)], axis=0).astype(BF16)
    f2 = jnp.concatenate([jnp.cos(ang64), jnp.sin(ang64)], axis=1).astype(BF16)
    atw = (2.0 * math.pi / seq) * (c[:, None] * i1[None, :]).astype(F32)
    twc = jnp.broadcast_to(jnp.cos(atw)[:, :, None], (FNET_R2, r1, LANES))
    tws = jnp.broadcast_to(jnp.sin(atw)[:, :, None], (FNET_R2, r1, LANES))
    return dict(taq=taq, tbq=tb * scale, tak=ta, tbk=tb, cs=cs.astype(BF16), f1=f1, f2=f2, twc=twc, tws=tws)


def _pad_heads(w, width):
    k = w.shape[0]
    w = w.reshape(k, MLA_HEADS, width)
    return jnp.pad(w, ((0, 0), (0, 0), (0, HEAD_PAD - width))).reshape(k, MLA_HEADS * HEAD_PAD)


def _cmul(ar, ai, br, bi):
    return ar * br - ai * bi, ar * bi + ai * br


def _s5_tables(lam_re, lam_im, log_dt, b_re, b_im, c_re, c_im, d_skip):
    t, p, g, ns = S5_CHUNK, S5_GROUP_DIM, S5_GROUPS, S5_STATE
    nt, tg = S5_TILES, S5_TILE_GROUPS
    dt = jnp.exp(log_dt)[..., None]
    ar, ai = lam_re * dt, lam_im * dt
    mag = jnp.exp(ar)
    lbr, lbi = mag * jnp.cos(ai), mag * jnp.sin(ai)
    den = lam_re * lam_re + lam_im * lam_im
    cfr = ((lbr - 1.0) * lam_re + lbi * lam_im) / den
    cfi = (lbi * lam_re - (lbr - 1.0) * lam_im) / den
    bbr, bbi = _cmul(cfr[..., None], cfi[..., None], b_re, b_im)
    d = jnp.arange(t + 1, dtype=F32)
    pmag = jnp.exp(ar[..., None] * d)
    pwr, pwi = pmag * jnp.cos(ai[..., None] * d), pmag * jnp.sin(ai[..., None] * d)
    cpr, cpi = _cmul(c_re[..., None], c_im[..., None], pwr[:, :, None, :, :t], pwi[:, :, None, :, :t])
    kern = (jnp.einsum('xgpnd,xgnq->xgdpq', cpr, bbr)
            - jnp.einsum('xgpnd,xgnq->xgdpq', cpi, bbi))
    skip = jnp.eye(p, dtype=F32)[None] * d_skip.reshape(g, p)[:, :, None]
    k0 = kern[0][:, :1] + kern[1][:, :1] + skip[:, None]
    kfull = jnp.concatenate([kern[1][:, :0:-1], k0, kern[0][:, 1:]], axis=1)
    eye = jnp.eye(tg, dtype=F32)
    kq = kfull.transpose(0, 1, 3, 2).reshape(nt, tg, 2 * t - 1, p, p)
    bd = jnp.einsum('Ggdqp,gh->Gdgqhp', kq, eye).reshape(nt, 2 * t - 1, LANES, LANES)
    lag = jnp.arange(t)[None, :] - jnp.arange(t)[:, None] + (t - 1)
    m = bd[:, lag].transpose(0, 1, 3, 2, 4).reshape(nt, t * LANES, t * LANES)
    rev = lambda v: v[..., ::-1]
    lay_p = lambda v: v.reshape(2, nt, tg * ns, t).transpose(0, 1, 3, 2)
    psr = lay_p(jnp.stack([rev(pwr[0])[..., 1:], pwr[1][..., :t]]))
    psi = lay_p(jnp.stack([rev(pwi[0])[..., 1:], pwi[1][..., :t]]))
    lay_b = lambda v: jnp.einsum('xGhnq,gh->xGgqhn', v.reshape(2, nt, tg, ns, p), eye).reshape(
        2, nt, LANES, tg * ns)
    bmr, bmi = lay_b(bbr), lay_b(bbi)
    bsr, bsi = _cmul(psr[:, :, :, None, :], psi[:, :, :, None, :], bmr[:, :, None], bmi[:, :, None])
    bst = jnp.stack([bsr, bsi], axis=1).transpose(2, 3, 4, 0, 1, 5).reshape(nt, t * LANES, 2 * S5_DIR)
    lay_c = lambda v: jnp.einsum('xGhpn,gh->xGgnhp', v.reshape(2, nt, tg, p, ns), eye).reshape(
        2, nt, tg * ns, LANES)
    cmr, cmi = lay_c(c_re), lay_c(c_im)
    por = jnp.stack([pwr[0][..., 1:], rev(pwr[1])[..., :t]]).reshape(2, nt, tg * ns, t)
    poi = jnp.stack([pwi[0][..., 1:], rev(pwi[1])[..., :t]]).reshape(2, nt, tg * ns, t)
    csr, csi = _cmul(cmr[:, :, :, None, :], cmi[:, :, :, None, :], por[..., None], poi[..., None])
    cst = jnp.stack([csr, -csi], axis=1).transpose(2, 0, 1, 3, 4, 5).reshape(nt, 2 * S5_DIR, t * LANES)
    a = jnp.stack([pwr[..., t], pwi[..., t]], axis=1)
    a = a.reshape(2, 2, nt, tg, ns).transpose(2, 0, 1, 3, 4).reshape(nt, 1, 2 * S5_DIR)
    a = jnp.broadcast_to(a, (nt, SUBLANES, 2 * S5_DIR))
    return dict(s5_m=m.astype(BF16), s5_bst=bst.astype(BF16), s5_cst=cst.astype(BF16), s5_a=a.astype(F32))


def _layer_weights(i, p):
    w_in = p["w_in"][i]
    k_dim = w_in.shape[0]
    half = QK_ROPE // 2
    kr = w_in[:, OFF_KR:OFF_GATE]
    z64 = jnp.zeros((k_dim, QK_NOPE), F32)
    z32 = jnp.zeros((k_dim, HEAD_PAD - QK_NOPE - QK_ROPE), F32)
    w_a = jnp.concatenate([w_in[:, :OFF_KR], z64, kr, z32, z64, kr[:, half:], kr[:, :half], z32], axis=1)
    wkv = p["w_kvb"][i].reshape(KV_LORA, MLA_HEADS, QK_NOPE + V_DIM)
    row = lambda v: v.reshape(1, -1).astype(F32)
    lw = dict(
        g_mix=row(p["g_mix"][i]), w_a=w_a.astype(BF16), w_gate=w_in[:, OFF_GATE:].astype(BF16),
        g_q=row(p["g_q"][i]), g_kv=row(p["g_kv"][i]),
        wq_a=_pad_heads(p["w_qb"][i], QK_NOPE + QK_ROPE).astype(BF16),
        wk=_pad_heads(wkv[:, :, :QK_NOPE].reshape(KV_LORA, -1), QK_NOPE).astype(BF16),
        wv_t=_pad_heads(wkv[:, :, QK_NOPE:].reshape(KV_LORA, -1), V_DIM).T.astype(BF16),
        w_fnet=p["w_fnet"][i].astype(BF16), w_glu=p["w_glu"][i].astype(BF16),
        w_s5=p["w_s5"][i].astype(BF16), w_o=p["w_o_mla"][i].astype(BF16),
        w_out=p["w_out"][i].astype(BF16), g_mlp=row(p["g_mlp"][i]),
        w_up=p["w_up"][i].astype(BF16), w_down=p["w_down"][i].astype(BF16))
    lw.update(_s5_tables(p["s5_lam_re"][i], p["s5_lam_im"][i], p["s5_log_dt"][i], p["s5_b_re"][i],
                         p["s5_b_im"][i], p["s5_c_re"][i], p["s5_c_im"][i], p["s5_d"][i]))
    return lw


def _trunk(x, layers, tabs, g_final):
    bsz, seq, _ = x.shape
    assert seq % S5_CHUNK == 0
    x = x.reshape(bsz * seq, D_MODEL)
    for i, lw in enumerate(layers):
        ucs, u5, q, k, vt = _inproj(x, lw, tabs, bsz, seq)
        yf = _fnet(ucs.reshape(bsz, -1, 2 * FNET_W), tabs, bsz, seq).reshape(bsz * seq, FNET_W)
        y5 = _s5(u5, lw, seq)
        o = _attention(q, k, vt, bsz, seq).reshape(bsz * seq, MLA_HEADS * V_DIM)
        x = _merge(x, yf, y5, o, lw, seq)
        x = _mlp(x, lw, g_final, final=(i == len(layers) - 1))
    return x.reshape(bsz, seq, D_MODEL)


def kernel(x_prompt, x_sample, g_mix, w_in, w_fnet, s5_lam_re, s5_lam_im, s5_log_dt, s5_b_re,
           s5_b_im, s5_c_re, s5_c_im, s5_d, w_glu, w_s5, g_q, w_qb, g_kv, w_kvb, w_o_mla,
           w_out, g_mlp, w_up, w_down, g_final):
    p = dict(g_mix=g_mix, w_in=w_in, w_fnet=w_fnet, s5_lam_re=s5_lam_re, s5_lam_im=s5_lam_im,
             s5_log_dt=s5_log_dt, s5_b_re=s5_b_re, s5_b_im=s5_b_im, s5_c_re=s5_c_re,
             s5_c_im=s5_c_im, s5_d=s5_d, w_glu=w_glu, w_s5=w_s5, g_q=g_q, w_qb=w_qb, g_kv=g_kv,
             w_kvb=w_kvb, w_o_mla=w_o_mla, w_out=w_out, g_mlp=g_mlp, w_up=w_up, w_down=w_down)
    layers = [_layer_weights(i, p) for i in range(g_mix.shape[0])]
    gfin = g_final.reshape(1, -1).astype(F32)
    outs = []
    for x in (x_prompt, x_sample):
        tabs = _tables(x.shape[1])
        outs.append(_trunk(x, layers, tabs, gfin))
    return tuple(outs)
```

```python
import functools
import math

import jax
import jax.numpy as jnp
from jax import lax
from jax.experimental import pallas as pl
from jax.experimental.pallas import tpu as pltpu

F32 = jnp.float32
BF16 = jnp.bfloat16

D_MODEL = 1024
FNET_GROUP_DIM = 64
FNET_GROUPS = 6
FNET_W = 384
S5_GROUP_DIM = 16
S5_GROUPS = 24
S5_W = 384
S5_STATE = 64
MLA_HEADS = 16
QK_NOPE = 64
QK_ROPE = 32
V_DIM = 64
Q_LORA = 384
KV_LORA = 256
ROPE_BASE = 10000.0
N_BRANCH = 3
D_FF = 4 * D_MODEL
EPS = 1e-6

OFF_FNET = 0
OFF_S5 = OFF_FNET + FNET_W
OFF_Q = OFF_S5 + S5_W
OFF_KV = OFF_Q + Q_LORA
OFF_KR = OFF_KV + KV_LORA
OFF_GATE = OFF_KR + QK_ROPE

LANES = 128
SUBLANES = 8
HEAD_PAD = 128
VMEM_LIMIT = 56 * 1024 * 1024

S5_CHUNK = SUBLANES
S5_TILES = S5_W // LANES
S5_TILE_GROUPS = LANES // S5_GROUP_DIM
S5_HALF = S5_TILE_GROUPS * S5_STATE
S5_DIR = 2 * S5_HALF
S5_BLOCK_ROWS = 512
FNET_R2 = FNET_GROUP_DIM
FNET_PITCH = FNET_R2 + SUBLANES

ZA_CQ = OFF_Q
ZA_CKV = OFF_KV
ZA_KRA = OFF_KR
ZA_KRB = OFF_KR + LANES
ZA_W = ZA_KRB + LANES

ATTN_SUB = 512


def _rms(x, g):
    return x * lax.rsqrt(jnp.mean(x * x, axis=-1, keepdims=True) + EPS) * g


def _dot(a, b):
    return jnp.dot(a, b, preferred_element_type=F32)


def _params(*sem):
    return pltpu.CompilerParams(dimension_semantics=sem, vmem_limit_bytes=VMEM_LIMIT)


def _const_spec(shape):
    nd = len(shape)
    return pl.BlockSpec(shape, lambda *_: (0,) * nd, pipeline_mode=pl.Buffered(1))


def _inproj_kernel(x_ref, gmix_ref, wa_ref, cs_ref, gq_ref, wqa_ref, gkv_ref, wk_ref,
                   wv_ref, taq_ref, tbq_ref, tak_ref, tbk_ref, *rest):
    ucs_ref, u5_ref, q_ref, k_ref, v_ref = rest[-5:]
    h = _rms(x_ref[...], gmix_ref[...]).astype(BF16)
    z = _dot(h, wa_ref[...])
    ucs = _dot(z[:, OFF_FNET:OFF_S5].astype(BF16), cs_ref[...])
    for blk in range(ucs.shape[0] // FNET_R2):
        ucs_ref[blk, :FNET_R2, :] = ucs[blk * FNET_R2:(blk + 1) * FNET_R2]
        ucs_ref[blk, FNET_R2:, :] = jnp.zeros((FNET_PITCH - FNET_R2, 2 * FNET_W), F32)
    for t in range(S5_TILES):
        for c in range(z.shape[0] // S5_CHUNK):
            u5_ref[t, c, 0] = z[c * S5_CHUNK:(c + 1) * S5_CHUNK, OFF_S5 + t * LANES:OFF_S5 + (t + 1) * LANES]
    hq = _rms(z[:, ZA_CQ:ZA_CQ + Q_LORA], gq_ref[...]).astype(BF16)
    hkv = _rms(z[:, ZA_CKV:ZA_CKV + KV_LORA], gkv_ref[...]).astype(BF16)
    kpe = z[:, ZA_KRA:ZA_KRA + LANES] * tak_ref[...] + z[:, ZA_KRB:ZA_KRB + LANES] * tbk_ref[...]
    kpe2 = jnp.concatenate([kpe, kpe], axis=1)
    taq2 = jnp.concatenate([taq_ref[...]] * 2, axis=1)
    tbq2 = jnp.concatenate([tbq_ref[...]] * 2, axis=1)
    first_half = (lax.broadcasted_iota(jnp.int32, taq2.shape, 1) & (HEAD_PAD - 1)) < QK_NOPE + QK_ROPE // 2
    for j in range(MLA_HEADS // 2):
        sl = slice(2 * HEAD_PAD * j, 2 * HEAD_PAD * (j + 1))
        qa = _dot(hq, wqa_ref[:, sl])
        qb = jnp.where(first_half, pltpu.roll(qa, 2 * HEAD_PAD - QK_ROPE // 2, 1), pltpu.roll(qa, QK_ROPE // 2, 1))
        q_ref[:, sl] = (qa * taq2 + qb * tbq2).astype(BF16)
        k_ref[:, sl] = (_dot(hkv, wk_ref[:, sl]) + kpe2).astype(BF16)
    vt = lax.dot_general(wv_ref[...], hkv, (((1,), (1,)), ((), ())), preferred_element_type=F32)
    ones_row = (lax.broadcasted_iota(jnp.int32, vt.shape, 0) & V_DIM) != 0
    v_ref[0] = jnp.where(ones_row, 1.0, vt).astype(BF16)


def _inproj(x, lw, tabs, bsz, seq):
    n = x.shape[0]
    tm = min(512, seq)
    nt = seq // tm
    nc = seq // S5_CHUNK
    row = lambda w: pl.BlockSpec((tm, w), lambda i: (i, 0))
    tab = pl.BlockSpec((tm, LANES), lambda i: (i % nt, 0))
    hw = MLA_HEADS * HEAD_PAD
    bs = _s5_batch(bsz)
    u5_shape = (S5_TILES, nc, bs, S5_CHUNK, LANES)
    u5_spec = pl.BlockSpec((S5_TILES, tm // S5_CHUNK, 1, S5_CHUNK, LANES),
                           lambda i: (0, i % nt, i // nt, 0, 0))
    in_specs = [row(D_MODEL), _const_spec((1, D_MODEL)), _const_spec((D_MODEL, ZA_W)),
                _const_spec((FNET_W, 2 * FNET_W)), _const_spec((1, Q_LORA)),
                _const_spec((Q_LORA, hw)), _const_spec((1, KV_LORA)),
                _const_spec((KV_LORA, hw)), _const_spec((hw, KV_LORA)),
                tab, tab, tab, tab]
    args = [x, lw["g_mix"], lw["w_a"], tabs["cs"], lw["g_q"], lw["wq_a"], lw["g_kv"],
            lw["wk"], lw["wv_t"], tabs["taq"], tabs["tbq"], tabs["tak"], tabs["tbk"]]
    aliases = {}
    if bsz < bs:
        aliases = {len(args): 1}
        in_specs.append(pl.BlockSpec(memory_space=pl.ANY))
        args.append(jnp.zeros(u5_shape, F32))
    return pl.pallas_call(
        _inproj_kernel,
        grid=(n // tm,),
        in_specs=in_specs,
        out_specs=[pl.BlockSpec((tm // FNET_R2, FNET_PITCH, 2 * FNET_W), lambda i: (i, 0, 0)), u5_spec, row(hw), row(hw),
                   pl.BlockSpec((1, hw, tm), lambda i: (i // nt, 0, i % nt))],
        out_shape=[jax.ShapeDtypeStruct((n // FNET_R2, FNET_PITCH, 2 * FNET_W), F32), jax.ShapeDtypeStruct(u5_shape, F32),
                   jax.ShapeDtypeStruct((n, hw), BF16), jax.ShapeDtypeStruct((n, hw), BF16),
                   jax.ShapeDtypeStruct((bsz, hw, seq), BF16)],
        input_output_aliases=aliases,
        compiler_params=_params("parallel"),
        name="inproj",
    )(*args)


def _fnet_kernel(uc_ref, us_ref, f1_ref, f2_ref, twc_ref, tws_ref, out_ref, ar_ref, ai_ref, *, r1, p2):
    f1 = f1_ref[...]
    for n2 in range(FNET_R2):
        rows = pl.ds(n2, r1, stride=FNET_PITCH)
        u = jnp.concatenate([uc_ref[0, rows, :], us_ref[0, rows, :]], axis=1).astype(BF16)
        pr = _dot(f1, u)
        ar = pr[:r1, :LANES] - pr[r1:, LANES:]
        ai = -(pr[:r1, LANES:] + pr[r1:, :LANES])
        c, s = twc_ref[n2], tws_ref[n2]
        ar_ref[n2 * p2:n2 * p2 + r1, :] = ar * c + ai * s
        ai_ref[n2 * p2:n2 * p2 + r1, :] = ai * c - ar * s
    f2 = f2_ref[...]
    for k1 in range(r1):
        rows = pl.ds(k1, FNET_R2, stride=p2)
        gk = jnp.concatenate([ar_ref[rows, :], ai_ref[rows, :]], axis=0).astype(BF16)
        out_ref[0, rows, :] = _dot(f2, gk)
    for k2 in range(FNET_R2):
        out_ref[0, k2 * p2 + r1:(k2 + 1) * p2, :] = jnp.zeros((p2 - r1, LANES), F32)


def _fnet(ucs, tabs, bsz, seq):
    r1 = seq // FNET_R2
    p2 = r1 + SUBLANES
    nt = FNET_W // LANES
    padded = lambda rows, off: pl.BlockSpec((1, rows, LANES), lambda b, c: (b, 0, c + off))
    return pl.pallas_call(
        functools.partial(_fnet_kernel, r1=r1, p2=p2),
        grid=(bsz, nt),
        in_specs=[padded(r1 * FNET_PITCH, 0), padded(r1 * FNET_PITCH, nt), _const_spec((2 * r1, r1)),
                  _const_spec((FNET_R2, 2 * FNET_R2)),
                  _const_spec((FNET_R2, r1, LANES)), _const_spec((FNET_R2, r1, LANES))],
        out_specs=padded(FNET_R2 * p2, 0),
        out_shape=jax.ShapeDtypeStruct((bsz, FNET_R2 * p2, FNET_W), F32),
        scratch_shapes=[pltpu.VMEM((FNET_R2 * p2, LANES), F32)] * 2,
        compiler_params=_params("parallel", "parallel"),
        name="fnet",
    )(ucs, ucs, tabs["f1"], tabs["f2"], tabs["twc"], tabs["tws"])


def _s5_rows(u_ref, rows):
    return jnp.concatenate([u_ref[0, pl.ds(s, rows, stride=S5_CHUNK), :] for s in range(S5_CHUNK)],
                           axis=1).astype(BF16)


def _s5_in_kernel(u_ref, bst_ref, loc_ref, *, rows):
    loc_ref[0] = _dot(_s5_rows(u_ref, rows), bst_ref[0])


def _s5_batch(bsz):
    assert bsz <= SUBLANES
    return SUBLANES // 2 if bsz <= SUBLANES // 2 else SUBLANES


def _s5_scan_kernel(lf_ref, lb_ref, a_ref, spf_ref, spb_ref, carry_ref, *, tiles, bs):
    @pl.when(pl.program_id(1) == 0)
    def _():
        carry_ref[...] = jnp.zeros_like(carry_ref)

    afr, afi = a_ref[0, :, :S5_HALF], a_ref[0, :, S5_HALF:S5_DIR]
    abr, abi = a_ref[0, :, S5_DIR:S5_DIR + S5_HALF], a_ref[0, :, S5_DIR + S5_HALF:]

    low = lax.broadcasted_iota(jnp.int32, (SUBLANES, S5_HALF), 0) < SUBLANES // 2
    swap = lambda v: pltpu.roll(v, SUBLANES // 2, 0)

    def advance(ar, ai, xr, xi, lr, li):
        return ar * xr - ai * xi + lr, ar * xi + ai * xr + li

    def step(i, carry):
        fr, fi, br, bi = carry
        rf = pl.ds(pl.multiple_of(i * SUBLANES, SUBLANES), SUBLANES)
        rb = pl.ds(pl.multiple_of((tiles - 1 - i) * SUBLANES, SUBLANES), SUBLANES)
        lfr, lfi = lf_ref[0, rf, :S5_HALF], lf_ref[0, rf, S5_HALF:]
        lbr, lbi = lb_ref[0, rb, :S5_HALF], lb_ref[0, rb, S5_HALF:]
        nfr, nfi = advance(afr, afi, fr, fi, lfr, lfi)
        nbr, nbi = advance(abr, abi, br, bi, lbr, lbi)
        if bs == SUBLANES:
            spf_ref[0, rf, :S5_HALF], spf_ref[0, rf, S5_HALF:] = fr, fi
            spb_ref[0, rb, :S5_HALF], spb_ref[0, rb, S5_HALF:] = br, bi
            return nfr, nfi, nbr, nbi
        mfr, mfi = swap(nfr), swap(nfi)
        spf_ref[0, rf, :S5_HALF] = jnp.where(low, fr, mfr)
        spf_ref[0, rf, S5_HALF:] = jnp.where(low, fi, mfi)
        nfr, nfi = advance(afr, afi, mfr, mfi, lfr, lfi)
        mbr, mbi = swap(nbr), swap(nbi)
        spb_ref[0, rb, :S5_HALF] = jnp.where(low, mbr, br)
        spb_ref[0, rb, S5_HALF:] = jnp.where(low, mbi, bi)
        nbr, nbi = advance(abr, abi, mbr, mbi, lbr, lbi)
        return (jnp.where(low, swap(nfr), nfr), jnp.where(low, swap(nfi), nfi),
                jnp.where(low, nbr, swap(nbr)), jnp.where(low, nbi, swap(nbi)))

    carry = lax.fori_loop(0, tiles, step, tuple(carry_ref[k] for k in range(4)), unroll=2)
    for k in range(4):
        carry_ref[k] = carry[k]


def _s5_out_kernel(u_ref, m_ref, spf_ref, spb_ref, cst_ref, y_ref, *, rows):
    y = (_dot(_s5_rows(u_ref, rows), m_ref[0]) + _dot(spf_ref[0].astype(BF16), cst_ref[0, :S5_DIR, :])
         + _dot(spb_ref[0].astype(BF16), cst_ref[0, S5_DIR:, :]))
    for s in range(S5_CHUNK):
        y_ref[0, pl.ds(s, rows, stride=S5_CHUNK), :] = y[:, s * LANES:(s + 1) * LANES]


def _s5(u5, lw, seq):
    nc, bs = u5.shape[1], u5.shape[2]
    rows = min(S5_BLOCK_ROWS, nc * bs)
    nblk = nc * bs // rows
    kw = S5_CHUNK * LANES
    u2 = u5.reshape(S5_TILES, nc * bs * S5_CHUNK, LANES)
    u_spec = pl.BlockSpec((1, rows * S5_CHUNK, LANES), lambda t, j: (t, j, 0))
    wspec = lambda a, b: pl.BlockSpec((1, a, b), lambda t, j: (t, 0, 0))
    loc = pl.pallas_call(
        functools.partial(_s5_in_kernel, rows=rows),
        grid=(S5_TILES, nblk),
        in_specs=[u_spec, wspec(kw, 2 * S5_DIR)],
        out_specs=pl.BlockSpec((1, rows, 2 * S5_DIR), lambda t, j: (t, j, 0)),
        out_shape=jax.ShapeDtypeStruct((S5_TILES, nc * bs, 2 * S5_DIR), F32),
        compiler_params=_params("parallel", "parallel"),
        name="s5_in",
    )(u2, lw["s5_bst"])
    fwd = lambda lane: pl.BlockSpec((1, rows, S5_DIR), lambda t, j: (t, j, lane))
    bwd = lambda lane: pl.BlockSpec((1, rows, S5_DIR), lambda t, j: (t, nblk - 1 - j, lane))
    sp_shape = jax.ShapeDtypeStruct((S5_TILES, nc * bs, S5_DIR), F32)
    spf, spb = pl.pallas_call(
        functools.partial(_s5_scan_kernel, tiles=rows // SUBLANES, bs=bs),
        grid=(S5_TILES, nblk),
        in_specs=[fwd(0), bwd(1), wspec(SUBLANES, 2 * S5_DIR)],
        out_specs=[fwd(0), bwd(0)],
        out_shape=[sp_shape, sp_shape],
        scratch_shapes=[pltpu.VMEM((4, SUBLANES, S5_HALF), F32)],
        compiler_params=_params("parallel", "arbitrary"),
        name="s5_scan",
    )(loc, loc, lw["s5_a"])
    y2 = pl.pallas_call(
        functools.partial(_s5_out_kernel, rows=rows),
        grid=(S5_TILES, nblk),
        in_specs=[u_spec, wspec(kw, kw), fwd(0), fwd(0), wspec(2 * S5_DIR, kw)],
        out_specs=u_spec,
        out_shape=jax.ShapeDtypeStruct(u2.shape, F32),
        compiler_params=_params("parallel", "parallel"),
        name="s5_out",
    )(u2, lw["s5_m"], spf, spb, lw["s5_cst"])
    return y2.reshape(u5.shape)


def _attn_kernel(q_ref, k_ref, vt_ref, o_ref, s_ref, m_ref, p_ref, *, sub):
    nsub = q_ref.shape[1] // sub

    def scores(u, j, slot):
        sl = slice(HEAD_PAD * j, HEAD_PAD * (j + 1))
        q = q_ref[0, pl.ds(pl.multiple_of(u * sub, sub), sub), sl]
        st = lax.dot_general(k_ref[0, :, sl], q, (((1,), (1,)), ((), ())),
                             preferred_element_type=F32)
        s_ref[slot] = st
        m_ref[slot] = jnp.max(st, axis=0, keepdims=True)

    def finish(j, slot):
        p_ref[...] = jnp.exp2(s_ref[slot] - m_ref[slot]).astype(BF16)
        ol = _dot(vt_ref[0, HEAD_PAD * j:HEAD_PAD * (j + 1), :], p_ref[...])
        return ol[:V_DIM] / ol[V_DIM:]

    scores(0, 0, 0)

    def body(u, carry):
        scores(u, 1, 1)
        o0 = finish(0, 0)
        scores(jnp.minimum(u + 1, nsub - 1), 0, 0)
        o1 = finish(1, 1)
        rows = pl.ds(pl.multiple_of(u * sub, sub), sub)
        o_ref[0, rows, :] = jnp.concatenate([o0, o1], axis=0).T.astype(BF16)
        return carry

    lax.fori_loop(0, nsub, body, 0)


def _attention(q, k, vt, bsz, seq):
    sub = min(ATTN_SUB, seq)
    hw = MLA_HEADS * HEAD_PAD
    blk = lambda w: pl.BlockSpec((1, seq, w), lambda b, h: (b, 0, h))
    return pl.pallas_call(
        functools.partial(_attn_kernel, sub=sub),
        grid=(bsz, MLA_HEADS // 2),
        in_specs=[blk(2 * HEAD_PAD), blk(2 * HEAD_PAD),
                  pl.BlockSpec((1, 2 * HEAD_PAD, seq), lambda b, h: (b, h, 0))],
        out_specs=blk(2 * V_DIM),
        out_shape=jax.ShapeDtypeStruct((bsz, seq, MLA_HEADS * V_DIM), BF16),
        scratch_shapes=[pltpu.VMEM((2, seq, sub), F32), pltpu.VMEM((2, 1, sub), F32),
                        pltpu.VMEM((seq, sub), BF16)],
        compiler_params=_params("parallel", "parallel"),
        name="attention",
    )(q.reshape(bsz, seq, hw), k.reshape(bsz, seq, hw), vt)


def _merge_kernel(x_ref, yf_ref, y5_ref, o_ref, gmix_ref, wfn_ref, wglu_ref, ws5_ref, wo_ref,
                  wg_ref, wout_ref, out_ref):
    x = x_ref[...]
    h = _rms(x, gmix_ref[...]).astype(BF16)
    r1 = yf_ref.shape[1] - SUBLANES
    yf = jnp.concatenate([yf_ref[b, :r1, :] for b in range(yf_ref.shape[0])], axis=0)
    y_a = _dot(yf.astype(BF16), wfn_ref[...])
    nchunk = y5_ref.shape[1]
    ys = jnp.concatenate(
        [jnp.concatenate([y5_ref[t, c, 0] for c in range(nchunk)], axis=0) for t in range(S5_TILES)],
        axis=1)
    s = jax.nn.gelu(ys).astype(BF16)
    hg = _dot(s, wglu_ref[...])
    glu = (hg[:, :S5_W] * jax.nn.sigmoid(hg[:, S5_W:])).astype(BF16)
    y_b = _dot(glu, ws5_ref[...])
    y_c = _dot(o_ref[...], wo_ref[...])
    merged = jax.nn.sigmoid(_dot(h, wg_ref[:, :D_MODEL])) * y_a
    merged += jax.nn.sigmoid(_dot(h, wg_ref[:, D_MODEL:2 * D_MODEL])) * y_b
    merged += jax.nn.sigmoid(_dot(h, wg_ref[:, 2 * D_MODEL:])) * y_c
    out_ref[...] = x + _dot(merged.astype(BF16), wout_ref[...])


def _merge(x, yf, y5, o, lw, seq):
    n = x.shape[0]
    tm = min(1024, seq)
    nt = seq // tm
    row = lambda w: pl.BlockSpec((tm, w), lambda i: (i, 0))
    y5_spec = pl.BlockSpec((S5_TILES, tm // S5_CHUNK, 1, S5_CHUNK, LANES),
                           lambda i: (0, i % nt, i // nt, 0, 0))
    r1 = seq // FNET_R2
    yf = yf.reshape(-1, yf.shape[1] // FNET_R2, FNET_W)
    yf_spec = pl.BlockSpec((tm // r1, yf.shape[1], FNET_W), lambda i: (i, 0, 0))
    return pl.pallas_call(
        _merge_kernel,
        grid=(n // tm,),
        in_specs=[row(D_MODEL), yf_spec, y5_spec, row(MLA_HEADS * V_DIM),
                  _const_spec((1, D_MODEL)), _const_spec((FNET_W, D_MODEL)),
                  _const_spec((S5_W, 2 * S5_W)), _const_spec((S5_W, D_MODEL)),
                  _const_spec((MLA_HEADS * V_DIM, D_MODEL)),
                  _const_spec((D_MODEL, N_BRANCH * D_MODEL)), _const_spec((D_MODEL, D_MODEL))],
        out_specs=row(D_MODEL),
        out_shape=jax.ShapeDtypeStruct((n, D_MODEL), F32),
        compiler_params=_params("parallel"),
        name="merge",
    )(x, yf, y5, o, lw["g_mix"], lw["w_fnet"], lw["w_glu"], lw["w_s5"], lw["w_o"], lw["w_gate"],
      lw["w_out"])


def _mlp_kernel(x_ref, g_ref, wup_ref, wdn_ref, gfin_ref, out_ref, *, final):
    x = x_ref[...]
    h = _rms(x, g_ref[...]).astype(BF16)
    a = jnp.square(jnp.maximum(_dot(h, wup_ref[...]), 0.0)).astype(BF16)
    y = x + _dot(a, wdn_ref[...])
    if final:
        y = _rms(y, gfin_ref[...])
    out_ref[...] = y


def _mlp(x, lw, g_final, final):
    n = x.shape[0]
    tm = min(512, n)
    resident = lambda shape: pl.BlockSpec(shape, lambda i: (0, 0), pipeline_mode=pl.Buffered(1))
    return pl.pallas_call(
        functools.partial(_mlp_kernel, final=final),
        grid=(n // tm,),
        in_specs=[pl.BlockSpec((tm, D_MODEL), lambda i: (i, 0)), _const_spec((1, D_MODEL)),
                  resident((D_MODEL, D_FF)), resident((D_FF, D_MODEL)), _const_spec((1, D_MODEL))],
        out_specs=pl.BlockSpec((tm, D_MODEL), lambda i: (i, 0)),
        out_shape=jax.ShapeDtypeStruct((n, D_MODEL), F32),
        compiler_params=_params("parallel"),
        name="mlp",
    )(x, lw["g_mlp"], lw["w_up"], lw["w_down"], g_final)


def _tables(seq):
    half = QK_ROPE // 2
    inv = ROPE_BASE ** (-jnp.arange(half, dtype=F32) / half)
    ang = jnp.arange(seq, dtype=F32)[:, None] * inv[None, :]
    cos, sin = jnp.cos(ang), jnp.sin(ang)
    one = jnp.ones((seq, QK_NOPE), F32)
    z64 = jnp.zeros((seq, QK_NOPE), F32)
    z32 = jnp.zeros((seq, HEAD_PAD - QK_NOPE - QK_ROPE), F32)
    scale = (QK_NOPE + QK_ROPE) ** -0.5 * math.log2(math.e)
    ta = jnp.concatenate([z64, cos, cos, z32], axis=1)
    tb = jnp.concatenate([z64, -sin, sin, z32], axis=1)
    taq = jnp.concatenate([one, cos, cos, z32], axis=1) * scale
    c = jnp.arange(FNET_GROUP_DIM)
    ang64 = (2.0 * math.pi / FNET_GROUP_DIM) * ((c[:, None] * c[None, :]) % FNET_GROUP_DIM).astype(F32)
    eye = jnp.eye(FNET_GROUPS, dtype=F32)
    norm = 1.0 / math.sqrt(seq * FNET_GROUP_DIM)
    cs = jnp.concatenate([jnp.kron(eye, jnp.cos(ang64)), jnp.kron(eye, jnp.sin(ang64))], axis=1) * norm
    r1 = seq // FNET_R2
    i1 = jnp.arange(r1)
    a1 = (2.0 * math.pi / r1) * ((i1[:, None] * i1[None, :]) % r1).astype(F32)
    f1 = jnp.concatenate([jnp.cos(a1), jnp.sin(a1)], axis=0).astype(BF16)
    f2 = jnp.concatenate([jnp.cos(ang64), jnp.sin(ang64)], axis=1).astype(BF16)
    atw = (2.0 * math.pi / seq) * (c[:, None] * i1[None, :]).astype(F32)
    twc = jnp.broadcast_to(jnp.cos(atw)[:, :, None], (FNET_R2, r1, LANES))
    tws = jnp.broadcast_to(jnp.sin(atw)[:, :, None], (FNET_R2, r1, LANES))
    return dict(taq=taq, tbq=tb * scale, tak=ta, tbk=tb, cs=cs.astype(BF16), f1=f1, f2=f2, twc=twc, tws=tws)


def _pad_heads(w, width):
    k = w.shape[0]
    w = w.reshape(k, MLA_HEADS, width)
    return jnp.pad(w, ((0, 0), (0, 0), (0, HEAD_PAD - width))).reshape(k, MLA_HEADS * HEAD_PAD)


def _cmul(ar, ai, br, bi):
    return ar * br - ai * bi, ar * bi + ai * br


def _s5_tables(lam_re, lam_im, log_dt, b_re, b_im, c_re, c_im, d_skip):
    t, p, g, ns = S5_CHUNK, S5_GROUP_DIM, S5_GROUPS, S5_STATE
    nt, tg = S5_TILES, S5_TILE_GROUPS
    dt = jnp.exp(log_dt)[..., None]
    ar, ai = lam_re * dt, lam_im * dt
    mag = jnp.exp(ar)
    lbr, lbi = mag * jnp.cos(ai), mag * jnp.sin(ai)
    den = lam_re * lam_re + lam_im * lam_im
    cfr = ((lbr - 1.0) * lam_re + lbi * lam_im) / den
    cfi = (lbi * lam_re - (lbr - 1.0) * lam_im) / den
    bbr, bbi = _cmul(cfr[..., None], cfi[..., None], b_re, b_im)
    d = jnp.arange(t + 1, dtype=F32)
    pmag = jnp.exp(ar[..., None] * d)
    pwr, pwi = pmag * jnp.cos(ai[..., None] * d), pmag * jnp.sin(ai[..., None] * d)
    cpr, cpi = _cmul(c_re[..., None], c_im[..., None], pwr[:, :, None, :, :t], pwi[:, :, None, :, :t])
    kern = (jnp.einsum('xgpnd,xgnq->xgdpq', cpr, bbr)
            - jnp.einsum('xgpnd,xgnq->xgdpq', cpi, bbi))
    skip = jnp.eye(p, dtype=F32)[None] * d_skip.reshape(g, p)[:, :, None]
    k0 = kern[0][:, :1] + kern[1][:, :1] + skip[:, None]
    kfull = jnp.concatenate([kern[1][:, :0:-1], k0, kern[0][:, 1:]], axis=1)
    eye = jnp.eye(tg, dtype=F32)
    kq = kfull.transpose(0, 1, 3, 2).reshape(nt, tg, 2 * t - 1, p, p)
    bd = jnp.einsum('Ggdqp,gh->Gdgqhp', kq, eye).reshape(nt, 2 * t - 1, LANES, LANES)
    lag = jnp.arange(t)[None, :] - jnp.arange(t)[:, None] + (t - 1)
    m = bd[:, lag].transpose(0, 1, 3, 2, 4).reshape(nt, t * LANES, t * LANES)
    rev = lambda v: v[..., ::-1]
    lay_p = lambda v: v.reshape(2, nt, tg * ns, t).transpose(0, 1, 3, 2)
    psr = lay_p(jnp.stack([rev(pwr[0])[..., 1:], pwr[1][..., :t]]))
    psi = lay_p(jnp.stack([rev(pwi[0])[..., 1:], pwi[1][..., :t]]))
    lay_b = lambda v: jnp.einsum('xGhnq,gh->xGgqhn', v.reshape(2, nt, tg, ns, p), eye).reshape(
        2, nt, LANES, tg * ns)
    bmr, bmi = lay_b(bbr), lay_b(bbi)
    bsr, bsi = _cmul(psr[:, :, :, None, :], psi[:, :, :, None, :], bmr[:, :, None], bmi[:, :, None])
    bst = jnp.stack([bsr, bsi], axis=1).transpose(2, 3, 4, 0, 1, 5).reshape(nt, t * LANES, 2 * S5_DIR)
    lay_c = lambda v: jnp.einsum('xGhpn,gh->xGgnhp', v.reshape(2, nt, tg, p, ns), eye).reshape(
        2, nt, tg * ns, LANES)
    cmr, cmi = lay_c(c_re), lay_c(c_im)
    por = jnp.stack([pwr[0][..., 1:], rev(pwr[1])[..., :t]]).reshape(2, nt, tg * ns, t)
    poi = jnp.stack([pwi[0][..., 1:], rev(pwi[1])[..., :t]]).reshape(2, nt, tg * ns, t)
    csr, csi = _cmul(cmr[:, :, :, None, :], cmi[:, :, :, None, :], por[..., None], poi[..., None])
    cst = jnp.stack([csr, -csi], axis=1).transpose(2, 0, 1, 3, 4, 5).reshape(nt, 2 * S5_DIR, t * LANES)
    a = jnp.stack([pwr[..., t], pwi[..., t]], axis=1)
    a = a.reshape(2, 2, nt, tg, ns).transpose(2, 0, 1, 3, 4).reshape(nt, 1, 2 * S5_DIR)
    a = jnp.broadcast_to(a, (nt, SUBLANES, 2 * S5_DIR))
    return dict(s5_m=m.astype(BF16), s5_bst=bst.astype(BF16), s5_cst=cst.astype(BF16), s5_a=a.astype(F32))


def _layer_weights(i, p):
    w_in = p["w_in"][i]
    k_dim = w_in.shape[0]
    half = QK_ROPE // 2
    kr = w_in[:, OFF_KR:OFF_GATE]
    z64 = jnp.zeros((k_dim, QK_NOPE), F32)
    z32 = jnp.zeros((k_dim, HEAD_PAD - QK_NOPE - QK_ROPE), F32)
    w_a = jnp.concatenate([w_in[:, :OFF_KR], z64, kr, z32, z64, kr[:, half:], kr[:, :half], z32], axis=1)
    wkv = p["w_kvb"][i].reshape(KV_LORA, MLA_HEADS, QK_NOPE + V_DIM)
    row = lambda v: v.reshape(1, -1).astype(F32)
    lw = dict(
        g_mix=row(p["g_mix"][i]), w_a=w_a.astype(BF16), w_gate=w_in[:, OFF_GATE:].astype(BF16),
        g_q=row(p["g_q"][i]), g_kv=row(p["g_kv"][i]),
        wq_a=_pad_heads(p["w_qb"][i], QK_NOPE + QK_ROPE).astype(BF16),
        wk=_pad_heads(wkv[:, :, :QK_NOPE].reshape(KV_LORA, -1), QK_NOPE).astype(BF16),
        wv_t=_pad_heads(wkv[:, :, QK_NOPE:].reshape(KV_LORA, -1), V_DIM).T.astype(BF16),
        w_fnet=p["w_fnet"][i].astype(BF16), w_glu=p["w_glu"][i].astype(BF16),
        w_s5=p["w_s5"][i].astype(BF16), w_o=p["w_o_mla"][i].astype(BF16),
        w_out=p["w_out"][i].astype(BF16), g_mlp=row(p["g_mlp"][i]),
        w_up=p["w_up"][i].astype(BF16), w_down=p["w_down"][i].astype(BF16))
    lw.update(_s5_tables(p["s5_lam_re"][i], p["s5_lam_im"][i], p["s5_log_dt"][i], p["s5_b_re"][i],
                         p["s5_b_im"][i], p["s5_c_re"][i], p["s5_c_im"][i], p["s5_d"][i]))
    return lw


def _trunk(x, layers, tabs, g_final):
    bsz, seq, _ = x.shape
    assert seq % S5_CHUNK == 0
    x = x.reshape(bsz * seq, D_MODEL)
    for i, lw in enumerate(layers):
        ucs, u5, q, k, vt = _inproj(x, lw, tabs, bsz, seq)
        yf = _fnet(ucs.reshape(bsz, -1, 2 * FNET_W), tabs, bsz, seq)
        y5 = _s5(u5, lw, seq)
        o = _attention(q, k, vt, bsz, seq).reshape(bsz * seq, MLA_HEADS * V_DIM)
        x = _merge(x, yf, y5, o, lw, seq)
        x = _mlp(x, lw, g_final, final=(i == len(layers) - 1))
    return x.reshape(bsz, seq, D_MODEL)


def kernel(x_prompt, x_sample, g_mix, w_in, w_fnet, s5_lam_re, s5_lam_im, s5_log_dt, s5_b_re,
           s5_b_im, s5_c_re, s5_c_im, s5_d, w_glu, w_s5, g_q, w_qb, g_kv, w_kvb, w_o_mla,
           w_out, g_mlp, w_up, w_down, g_final):
    p = dict(g_mix=g_mix, w_in=w_in, w_fnet=w_fnet, s5_lam_re=s5_lam_re, s5_lam_im=s5_lam_im,
             s5_log_dt=s5_log_dt, s5_b_re=s5_b_re, s5_b_im=s5_b_im, s5_c_re=s5_c_re,
             s5_c_im=s5_c_im, s5_d=s5_d, w_glu=w_glu, w_s5=w_s5, g_q=g_q, w_qb=w_qb, g_kv=g_kv,
             w_kvb=w_kvb, w_o_mla=w_o_mla, w_out=w_out, g_mlp=g_mlp, w_up=w_up, w_down=w_down)
    layers = [_layer_weights(i, p) for i in range(g_mix.shape[0])]
    gfin = g_final.reshape(1, -1).astype(F32)
    outs = []
    for x in (x_prompt, x_sample):
        tabs = _tables(x.shape[1])
        outs.append(_trunk(x, layers, tabs, gfin))
    return tuple(outs)
```

```python
import functools
import math

import jax
import jax.numpy as jnp
from jax import lax
from jax.experimental import pallas as pl
from jax.experimental.pallas import tpu as pltpu

F32 = jnp.float32
BF16 = jnp.bfloat16

D_MODEL = 1024
FNET_GROUP_DIM = 64
FNET_GROUPS = 6
FNET_W = 384
S5_GROUP_DIM = 16
S5_GROUPS = 24
S5_W = 384
S5_STATE = 64
MLA_HEADS = 16
QK_NOPE = 64
QK_ROPE = 32
V_DIM = 64
Q_LORA = 384
KV_LORA = 256
ROPE_BASE = 10000.0
N_BRANCH = 3
D_FF = 4 * D_MODEL
EPS = 1e-6

OFF_FNET = 0
OFF_S5 = OFF_FNET + FNET_W
OFF_Q = OFF_S5 + S5_W
OFF_KV = OFF_Q + Q_LORA
OFF_KR = OFF_KV + KV_LORA
OFF_GATE = OFF_KR + QK_ROPE

LANES = 128
SUBLANES = 8
HEAD_PAD = 128
VMEM_LIMIT = 56 * 1024 * 1024

S5_CHUNK = SUBLANES
S5_TILES = S5_W // LANES
S5_TILE_GROUPS = LANES // S5_GROUP_DIM
S5_HALF = S5_TILE_GROUPS * S5_STATE
S5_DIR = 2 * S5_HALF
S5_BLOCK_ROWS = 512
FNET_R2 = FNET_GROUP_DIM
FNET_PITCH = FNET_R2 + SUBLANES

ZA_CQ = OFF_Q
ZA_CKV = OFF_KV
ZA_KRA = OFF_KR
ZA_KRB = OFF_KR + LANES
ZA_W = ZA_KRB + LANES

ATTN_SUB = 512


def _rms(x, g):
    return x * lax.rsqrt(jnp.mean(x * x, axis=-1, keepdims=True) + EPS) * g


def _dot(a, b):
    return jnp.dot(a, b, preferred_element_type=F32)


def _params(*sem):
    return pltpu.CompilerParams(dimension_semantics=sem, vmem_limit_bytes=VMEM_LIMIT)


def _const_spec(shape):
    nd = len(shape)
    return pl.BlockSpec(shape, lambda *_: (0,) * nd, pipeline_mode=pl.Buffered(1))


def _inproj_kernel(x_ref, gmix_ref, wa_ref, cs_ref, gq_ref, wqa_ref, gkv_ref, wk_ref,
                   wv_ref, taq_ref, tbq_ref, tak_ref, tbk_ref, *rest):
    ucs_ref, u5_ref, q_ref, k_ref, v_ref = rest[-5:]
    h = _rms(x_ref[...], gmix_ref[...]).astype(BF16)
    z = _dot(h, wa_ref[...])
    ucs = _dot(z[:, OFF_FNET:OFF_S5].astype(BF16), cs_ref[...])
    for blk in range(ucs.shape[0] // FNET_R2):
        ucs_ref[blk, :FNET_R2, :] = ucs[blk * FNET_R2:(blk + 1) * FNET_R2]
        ucs_ref[blk, FNET_R2:, :] = jnp.zeros((FNET_PITCH - FNET_R2, 2 * FNET_W), F32)
    for t in range(S5_TILES):
        for c in range(z.shape[0] // S5_CHUNK):
            u5_ref[t, c, 0] = z[c * S5_CHUNK:(c + 1) * S5_CHUNK, OFF_S5 + t * LANES:OFF_S5 + (t + 1) * LANES]
    hq = _rms(z[:, ZA_CQ:ZA_CQ + Q_LORA], gq_ref[...]).astype(BF16)
    hkv = _rms(z[:, ZA_CKV:ZA_CKV + KV_LORA], gkv_ref[...]).astype(BF16)
    kpe = z[:, ZA_KRA:ZA_KRA + LANES] * tak_ref[...] + z[:, ZA_KRB:ZA_KRB + LANES] * tbk_ref[...]
    kpe2 = jnp.concatenate([kpe, kpe], axis=1)
    taq2 = jnp.concatenate([taq_ref[...]] * 2, axis=1)
    tbq2 = jnp.concatenate([tbq_ref[...]] * 2, axis=1)
    first_half = (lax.broadcasted_iota(jnp.int32, taq2.shape, 1) & (HEAD_PAD - 1)) < QK_NOPE + QK_ROPE // 2
    for j in range(MLA_HEADS // 2):
        sl = slice(2 * HEAD_PAD * j, 2 * HEAD_PAD * (j + 1))
        qa = _dot(hq, wqa_ref[:, sl])
        qb = jnp.where(first_half, pltpu.roll(qa, 2 * HEAD_PAD - QK_ROPE // 2, 1), pltpu.roll(qa, QK_ROPE // 2, 1))
        q_ref[:, sl] = (qa * taq2 + qb * tbq2).astype(BF16)
        k_ref[:, sl] = (_dot(hkv, wk_ref[:, sl]) + kpe2).astype(BF16)
    vt = lax.dot_general(wv_ref[...], hkv, (((1,), (1,)), ((), ())), preferred_element_type=F32)
    ones_row = (lax.broadcasted_iota(jnp.int32, vt.shape, 0) & V_DIM) != 0
    v_ref[0] = jnp.where(ones_row, 1.0, vt).astype(BF16)


def _inproj(x, lw, tabs, bsz, seq):
    n = x.shape[0]
    tm = min(512, seq)
    nt = seq // tm
    nc = seq // S5_CHUNK
    row = lambda w: pl.BlockSpec((tm, w), lambda i: (i, 0))
    tab = pl.BlockSpec((tm, LANES), lambda i: (i % nt, 0))
    hw = MLA_HEADS * HEAD_PAD
    bs = _s5_batch(bsz)
    u5_shape = (S5_TILES, nc, bs, S5_CHUNK, LANES)
    u5_spec = pl.BlockSpec((S5_TILES, tm // S5_CHUNK, 1, S5_CHUNK, LANES),
                           lambda i: (0, i % nt, i // nt, 0, 0))
    in_specs = [row(D_MODEL), _const_spec((1, D_MODEL)), _const_spec((D_MODEL, ZA_W)),
                _const_spec((FNET_W, 2 * FNET_W)), _const_spec((1, Q_LORA)),
                _const_spec((Q_LORA, hw)), _const_spec((1, KV_LORA)),
                _const_spec((KV_LORA, hw)), _const_spec((hw, KV_LORA)),
                tab, tab, tab, tab]
    args = [x, lw["g_mix"], lw["w_a"], tabs["cs"], lw["g_q"], lw["wq_a"], lw["g_kv"],
            lw["wk"], lw["wv_t"], tabs["taq"], tabs["tbq"], tabs["tak"], tabs["tbk"]]
    aliases = {}
    if bsz < bs:
        aliases = {len(args): 1}
        in_specs.append(pl.BlockSpec(memory_space=pl.ANY))
        args.append(jnp.zeros(u5_shape, F32))
    return pl.pallas_call(
        _inproj_kernel,
        grid=(n // tm,),
        in_specs=in_specs,
        out_specs=[pl.BlockSpec((tm // FNET_R2, FNET_PITCH, 2 * FNET_W), lambda i: (i, 0, 0)), u5_spec, row(hw), row(hw),
                   pl.BlockSpec((1, hw, tm), lambda i: (i // nt, 0, i % nt))],
        out_shape=[jax.ShapeDtypeStruct((n // FNET_R2, FNET_PITCH, 2 * FNET_W), F32), jax.ShapeDtypeStruct(u5_shape, F32),
                   jax.ShapeDtypeStruct((n, hw), BF16), jax.ShapeDtypeStruct((n, hw), BF16),
                   jax.ShapeDtypeStruct((bsz, hw, seq), BF16)],
        input_output_aliases=aliases,
        compiler_params=_params("parallel"),
        name="inproj",
    )(*args)


def _fnet_kernel(uc_ref, us_ref, f1_ref, f2_ref, twc_ref, tws_ref, out_ref, ar_ref, ai_ref, *, r1, p2):
    f1 = f1_ref[...]
    for n2 in range(FNET_R2):
        rows = pl.ds(n2, r1, stride=FNET_PITCH)
        u = jnp.concatenate([uc_ref[0, rows, :], us_ref[0, rows, :]], axis=1).astype(BF16)
        pr = _dot(f1, u)
        ar = pr[:r1, :LANES] - pr[r1:, LANES:]
        ai = -(pr[:r1, LANES:] + pr[r1:, :LANES])
        c, s = twc_ref[n2], tws_ref[n2]
        ar_ref[n2 * p2:n2 * p2 + r1, :] = ar * c + ai * s
        ai_ref[n2 * p2:n2 * p2 + r1, :] = ai * c - ar * s
    f2 = f2_ref[...]
    for k1 in range(r1):
        rows = pl.ds(k1, FNET_R2, stride=p2)
        gk = jnp.concatenate([ar_ref[rows, :], ai_ref[rows, :]], axis=0).astype(BF16)
        out_ref[0, rows, :] = _dot(f2, gk)
    for k2 in range(FNET_R2):
        out_ref[0, k2 * p2 + r1:(k2 + 1) * p2, :] = jnp.zeros((p2 - r1, LANES), F32)


def _fnet(ucs, tabs, bsz, seq):
    r1 = seq // FNET_R2
    p2 = r1 + SUBLANES
    nt = FNET_W // LANES
    padded = lambda rows, off: pl.BlockSpec((1, rows, LANES), lambda b, c: (b, 0, c + off))
    return pl.pallas_call(
        functools.partial(_fnet_kernel, r1=r1, p2=p2),
        grid=(bsz, nt),
        in_specs=[padded(r1 * FNET_PITCH, 0), padded(r1 * FNET_PITCH, nt), _const_spec((2 * r1, r1)),
                  _const_spec((FNET_R2, 2 * FNET_R2)),
                  _const_spec((FNET_R2, r1, LANES)), _const_spec((FNET_R2, r1, LANES))],
        out_specs=padded(FNET_R2 * p2, 0),
        out_shape=jax.ShapeDtypeStruct((bsz, FNET_R2 * p2, FNET_W), F32),
        scratch_shapes=[pltpu.VMEM((FNET_R2 * p2, LANES), F32)] * 2,
        compiler_params=_params("parallel", "parallel"),
        name="fnet",
    )(ucs, ucs, tabs["f1"], tabs["f2"], tabs["twc"], tabs["tws"])


def _s5_rows(u_ref, rows):
    return jnp.concatenate([u_ref[0, pl.ds(s, rows, stride=S5_CHUNK), :] for s in range(S5_CHUNK)],
                           axis=1).astype(BF16)


def _s5_in_kernel(u_ref, bst_ref, loc_ref, *, rows):
    loc_ref[0] = _dot(_s5_rows(u_ref, rows), bst_ref[0])


def _s5_batch(bsz):
    assert bsz <= SUBLANES
    return SUBLANES // 2 if bsz <= SUBLANES // 2 else SUBLANES


def _s5_scan_kernel(lf_ref, lb_ref, a_ref, spf_ref, spb_ref, carry_ref, *, tiles, bs):
    @pl.when(pl.program_id(1) == 0)
    def _():
        carry_ref[...] = jnp.zeros_like(carry_ref)

    afr, afi = a_ref[0, :, :S5_HALF], a_ref[0, :, S5_HALF:S5_DIR]
    abr, abi = a_ref[0, :, S5_DIR:S5_DIR + S5_HALF], a_ref[0, :, S5_DIR + S5_HALF:]

    low = lax.broadcasted_iota(jnp.int32, (SUBLANES, S5_HALF), 0) < SUBLANES // 2
    swap = lambda v: pltpu.roll(v, SUBLANES // 2, 0)

    def advance(ar, ai, xr, xi, lr, li):
        return ar * xr - ai * xi + lr, ar * xi + ai * xr + li

    def step(i, carry):
        fr, fi, br, bi = carry
        rf = pl.ds(pl.multiple_of(i * SUBLANES, SUBLANES), SUBLANES)
        rb = pl.ds(pl.multiple_of((tiles - 1 - i) * SUBLANES, SUBLANES), SUBLANES)
        lfr, lfi = lf_ref[0, rf, :S5_HALF], lf_ref[0, rf, S5_HALF:]
        lbr, lbi = lb_ref[0, rb, :S5_HALF], lb_ref[0, rb, S5_HALF:]
        nfr, nfi = advance(afr, afi, fr, fi, lfr, lfi)
        nbr, nbi = advance(abr, abi, br, bi, lbr, lbi)
        if bs == SUBLANES:
            spf_ref[0, rf, :S5_HALF], spf_ref[0, rf, S5_HALF:] = fr, fi
            spb_ref[0, rb, :S5_HALF], spb_ref[0, rb, S5_HALF:] = br, bi
            return nfr, nfi, nbr, nbi
        mfr, mfi = swap(nfr), swap(nfi)
        spf_ref[0, rf, :S5_HALF] = jnp.where(low, fr, mfr)
        spf_ref[0, rf, S5_HALF:] = jnp.where(low, fi, mfi)
        nfr, nfi = advance(afr, afi, mfr, mfi, lfr, lfi)
        mbr, mbi = swap(nbr), swap(nbi)
        spb_ref[0, rb, :S5_HALF] = jnp.where(low, mbr, br)
        spb_ref[0, rb, S5_HALF:] = jnp.where(low, mbi, bi)
        nbr, nbi = advance(abr, abi, mbr, mbi, lbr, lbi)
        return (jnp.where(low, swap(nfr), nfr), jnp.where(low, swap(nfi), nfi),
                jnp.where(low, nbr, swap(nbr)), jnp.where(low, nbi, swap(nbi)))

    carry = lax.fori_loop(0, tiles, step, tuple(carry_ref[k] for k in range(4)), unroll=2)
    for k in range(4):
        carry_ref[k] = carry[k]


def _s5_out_kernel(u_ref, m_ref, spf_ref, spb_ref, cst_ref, y_ref, *, rows):
    y = (_dot(_s5_rows(u_ref, rows), m_ref[0]) + _dot(spf_ref[0].astype(BF16), cst_ref[0, :S5_DIR, :])
         + _dot(spb_ref[0].astype(BF16), cst_ref[0, S5_DIR:, :]))
    for s in range(S5_CHUNK):
        y_ref[0, pl.ds(s, rows, stride=S5_CHUNK), :] = y[:, s * LANES:(s + 1) * LANES]


def _s5(u5, lw, seq):
    nc, bs = u5.shape[1], u5.shape[2]
    rows = min(S5_BLOCK_ROWS, nc * bs)
    nblk = nc * bs // rows
    kw = S5_CHUNK * LANES
    u2 = u5.reshape(S5_TILES, nc * bs * S5_CHUNK, LANES)
    u_spec = pl.BlockSpec((1, rows * S5_CHUNK, LANES), lambda t, j: (t, j, 0))
    wspec = lambda a, b: pl.BlockSpec((1, a, b), lambda t, j: (t, 0, 0))
    loc = pl.pallas_call(
        functools.partial(_s5_in_kernel, rows=rows),
        grid=(S5_TILES, nblk),
        in_specs=[u_spec, wspec(kw, 2 * S5_DIR)],
        out_specs=pl.BlockSpec((1, rows, 2 * S5_DIR), lambda t, j: (t, j, 0)),
        out_shape=jax.ShapeDtypeStruct((S5_TILES, nc * bs, 2 * S5_DIR), F32),
        compiler_params=_params("parallel", "parallel"),
        name="s5_in",
    )(u2, lw["s5_bst"])
    fwd = lambda lane: pl.BlockSpec((1, rows, S5_DIR), lambda t, j: (t, j, lane))
    bwd = lambda lane: pl.BlockSpec((1, rows, S5_DIR), lambda t, j: (t, nblk - 1 - j, lane))
    sp_shape = jax.ShapeDtypeStruct((S5_TILES, nc * bs, S5_DIR), F32)
    spf, spb = pl.pallas_call(
        functools.partial(_s5_scan_kernel, tiles=rows // SUBLANES, bs=bs),
        grid=(S5_TILES, nblk),
        in_specs=[fwd(0), bwd(1), wspec(SUBLANES, 2 * S5_DIR)],
        out_specs=[fwd(0), bwd(0)],
        out_shape=[sp_shape, sp_shape],
        scratch_shapes=[pltpu.VMEM((4, SUBLANES, S5_HALF), F32)],
        compiler_params=_params("parallel", "arbitrary"),
        name="s5_scan",
    )(loc, loc, lw["s5_a"])
    y2 = pl.pallas_call(
        functools.partial(_s5_out_kernel, rows=rows),
        grid=(S5_TILES, nblk),
        in_specs=[u_spec, wspec(kw, kw), fwd(0), fwd(0), wspec(2 * S5_DIR, kw)],
        out_specs=u_spec,
        out_shape=jax.ShapeDtypeStruct(u2.shape, F32),
        compiler_params=_params("parallel", "parallel"),
        name="s5_out",
    )(u2, lw["s5_m"], spf, spb, lw["s5_cst"])
    return y2.reshape(u5.shape)


def _attn_kernel(q_ref, k_ref, vt_ref, o_ref, s_ref, m_ref, p_ref, *, sub):
    nsub = q_ref.shape[1] // sub

    def scores(u, j, slot):
        sl = slice(HEAD_PAD * j, HEAD_PAD * (j + 1))
        q = q_ref[0, pl.ds(pl.multiple_of(u * sub, sub), sub), sl]
        st = lax.dot_general(k_ref[0, :, sl], q, (((1,), (1,)), ((), ())),
                             preferred_element_type=F32)
        s_ref[slot] = st
        m_ref[slot] = jnp.max(st, axis=0, keepdims=True)

    def finish(j, slot):
        p_ref[...] = jnp.exp2(s_ref[slot] - m_ref[slot]).astype(BF16)
        ol = _dot(vt_ref[0, HEAD_PAD * j:HEAD_PAD * (j + 1), :], p_ref[...])
        return ol[:V_DIM] / ol[V_DIM:]

    scores(0, 0, 0)

    def body(u, carry):
        scores(u, 1, 1)
        o0 = finish(0, 0)
        scores(jnp.minimum(u + 1, nsub - 1), 0, 0)
        o1 = finish(1, 1)
        rows = pl.ds(pl.multiple_of(u * sub, sub), sub)
        o_ref[0, rows, :] = jnp.concatenate([o0, o1], axis=0).T.astype(BF16)
        return carry

    lax.fori_loop(0, nsub, body, 0)


def _attention(q, k, vt, bsz, seq):
    sub = min(ATTN_SUB, seq)
    hw = MLA_HEADS * HEAD_PAD
    blk = lambda w: pl.BlockSpec((1, seq, w), lambda b, h: (b, 0, h))
    return pl.pallas_call(
        functools.partial(_attn_kernel, sub=sub),
        grid=(bsz, MLA_HEADS // 2),
        in_specs=[blk(2 * HEAD_PAD), blk(2 * HEAD_PAD),
                  pl.BlockSpec((1, 2 * HEAD_PAD, seq), lambda b, h: (b, h, 0))],
        out_specs=blk(2 * V_DIM),
        out_shape=jax.ShapeDtypeStruct((bsz, seq, MLA_HEADS * V_DIM), BF16),
        scratch_shapes=[pltpu.VMEM((2, seq, sub), F32), pltpu.VMEM((2, 1, sub), F32),
                        pltpu.VMEM((seq, sub), BF16)],
        compiler_params=_params("parallel", "parallel"),
        name="attention",
    )(q.reshape(bsz, seq, hw), k.reshape(bsz, seq, hw), vt)


def _merge_kernel(x_ref, yf_ref, y5_ref, o_ref, gmix_ref, wfn_ref, wglu_ref, ws5_ref, wo_ref,
                  wg_ref, wout_ref, out_ref):
    x = x_ref[...]
    h = _rms(x, gmix_ref[...]).astype(BF16)
    r1 = yf_ref.shape[1] - SUBLANES
    yf = jnp.concatenate([yf_ref[b, :r1, :] for b in range(yf_ref.shape[0])], axis=0)
    y_a = _dot(yf.astype(BF16), wfn_ref[...])
    nchunk = y5_ref.shape[1]
    ys = jnp.concatenate(
        [jnp.concatenate([y5_ref[t, c, 0] for c in range(nchunk)], axis=0) for t in range(S5_TILES)],
        axis=1)
    s = jax.nn.gelu(ys).astype(BF16)
    hg = _dot(s, wglu_ref[...])
    glu = (hg[:, :S5_W] * jax.nn.sigmoid(hg[:, S5_W:])).astype(BF16)
    y_b = _dot(glu, ws5_ref[...])
    y_c = _dot(o_ref[...], wo_ref[...])
    merged = jax.nn.sigmoid(_dot(h, wg_ref[:, :D_MODEL])) * y_a
    merged += jax.nn.sigmoid(_dot(h, wg_ref[:, D_MODEL:2 * D_MODEL])) * y_b
    merged += jax.nn.sigmoid(_dot(h, wg_ref[:, 2 * D_MODEL:])) * y_c
    out_ref[...] = x + _dot(merged.astype(BF16), wout_ref[...])


def _merge(x, yf, y5, o, lw, seq):
    n = x.shape[0]
    tm = min(1024, seq)
    nt = seq // tm
    row = lambda w: pl.BlockSpec((tm, w), lambda i: (i, 0))
    y5_spec = pl.BlockSpec((S5_TILES, tm // S5_CHUNK, 1, S5_CHUNK, LANES),
                           lambda i: (0, i % nt, i // nt, 0, 0))
    r1 = seq // FNET_R2
    yf = yf.reshape(-1, yf.shape[1] // FNET_R2, FNET_W)
    yf_spec = pl.BlockSpec((tm // r1, yf.shape[1], FNET_W), lambda i: (i, 0, 0))
    return pl.pallas_call(
        _merge_kernel,
        grid=(n // tm,),
        in_specs=[row(D_MODEL), yf_spec, y5_spec, row(MLA_HEADS * V_DIM),
                  _const_spec((1, D_MODEL)), _const_spec((FNET_W, D_MODEL)),
                  _const_spec((S5_W, 2 * S5_W)), _const_spec((S5_W, D_MODEL)),
                  _const_spec((MLA_HEADS * V_DIM, D_MODEL)),
                  _const_spec((D_MODEL, N_BRANCH * D_MODEL)), _const_spec((D_MODEL, D_MODEL))],
        out_specs=row(D_MODEL),
        out_shape=jax.ShapeDtypeStruct((n, D_MODEL), F32),
        compiler_params=_params("parallel"),
        name="merge",
    )(x, yf, y5, o, lw["g_mix"], lw["w_fnet"], lw["w_glu"], lw["w_s5"], lw["w_o"], lw["w_gate"],
      lw["w_out"])


def _mlp_kernel(x_ref, g_ref, wup_ref, wdn_ref, gfin_ref, out_ref, *, final):
    x = x_ref[...]
    h = _rms(x, g_ref[...]).astype(BF16)
    a = jnp.square(jnp.maximum(_dot(h, wup_ref[...]), 0.0)).astype(BF16)
    y = x + _dot(a, wdn_ref[...])
    if final:
        y = _rms(y, gfin_ref[...])
    out_ref[...] = y


def _mlp(x, lw, g_final, final):
    n = x.shape[0]
    tm = min(512, n)
    resident = lambda shape: pl.BlockSpec(shape, lambda i: (0, 0), pipeline_mode=pl.Buffered(1))
    return pl.pallas_call(
        functools.partial(_mlp_kernel, final=final),
        grid=(n // tm,),
        in_specs=[pl.BlockSpec((tm, D_MODEL), lambda i: (i, 0)), _const_spec((1, D_MODEL)),
                  resident((D_MODEL, D_FF)), resident((D_FF, D_MODEL)), _const_spec((1, D_MODEL))],
        out_specs=pl.BlockSpec((tm, D_MODEL), lambda i: (i, 0)),
        out_shape=jax.ShapeDtypeStruct((n, D_MODEL), F32),
        compiler_params=_params("parallel"),
        name="mlp",
    )(x, lw["g_mlp"], lw["w_up"], lw["w_down"], g_final)


def _tables(seq):
    half = QK_ROPE // 2
    inv = ROPE_BASE ** (-jnp.arange(half, dtype=F32) / half)
    ang = jnp.arange(seq, dtype=F32)[:, None] * inv[None, :]
    cos, sin = jnp.cos(ang), jnp.sin(ang)
    one = jnp.ones((seq, QK_NOPE), F32)
    z64 = jnp.zeros((seq, QK_NOPE), F32)
    z32 = jnp.zeros((seq, HEAD_PAD - QK_NOPE - QK_ROPE), F32)
    scale = (QK_NOPE + QK_ROPE) ** -0.5 * math.log2(math.e)
    ta = jnp.concatenate([z64, cos, cos, z32], axis=1)
    tb = jnp.concatenate([z64, -sin, sin, z32], axis=1)
    taq = jnp.concatenate([one, cos, cos, z32], axis=1) * scale
    c = jnp.arange(FNET_GROUP_DIM)
    ang64 = (2.0 * math.pi / FNET_GROUP_DIM) * ((c[:, None] * c[None, :]) % FNET_GROUP_DIM).astype(F32)
    eye = jnp.eye(FNET_GROUPS, dtype=F32)
    norm = 1.0 / math.sqrt(seq * FNET_GROUP_DIM)
    cs = jnp.concatenate([jnp.kron(eye, jnp.cos(ang64)), jnp.kron(eye, jnp.sin(ang64))], axis=1) * norm
    r1 = seq // FNET_R2
    i1 = jnp.arange(r1)
    a1 = (2.0 * math.pi / r1) * ((i1[:, None] * i1[None, :]) % r1).astype(F32)
    f1 = jnp.concatenate([jnp.cos(a1), jnp.sin(a1)], axis=0).astype(BF16)
    f2 = jnp.concatenate([jnp.cos(ang64), jnp.sin(ang64)], axis=1).astype(BF16)
    atw = (2.0 * math.pi / seq) * (c[:, None] * i1[None, :]).astype(F32)
    twc = jnp.broadcast_to(jnp.cos(atw)[:, :, None], (FNET_R2, r1, LANES))
    tws = jnp.broadcast_to(jnp.sin(atw)[:, :, None], (FNET_R2, r1, LANES))
    return dict(taq=taq, tbq=tb * scale, tak=ta, tbk=tb, cs=cs.astype(BF16), f1=f1, f2=f2, twc=twc, tws=tws)


def _pad_heads(w, width):
    k = w.shape[0]
    w = w.reshape(k, MLA_HEADS, width)
    return jnp.pad(w, ((0, 0), (0, 0), (0, HEAD_PAD - width))).reshape(k, MLA_HEADS * HEAD_PAD)


def _cmul(ar, ai, br, bi):
    return ar * br - ai * bi, ar * bi + ai * br


def _s5_tables(lam_re, lam_im, log_dt, b_re, b_im, c_re, c_im, d_skip):
    t, p, g, ns = S5_CHUNK, S5_GROUP_DIM, S5_GROUPS, S5_STATE
    nt, tg = S5_TILES, S5_TILE_GROUPS
    dt = jnp.exp(log_dt)[..., None]
    ar, ai = lam_re * dt, lam_im * dt
    mag = jnp.exp(ar)
    lbr, lbi = mag * jnp.cos(ai), mag * jnp.sin(ai)
    den = lam_re * lam_re + lam_im * lam_im
    cfr = ((lbr - 1.0) * lam_re + lbi * lam_im) / den
    cfi = (lbi * lam_re - (lbr - 1.0) * lam_im) / den
    bbr, bbi = _cmul(cfr[..., None], cfi[..., None], b_re, b_im)
    d = jnp.arange(t + 1, dtype=F32)
    pmag = jnp.exp(ar[..., None] * d)
    pwr, pwi = pmag * jnp.cos(ai[..., None] * d), pmag * jnp.sin(ai[..., None] * d)
    cpr, cpi = _cmul(c_re[..., None], c_im[..., None], pwr[:, :, None, :, :t], pwi[:, :, None, :, :t])
    kern = (jnp.einsum('xgpnd,xgnq->xgdpq', cpr, bbr)
            - jnp.einsum('xgpnd,xgnq->xgdpq', cpi, bbi))
    skip = jnp.eye(p, dtype=F32)[None] * d_skip.reshape(g, p)[:, :, None]
    k0 = kern[0][:, :1] + kern[1][:, :1] + skip[:, None]
    kfull = jnp.concatenate([kern[1][:, :0:-1], k0, kern[0][:, 1:]], axis=1)
    eye = jnp.eye(tg, dtype=F32)
    kq = kfull.transpose(0, 1, 3, 2).reshape(nt, tg, 2 * t - 1, p, p)
    bd = jnp.einsum('Ggdqp,gh->Gdgqhp', kq, eye).reshape(nt, 2 * t - 1, LANES, LANES)
    rev = lambda v: v[..., ::-1]
    lay_p = lambda v: v.reshape(2, nt, tg * ns, t).transpose(0, 1, 3, 2)
    psr = lay_p(jnp.stack([rev(pwr[0])[..., 1:], pwr[1][..., :t]]))
    psi = lay_p(jnp.stack([rev(pwi[0])[..., 1:], pwi[1][..., :t]]))
    lay_b = lambda v: jnp.einsum('xGhnq,gh->xGgqhn', v.reshape(2, nt, tg, ns, p), eye).reshape(
        2, nt, LANES, tg * ns)
    bmr, bmi = lay_b(bbr), lay_b(bbi)
    lay_c = lambda v: jnp.einsum('xGhpn,gh->xGgnhp', v.reshape(2, nt, tg, p, ns), eye).reshape(
        2, nt, tg * ns, LANES)
    cmr, cmi = lay_c(c_re), lay_c(c_im)
    por = jnp.stack([pwr[0][..., 1:], rev(pwr[1])[..., :t]]).reshape(2, nt, tg * ns, t)
    poi = jnp.stack([pwi[0][..., 1:], rev(pwi[1])[..., :t]]).reshape(2, nt, tg * ns, t)
    both = lambda v, ax: jnp.broadcast_to(jnp.expand_dims(v, ax), v.shape[:ax] + (2,) + v.shape[ax:])
    ps = both(jnp.stack([psr, psi]).transpose(2, 0, 3, 1, 4), 4).reshape(nt, 2, t, 2 * S5_DIR)
    bm = jnp.stack([jnp.stack([bmr, bmi], axis=1), jnp.stack([-bmi, bmr], axis=1)])
    bm = bm.transpose(3, 0, 4, 1, 2, 5).reshape(nt, 2, LANES, 2 * S5_DIR)
    cm = jnp.stack([jnp.stack([cmr, -cmi], axis=1), jnp.stack([-cmi, -cmr], axis=1)])
    cm = cm.transpose(3, 0, 1, 2, 4, 5).reshape(nt, 2, 2 * S5_DIR, LANES)
    po = both(jnp.stack([por, poi]).transpose(2, 0, 1, 3, 4), 3).reshape(nt, 2, 2 * S5_DIR, t)
    a = jnp.stack([pwr[..., t], pwi[..., t]], axis=1)
    a = a.reshape(2, 2, nt, tg, ns).transpose(2, 0, 1, 3, 4).reshape(nt, 1, 2 * S5_DIR)
    a = jnp.broadcast_to(a, (nt, SUBLANES, 2 * S5_DIR))
    m, bst, cst = _s5_expand(bd, ps, bm, cm, po)
    return dict(s5_m=m, s5_bst=bst, s5_cst=cst, s5_a=a.astype(F32))


def _s5_expand_kernel(bd_ref, ps_ref, bm_ref, cm_ref, po_ref, m_ref, bst_ref, cst_ref):
    t = S5_CHUNK
    for s in range(t):
        rows = slice(s * LANES, (s + 1) * LANES)
        bst_ref[0, rows, :] = (ps_ref[0, 0, s:s + 1, :] * bm_ref[0, 0]
                               + ps_ref[0, 1, s:s + 1, :] * bm_ref[0, 1]).astype(BF16)
        for c in range(t):
            m_ref[0, rows, c * LANES:(c + 1) * LANES] = bd_ref[0, c - s + t - 1].astype(BF16)
    for c in range(t):
        cst_ref[0, :, c * LANES:(c + 1) * LANES] = (cm_ref[0, 0] * po_ref[0, 0, :, c:c + 1]
                                                    + cm_ref[0, 1] * po_ref[0, 1, :, c:c + 1]).astype(BF16)


def _s5_expand(bd, ps, bm, cm, po):
    nt, t, kw = S5_TILES, S5_CHUNK, S5_CHUNK * LANES
    tile = lambda shape: pl.BlockSpec((1,) + shape, lambda i: (i,) + (0,) * len(shape))
    return pl.pallas_call(
        _s5_expand_kernel,
        grid=(nt,),
        in_specs=[tile((2 * t - 1, LANES, LANES)), tile((2, t, 2 * S5_DIR)), tile((2, LANES, 2 * S5_DIR)),
                  tile((2, 2 * S5_DIR, LANES)), tile((2, 2 * S5_DIR, t))],
        out_specs=[tile((kw, kw)), tile((kw, 2 * S5_DIR)), tile((2 * S5_DIR, kw))],
        out_shape=[jax.ShapeDtypeStruct((nt, kw, kw), BF16), jax.ShapeDtypeStruct((nt, kw, 2 * S5_DIR), BF16),
                   jax.ShapeDtypeStruct((nt, 2 * S5_DIR, kw), BF16)],
        compiler_params=_params("parallel"),
        name="s5_tables",
    )(bd, ps, bm, cm, po)


def _layer_weights(i, p):
    w_in = p["w_in"][i]
    k_dim = w_in.shape[0]
    half = QK_ROPE // 2
    kr = w_in[:, OFF_KR:OFF_GATE]
    z64 = jnp.zeros((k_dim, QK_NOPE), F32)
    z32 = jnp.zeros((k_dim, HEAD_PAD - QK_NOPE - QK_ROPE), F32)
    w_a = jnp.concatenate([w_in[:, :OFF_KR], z64, kr, z32, z64, kr[:, half:], kr[:, :half], z32], axis=1)
    wkv = p["w_kvb"][i].reshape(KV_LORA, MLA_HEADS, QK_NOPE + V_DIM)
    row = lambda v: v.reshape(1, -1).astype(F32)
    lw = dict(
        g_mix=row(p["g_mix"][i]), w_a=w_a.astype(BF16), w_gate=w_in[:, OFF_GATE:].astype(BF16),
        g_q=row(p["g_q"][i]), g_kv=row(p["g_kv"][i]),
        wq_a=_pad_heads(p["w_qb"][i], QK_NOPE + QK_ROPE).astype(BF16),
        wk=_pad_heads(wkv[:, :, :QK_NOPE].reshape(KV_LORA, -1), QK_NOPE).astype(BF16),
        wv_t=_pad_heads(wkv[:, :, QK_NOPE:].reshape(KV_LORA, -1), V_DIM).T.astype(BF16),
        w_fnet=p["w_fnet"][i].astype(BF16), w_glu=p["w_glu"][i].astype(BF16),
        w_s5=p["w_s5"][i].astype(BF16), w_o=p["w_o_mla"][i].astype(BF16),
        w_out=p["w_out"][i].astype(BF16), g_mlp=row(p["g_mlp"][i]),
        w_up=p["w_up"][i].astype(BF16), w_down=p["w_down"][i].astype(BF16))
    lw.update(_s5_tables(p["s5_lam_re"][i], p["s5_lam_im"][i], p["s5_log_dt"][i], p["s5_b_re"][i],
                         p["s5_b_im"][i], p["s5_c_re"][i], p["s5_c_im"][i], p["s5_d"][i]))
    return lw


def _trunk(x, layers, tabs, g_final):
    bsz, seq, _ = x.shape
    assert seq % S5_CHUNK == 0
    x = x.reshape(bsz * seq, D_MODEL)
    for i, lw in enumerate(layers):
        ucs, u5, q, k, vt = _inproj(x, lw, tabs, bsz, seq)
        yf = _fnet(ucs.reshape(bsz, -1, 2 * FNET_W), tabs, bsz, seq)
        y5 = _s5(u5, lw, seq)
        o = _attention(q, k, vt, bsz, seq).reshape(bsz * seq, MLA_HEADS * V_DIM)
        x = _merge(x, yf, y5, o, lw, seq)
        x = _mlp(x, lw, g_final, final=(i == len(layers) - 1))
    return x.reshape(bsz, seq, D_MODEL)


def kernel(x_prompt, x_sample, g_mix, w_in, w_fnet, s5_lam_re, s5_lam_im, s5_log_dt, s5_b_re,
           s5_b_im, s5_c_re, s5_c_im, s5_d, w_glu, w_s5, g_q, w_qb, g_kv, w_kvb, w_o_mla,
           w_out, g_mlp, w_up, w_down, g_final):
    p = dict(g_mix=g_mix, w_in=w_in, w_fnet=w_fnet, s5_lam_re=s5_lam_re, s5_lam_im=s5_lam_im,
             s5_log_dt=s5_log_dt, s5_b_re=s5_b_re, s5_b_im=s5_b_im, s5_c_re=s5_c_re,
             s5_c_im=s5_c_im, s5_d=s5_d, w_glu=w_glu, w_s5=w_s5, g_q=g_q, w_qb=w_qb, g_kv=g_kv,
             w_kvb=w_kvb, w_o_mla=w_o_mla, w_out=w_out, g_mlp=g_mlp, w_up=w_up, w_down=w_down)
    layers = [_layer_weights(i, p) for i in range(g_mix.shape[0])]
    gfin = g_final.reshape(1, -1).astype(F32)
    outs = []
    for x in (x_prompt, x_sample):
        tabs = _tables(x.shape[1])
        outs.append(_trunk(x, layers, tabs, gfin))
    return tuple(outs)
```

```python
import functools
import math

import jax
import jax.numpy as jnp
from jax import lax
from jax.experimental import pallas as pl
from jax.experimental.pallas import tpu as pltpu

F32 = jnp.float32
BF16 = jnp.bfloat16

D_MODEL = 1024
FNET_GROUP_DIM = 64
FNET_GROUPS = 6
FNET_W = 384
S5_GROUP_DIM = 16
S5_GROUPS = 24
S5_W = 384
S5_STATE = 64
MLA_HEADS = 16
QK_NOPE = 64
QK_ROPE = 32
V_DIM = 64
Q_LORA = 384
KV_LORA = 256
ROPE_BASE = 10000.0
N_BRANCH = 3
D_FF = 4 * D_MODEL
EPS = 1e-6

OFF_FNET = 0
OFF_S5 = OFF_FNET + FNET_W
OFF_Q = OFF_S5 + S5_W
OFF_KV = OFF_Q + Q_LORA
OFF_KR = OFF_KV + KV_LORA
OFF_GATE = OFF_KR + QK_ROPE

LANES = 128
SUBLANES = 8
HEAD_PAD = 128
VMEM_LIMIT = 56 * 1024 * 1024

S5_CHUNK = SUBLANES
S5_TILES = S5_W // LANES
S5_TILE_GROUPS = LANES // S5_GROUP_DIM
S5_HALF = S5_TILE_GROUPS * S5_STATE
S5_DIR = 2 * S5_HALF
S5_BLOCK_ROWS = 512
FNET_R2 = FNET_GROUP_DIM
FNET_PITCH = FNET_R2 + SUBLANES

ZA_CQ = OFF_Q
ZA_CKV = OFF_KV
ZA_KRA = OFF_KR
ZA_KRB = OFF_KR + LANES
ZA_W = ZA_KRB + LANES

ATTN_SUB = 512


def _rms(x, g):
    return x * lax.rsqrt(jnp.mean(x * x, axis=-1, keepdims=True) + EPS) * g


def _dot(a, b):
    return jnp.dot(a, b, preferred_element_type=F32)


def _params(*sem):
    return pltpu.CompilerParams(dimension_semantics=sem, vmem_limit_bytes=VMEM_LIMIT)


def _const_spec(shape):
    nd = len(shape)
    return pl.BlockSpec(shape, lambda *_: (0,) * nd, pipeline_mode=pl.Buffered(1))


def _inproj_kernel(x_ref, gmix_ref, wa_ref, cs_ref, gq_ref, wqa_ref, gkv_ref, wk_ref,
                   wv_ref, taq_ref, tbq_ref, tak_ref, tbk_ref, *rest):
    ucs_ref, u5_ref, q_ref, k_ref, v_ref = rest[-5:]
    h = _rms(x_ref[...], gmix_ref[...]).astype(BF16)
    z = _dot(h, wa_ref[...])
    ucs = _dot(z[:, OFF_FNET:OFF_S5].astype(BF16), cs_ref[...])
    for blk in range(ucs.shape[0] // FNET_R2):
        ucs_ref[blk, :FNET_R2, :] = ucs[blk * FNET_R2:(blk + 1) * FNET_R2]
        ucs_ref[blk, FNET_R2:, :] = jnp.zeros((FNET_PITCH - FNET_R2, 2 * FNET_W), F32)
    for t in range(S5_TILES):
        for c in range(z.shape[0] // S5_CHUNK):
            u5_ref[t, c, 0] = z[c * S5_CHUNK:(c + 1) * S5_CHUNK, OFF_S5 + t * LANES:OFF_S5 + (t + 1) * LANES]
    hq = _rms(z[:, ZA_CQ:ZA_CQ + Q_LORA], gq_ref[...]).astype(BF16)
    hkv = _rms(z[:, ZA_CKV:ZA_CKV + KV_LORA], gkv_ref[...]).astype(BF16)
    kpe = z[:, ZA_KRA:ZA_KRA + LANES] * tak_ref[...] + z[:, ZA_KRB:ZA_KRB + LANES] * tbk_ref[...]
    kpe2 = jnp.concatenate([kpe, kpe], axis=1)
    taq2 = jnp.concatenate([taq_ref[...]] * 2, axis=1)
    tbq2 = jnp.concatenate([tbq_ref[...]] * 2, axis=1)
    first_half = (lax.broadcasted_iota(jnp.int32, taq2.shape, 1) & (HEAD_PAD - 1)) < QK_NOPE + QK_ROPE // 2
    for j in range(MLA_HEADS // 2):
        sl = slice(2 * HEAD_PAD * j, 2 * HEAD_PAD * (j + 1))
        qa = _dot(hq, wqa_ref[:, sl])
        qb = jnp.where(first_half, pltpu.roll(qa, 2 * HEAD_PAD - QK_ROPE // 2, 1), pltpu.roll(qa, QK_ROPE // 2, 1))
        q_ref[:, sl] = (qa * taq2 + qb * tbq2).astype(BF16)
        k_ref[:, sl] = (_dot(hkv, wk_ref[:, sl]) + kpe2).astype(BF16)
    vt = lax.dot_general(wv_ref[...], hkv, (((1,), (1,)), ((), ())), preferred_element_type=F32)
    ones_row = (lax.broadcasted_iota(jnp.int32, vt.shape, 0) & V_DIM) != 0
    v_ref[0] = jnp.where(ones_row, 1.0, vt).astype(BF16)


def _inproj(x, lw, tabs, bsz, seq):
    n = x.shape[0]
    tm = min(512, seq)
    nt = seq // tm
    nc = seq // S5_CHUNK
    row = lambda w: pl.BlockSpec((tm, w), lambda i: (i, 0))
    tab = pl.BlockSpec((tm, LANES), lambda i: (i % nt, 0))
    hw = MLA_HEADS * HEAD_PAD
    bs = _s5_batch(bsz)
    u5_shape = (S5_TILES, nc, bs, S5_CHUNK, LANES)
    u5_spec = pl.BlockSpec((S5_TILES, tm // S5_CHUNK, 1, S5_CHUNK, LANES),
                           lambda i: (0, i % nt, i // nt, 0, 0))
    in_specs = [row(D_MODEL), _const_spec((1, D_MODEL)), _const_spec((D_MODEL, ZA_W)),
                _const_spec((FNET_W, 2 * FNET_W)), _const_spec((1, Q_LORA)),
                _const_spec((Q_LORA, hw)), _const_spec((1, KV_LORA)),
                _const_spec((KV_LORA, hw)), _const_spec((hw, KV_LORA)),
                tab, tab, tab, tab]
    args = [x, lw["g_mix"], lw["w_a"], tabs["cs"], lw["g_q"], lw["wq_a"], lw["g_kv"],
            lw["wk"], lw["wv_t"], tabs["taq"], tabs["tbq"], tabs["tak"], tabs["tbk"]]
    aliases = {}
    if bsz < bs:
        aliases = {len(args): 1}
        in_specs.append(pl.BlockSpec(memory_space=pl.ANY))
        args.append(jnp.zeros(u5_shape, F32))
    return pl.pallas_call(
        _inproj_kernel,
        grid=(n // tm,),
        in_specs=in_specs,
        out_specs=[pl.BlockSpec((tm // FNET_R2, FNET_PITCH, 2 * FNET_W), lambda i: (i, 0, 0)), u5_spec, row(hw), row(hw),
                   pl.BlockSpec((1, hw, tm), lambda i: (i // nt, 0, i % nt))],
        out_shape=[jax.ShapeDtypeStruct((n // FNET_R2, FNET_PITCH, 2 * FNET_W), F32), jax.ShapeDtypeStruct(u5_shape, F32),
                   jax.ShapeDtypeStruct((n, hw), BF16), jax.ShapeDtypeStruct((n, hw), BF16),
                   jax.ShapeDtypeStruct((bsz, hw, seq), BF16)],
        input_output_aliases=aliases,
        compiler_params=_params("parallel"),
        name="inproj",
    )(*args)


def _fnet_kernel(uc_ref, us_ref, f1_ref, f2_ref, twc_ref, tws_ref, out_ref, ar_ref, ai_ref, *, r1, p2):
    f1 = f1_ref[...]
    for n2 in range(FNET_R2):
        rows = pl.ds(n2, r1, stride=FNET_PITCH)
        u = jnp.concatenate([uc_ref[0, rows, :], us_ref[0, rows, :]], axis=1).astype(BF16)
        pr = _dot(f1, u)
        ar = pr[:r1, :LANES] - pr[r1:, LANES:]
        ai = -(pr[:r1, LANES:] + pr[r1:, :LANES])
        c, s = twc_ref[n2], tws_ref[n2]
        ar_ref[n2 * p2:n2 * p2 + r1, :] = ar * c + ai * s
        ai_ref[n2 * p2:n2 * p2 + r1, :] = ai * c - ar * s
    f2 = f2_ref[...]
    for k1 in range(r1):
        rows = pl.ds(k1, FNET_R2, stride=p2)
        gk = jnp.concatenate([ar_ref[rows, :], ai_ref[rows, :]], axis=0).astype(BF16)
        out_ref[0, rows, :] = _dot(f2, gk)
    for k2 in range(FNET_R2):
        out_ref[0, k2 * p2 + r1:(k2 + 1) * p2, :] = jnp.zeros((p2 - r1, LANES), F32)


def _fnet(ucs, tabs, bsz, seq):
    r1 = seq // FNET_R2
    p2 = r1 + SUBLANES
    nt = FNET_W // LANES
    padded = lambda rows, off: pl.BlockSpec((1, rows, LANES), lambda b, c: (b, 0, c + off))
    return pl.pallas_call(
        functools.partial(_fnet_kernel, r1=r1, p2=p2),
        grid=(bsz, nt),
        in_specs=[padded(r1 * FNET_PITCH, 0), padded(r1 * FNET_PITCH, nt), _const_spec((2 * r1, r1)),
                  _const_spec((FNET_R2, 2 * FNET_R2)),
                  _const_spec((FNET_R2, r1, LANES)), _const_spec((FNET_R2, r1, LANES))],
        out_specs=padded(FNET_R2 * p2, 0),
        out_shape=jax.ShapeDtypeStruct((bsz, FNET_R2 * p2, FNET_W), F32),
        scratch_shapes=[pltpu.VMEM((FNET_R2 * p2, LANES), F32)] * 2,
        compiler_params=_params("parallel", "parallel"),
        name="fnet",
    )(ucs, ucs, tabs["f1"], tabs["f2"], tabs["twc"], tabs["tws"])


def _s5_rows(u_ref, rows):
    return jnp.concatenate([u_ref[0, pl.ds(s, rows, stride=S5_CHUNK), :] for s in range(S5_CHUNK)],
                           axis=1).astype(BF16)


def _s5_batch(bsz):
    assert bsz <= SUBLANES
    return SUBLANES // 2 if bsz <= SUBLANES // 2 else SUBLANES


def _s5_scan_kernel(uf_ref, ub_ref, bf_ref, bb_ref, a_ref, spf_ref, spb_ref, carry_ref, lf_ref, lb_ref, *, tiles, bs):
    @pl.when(pl.program_id(1) == 0)
    def _():
        carry_ref[...] = jnp.zeros_like(carry_ref)

    lf_ref[0] = _dot(_s5_rows(uf_ref, tiles * SUBLANES), bf_ref[0])
    lb_ref[0] = _dot(_s5_rows(ub_ref, tiles * SUBLANES), bb_ref[0])
    afr, afi = a_ref[0, :, :S5_HALF], a_ref[0, :, S5_HALF:S5_DIR]
    abr, abi = a_ref[0, :, S5_DIR:S5_DIR + S5_HALF], a_ref[0, :, S5_DIR + S5_HALF:]

    low = lax.broadcasted_iota(jnp.int32, (SUBLANES, S5_HALF), 0) < SUBLANES // 2
    swap = lambda v: pltpu.roll(v, SUBLANES // 2, 0)

    def advance(ar, ai, xr, xi, lr, li):
        return ar * xr - ai * xi + lr, ar * xi + ai * xr + li

    def step(i, carry):
        fr, fi, br, bi = carry
        rf = pl.ds(pl.multiple_of(i * SUBLANES, SUBLANES), SUBLANES)
        rb = pl.ds(pl.multiple_of((tiles - 1 - i) * SUBLANES, SUBLANES), SUBLANES)
        lfr, lfi = lf_ref[0, rf, :S5_HALF], lf_ref[0, rf, S5_HALF:]
        lbr, lbi = lb_ref[0, rb, :S5_HALF], lb_ref[0, rb, S5_HALF:]
        nfr, nfi = advance(afr, afi, fr, fi, lfr, lfi)
        nbr, nbi = advance(abr, abi, br, bi, lbr, lbi)
        if bs == SUBLANES:
            spf_ref[0, rf, :S5_HALF], spf_ref[0, rf, S5_HALF:] = fr, fi
            spb_ref[0, rb, :S5_HALF], spb_ref[0, rb, S5_HALF:] = br, bi
            return nfr, nfi, nbr, nbi
        mfr, mfi = swap(nfr), swap(nfi)
        spf_ref[0, rf, :S5_HALF] = jnp.where(low, fr, mfr)
        spf_ref[0, rf, S5_HALF:] = jnp.where(low, fi, mfi)
        nfr, nfi = advance(afr, afi, mfr, mfi, lfr, lfi)
        mbr, mbi = swap(nbr), swap(nbi)
        spb_ref[0, rb, :S5_HALF] = jnp.where(low, mbr, br)
        spb_ref[0, rb, S5_HALF:] = jnp.where(low, mbi, bi)
        nbr, nbi = advance(abr, abi, mbr, mbi, lbr, lbi)
        return (jnp.where(low, swap(nfr), nfr), jnp.where(low, swap(nfi), nfi),
                jnp.where(low, nbr, swap(nbr)), jnp.where(low, nbi, swap(nbi)))

    carry = lax.fori_loop(0, tiles, step, tuple(carry_ref[k] for k in range(4)), unroll=2)
    for k in range(4):
        carry_ref[k] = carry[k]


def _s5_out_kernel(u_ref, m_ref, spf_ref, spb_ref, cst_ref, y_ref, *, rows):
    y = (_dot(_s5_rows(u_ref, rows), m_ref[0]) + _dot(spf_ref[0].astype(BF16), cst_ref[0, :S5_DIR, :])
         + _dot(spb_ref[0].astype(BF16), cst_ref[0, S5_DIR:, :]))
    for s in range(S5_CHUNK):
        y_ref[0, pl.ds(s, rows, stride=S5_CHUNK), :] = y[:, s * LANES:(s + 1) * LANES]


def _s5(u5, lw, seq):
    nc, bs = u5.shape[1], u5.shape[2]
    rows = min(S5_BLOCK_ROWS, nc * bs)
    nblk = nc * bs // rows
    kw = S5_CHUNK * LANES
    u2 = u5.reshape(S5_TILES, nc * bs * S5_CHUNK, LANES)
    u_spec = pl.BlockSpec((1, rows * S5_CHUNK, LANES), lambda t, j: (t, j, 0))
    u_mirror = pl.BlockSpec((1, rows * S5_CHUNK, LANES), lambda t, j: (t, nblk - 1 - j, 0))
    wspec = lambda a, b: pl.BlockSpec((1, a, b), lambda t, j: (t, 0, 0))
    bst_dir = lambda d: pl.BlockSpec((1, kw, S5_DIR), lambda t, j: (t, 0, d))
    fwd = lambda lane: pl.BlockSpec((1, rows, S5_DIR), lambda t, j: (t, j, lane))
    bwd = lambda lane: pl.BlockSpec((1, rows, S5_DIR), lambda t, j: (t, nblk - 1 - j, lane))
    sp_shape = jax.ShapeDtypeStruct((S5_TILES, nc * bs, S5_DIR), F32)
    spf, spb = pl.pallas_call(
        functools.partial(_s5_scan_kernel, tiles=rows // SUBLANES, bs=bs),
        grid=(S5_TILES, nblk),
        in_specs=[u_spec, u_mirror, bst_dir(0), bst_dir(1), wspec(SUBLANES, 2 * S5_DIR)],
        out_specs=[fwd(0), bwd(0)],
        out_shape=[sp_shape, sp_shape],
        scratch_shapes=[pltpu.VMEM((4, SUBLANES, S5_HALF), F32), pltpu.VMEM((1, rows, S5_DIR), F32),
                        pltpu.VMEM((1, rows, S5_DIR), F32)],
        compiler_params=_params("parallel", "arbitrary"),
        name="s5_scan",
    )(u2, u2, lw["s5_bst"], lw["s5_bst"], lw["s5_a"])
    y2 = pl.pallas_call(
        functools.partial(_s5_out_kernel, rows=rows),
        grid=(S5_TILES, nblk),
        in_specs=[u_spec, wspec(kw, kw), fwd(0), fwd(0), wspec(2 * S5_DIR, kw)],
        out_specs=u_spec,
        out_shape=jax.ShapeDtypeStruct(u2.shape, F32),
        compiler_params=_params("parallel", "parallel"),
        name="s5_out",
    )(u2, lw["s5_m"], spf, spb, lw["s5_cst"])
    return y2.reshape(u5.shape)


def _attn_kernel(q_ref, k_ref, vt_ref, o_ref, s_ref, m_ref, p_ref, *, sub):
    nsub = q_ref.shape[1] // sub

    def scores(u, j, slot):
        sl = slice(HEAD_PAD * j, HEAD_PAD * (j + 1))
        q = q_ref[0, pl.ds(pl.multiple_of(u * sub, sub), sub), sl]
        st = lax.dot_general(k_ref[0, :, sl], q, (((1,), (1,)), ((), ())),
                             preferred_element_type=F32)
        s_ref[slot] = st
        m_ref[slot] = jnp.max(st, axis=0, keepdims=True)

    def finish(j, slot):
        p_ref[...] = jnp.exp2(s_ref[slot] - m_ref[slot]).astype(BF16)
        ol = _dot(vt_ref[0, HEAD_PAD * j:HEAD_PAD * (j + 1), :], p_ref[...])
        return ol[:V_DIM] / ol[V_DIM:]

    scores(0, 0, 0)

    def body(u, carry):
        scores(u, 1, 1)
        o0 = finish(0, 0)
        scores(jnp.minimum(u + 1, nsub - 1), 0, 0)
        o1 = finish(1, 1)
        rows = pl.ds(pl.multiple_of(u * sub, sub), sub)
        o_ref[0, rows, :] = jnp.concatenate([o0, o1], axis=0).T.astype(BF16)
        return carry

    lax.fori_loop(0, nsub, body, 0)


def _attention(q, k, vt, bsz, seq):
    sub = min(ATTN_SUB, seq)
    hw = MLA_HEADS * HEAD_PAD
    blk = lambda w: pl.BlockSpec((1, seq, w), lambda b, h: (b, 0, h))
    return pl.pallas_call(
        functools.partial(_attn_kernel, sub=sub),
        grid=(bsz, MLA_HEADS // 2),
        in_specs=[blk(2 * HEAD_PAD), blk(2 * HEAD_PAD),
                  pl.BlockSpec((1, 2 * HEAD_PAD, seq), lambda b, h: (b, h, 0))],
        out_specs=blk(2 * V_DIM),
        out_shape=jax.ShapeDtypeStruct((bsz, seq, MLA_HEADS * V_DIM), BF16),
        scratch_shapes=[pltpu.VMEM((2, seq, sub), F32), pltpu.VMEM((2, 1, sub), F32),
                        pltpu.VMEM((seq, sub), BF16)],
        compiler_params=_params("parallel", "parallel"),
        name="attention",
    )(q.reshape(bsz, seq, hw), k.reshape(bsz, seq, hw), vt)


def _merge_kernel(x_ref, yf_ref, y5_ref, o_ref, gmix_ref, wfn_ref, wglu_ref, ws5_ref, wo_ref,
                  wg_ref, wout_ref, out_ref):
    x = x_ref[...]
    h = _rms(x, gmix_ref[...]).astype(BF16)
    r1 = yf_ref.shape[1] - SUBLANES
    yf = jnp.concatenate([yf_ref[b, :r1, :] for b in range(yf_ref.shape[0])], axis=0)
    y_a = _dot(yf.astype(BF16), wfn_ref[...])
    nchunk = y5_ref.shape[1]
    ys = jnp.concatenate(
        [jnp.concatenate([y5_ref[t, c, 0] for c in range(nchunk)], axis=0) for t in range(S5_TILES)],
        axis=1)
    s = jax.nn.gelu(ys).astype(BF16)
    hg = _dot(s, wglu_ref[...])
    glu = (hg[:, :S5_W] * jax.nn.sigmoid(hg[:, S5_W:])).astype(BF16)
    y_b = _dot(glu, ws5_ref[...])
    y_c = _dot(o_ref[...], wo_ref[...])
    merged = jax.nn.sigmoid(_dot(h, wg_ref[:, :D_MODEL])) * y_a
    merged += jax.nn.sigmoid(_dot(h, wg_ref[:, D_MODEL:2 * D_MODEL])) * y_b
    merged += jax.nn.sigmoid(_dot(h, wg_ref[:, 2 * D_MODEL:])) * y_c
    out_ref[...] = x + _dot(merged.astype(BF16), wout_ref[...])


def _merge(x, yf, y5, o, lw, seq):
    n = x.shape[0]
    tm = min(1024, seq)
    nt = seq // tm
    row = lambda w: pl.BlockSpec((tm, w), lambda i: (i, 0))
    y5_spec = pl.BlockSpec((S5_TILES, tm // S5_CHUNK, 1, S5_CHUNK, LANES),
                           lambda i: (0, i % nt, i // nt, 0, 0))
    r1 = seq // FNET_R2
    yf = yf.reshape(-1, yf.shape[1] // FNET_R2, FNET_W)
    yf_spec = pl.BlockSpec((tm // r1, yf.shape[1], FNET_W), lambda i: (i, 0, 0))
    return pl.pallas_call(
        _merge_kernel,
        grid=(n // tm,),
        in_specs=[row(D_MODEL), yf_spec, y5_spec, row(MLA_HEADS * V_DIM),
                  _const_spec((1, D_MODEL)), _const_spec((FNET_W, D_MODEL)),
                  _const_spec((S5_W, 2 * S5_W)), _const_spec((S5_W, D_MODEL)),
                  _const_spec((MLA_HEADS * V_DIM, D_MODEL)),
                  _const_spec((D_MODEL, N_BRANCH * D_MODEL)), _const_spec((D_MODEL, D_MODEL))],
        out_specs=row(D_MODEL),
        out_shape=jax.ShapeDtypeStruct((n, D_MODEL), F32),
        compiler_params=_params("parallel"),
        name="merge",
    )(x, yf, y5, o, lw["g_mix"], lw["w_fnet"], lw["w_glu"], lw["w_s5"], lw["w_o"], lw["w_gate"],
      lw["w_out"])


def _mlp_kernel(x_ref, g_ref, wup_ref, wdn_ref, gfin_ref, out_ref, *, final):
    x = x_ref[...]
    h = _rms(x, g_ref[...]).astype(BF16)
    a = jnp.square(jnp.maximum(_dot(h, wup_ref[...]), 0.0)).astype(BF16)
    y = x + _dot(a, wdn_ref[...])
    if final:
        y = _rms(y, gfin_ref[...])
    out_ref[...] = y


def _mlp(x, lw, g_final, final):
    n = x.shape[0]
    tm = min(512, n)
    resident = lambda shape: pl.BlockSpec(shape, lambda i: (0, 0), pipeline_mode=pl.Buffered(1))
    return pl.pallas_call(
        functools.partial(_mlp_kernel, final=final),
        grid=(n // tm,),
        in_specs=[pl.BlockSpec((tm, D_MODEL), lambda i: (i, 0)), _const_spec((1, D_MODEL)),
                  resident((D_MODEL, D_FF)), resident((D_FF, D_MODEL)), _const_spec((1, D_MODEL))],
        out_specs=pl.BlockSpec((tm, D_MODEL), lambda i: (i, 0)),
        out_shape=jax.ShapeDtypeStruct((n, D_MODEL), F32),
        compiler_params=_params("parallel"),
        name="mlp",
    )(x, lw["g_mlp"], lw["w_up"], lw["w_down"], g_final)


def _tables(seq):
    half = QK_ROPE // 2
    inv = ROPE_BASE ** (-jnp.arange(half, dtype=F32) / half)
    ang = jnp.arange(seq, dtype=F32)[:, None] * inv[None, :]
    cos, sin = jnp.cos(ang), jnp.sin(ang)
    one = jnp.ones((seq, QK_NOPE), F32)
    z64 = jnp.zeros((seq, QK_NOPE), F32)
    z32 = jnp.zeros((seq, HEAD_PAD - QK_NOPE - QK_ROPE), F32)
    scale = (QK_NOPE + QK_ROPE) ** -0.5 * math.log2(math.e)
    ta = jnp.concatenate([z64, cos, cos, z32], axis=1)
    tb = jnp.concatenate([z64, -sin, sin, z32], axis=1)
    taq = jnp.concatenate([one, cos, cos, z32], axis=1) * scale
    c = jnp.arange(FNET_GROUP_DIM)
    ang64 = (2.0 * math.pi / FNET_GROUP_DIM) * ((c[:, None] * c[None, :]) % FNET_GROUP_DIM).astype(F32)
    eye = jnp.eye(FNET_GROUPS, dtype=F32)
    norm = 1.0 / math.sqrt(seq * FNET_GROUP_DIM)
    cs = jnp.concatenate([jnp.kron(eye, jnp.cos(ang64)), jnp.kron(eye, jnp.sin(ang64))], axis=1) * norm
    r1 = seq // FNET_R2
    i1 = jnp.arange(r1)
    a1 = (2.0 * math.pi / r1) * ((i1[:, None] * i1[None, :]) % r1).astype(F32)
    f1 = jnp.concatenate([jnp.cos(a1), jnp.sin(a1)], axis=0).astype(BF16)
    f2 = jnp.concatenate([jnp.cos(ang64), jnp.sin(ang64)], axis=1).astype(BF16)
    atw = (2.0 * math.pi / seq) * (c[:, None] * i1[None, :]).astype(F32)
    twc = jnp.broadcast_to(jnp.cos(atw)[:, :, None], (FNET_R2, r1, LANES))
    tws = jnp.broadcast_to(jnp.sin(atw)[:, :, None], (FNET_R2, r1, LANES))
    return dict(taq=taq, tbq=tb * scale, tak=ta, tbk=tb, cs=cs.astype(BF16), f1=f1, f2=f2, twc=twc, tws=tws)


def _pad_heads(w, width):
    k = w.shape[0]
    w = w.reshape(k, MLA_HEADS, width)
    return jnp.pad(w, ((0, 0), (0, 0), (0, HEAD_PAD - width))).reshape(k, MLA_HEADS * HEAD_PAD)


def _cmul(ar, ai, br, bi):
    return ar * br - ai * bi, ar * bi + ai * br


def _s5_tables(lam_re, lam_im, log_dt, b_re, b_im, c_re, c_im, d_skip):
    t, p, g, ns = S5_CHUNK, S5_GROUP_DIM, S5_GROUPS, S5_STATE
    nt, tg = S5_TILES, S5_TILE_GROUPS
    dt = jnp.exp(log_dt)[..., None]
    ar, ai = lam_re * dt, lam_im * dt
    mag = jnp.exp(ar)
    lbr, lbi = mag * jnp.cos(ai), mag * jnp.sin(ai)
    den = lam_re * lam_re + lam_im * lam_im
    cfr = ((lbr - 1.0) * lam_re + lbi * lam_im) / den
    cfi = (lbi * lam_re - (lbr - 1.0) * lam_im) / den
    bbr, bbi = _cmul(cfr[..., None], cfi[..., None], b_re, b_im)
    d = jnp.arange(t + 1, dtype=F32)
    pmag = jnp.exp(ar[..., None] * d)
    pwr, pwi = pmag * jnp.cos(ai[..., None] * d), pmag * jnp.sin(ai[..., None] * d)
    cpr, cpi = _cmul(c_re[..., None], c_im[..., None], pwr[:, :, None, :, :t], pwi[:, :, None, :, :t])
    kern = (jnp.einsum('xgpnd,xgnq->xgdpq', cpr, bbr)
            - jnp.einsum('xgpnd,xgnq->xgdpq', cpi, bbi))
    skip = jnp.eye(p, dtype=F32)[None] * d_skip.reshape(g, p)[:, :, None]
    k0 = kern[0][:, :1] + kern[1][:, :1] + skip[:, None]
    kfull = jnp.concatenate([kern[1][:, :0:-1], k0, kern[0][:, 1:]], axis=1)
    eye = jnp.eye(tg, dtype=F32)
    kq = kfull.transpose(0, 1, 3, 2).reshape(nt, tg, 2 * t - 1, p, p)
    bd = jnp.einsum('Ggdqp,gh->Gdgqhp', kq, eye).reshape(nt, 2 * t - 1, LANES, LANES)
    rev = lambda v: v[..., ::-1]
    lay_p = lambda v: v.reshape(2, nt, tg * ns, t).transpose(0, 1, 3, 2)
    psr = lay_p(jnp.stack([rev(pwr[0])[..., 1:], pwr[1][..., :t]]))
    psi = lay_p(jnp.stack([rev(pwi[0])[..., 1:], pwi[1][..., :t]]))
    lay_b = lambda v: jnp.einsum('xGhnq,gh->xGgqhn', v.reshape(2, nt, tg, ns, p), eye).reshape(
        2, nt, LANES, tg * ns)
    bmr, bmi = lay_b(bbr), lay_b(bbi)
    lay_c = lambda v: jnp.einsum('xGhpn,gh->xGgnhp', v.reshape(2, nt, tg, p, ns), eye).reshape(
        2, nt, tg * ns, LANES)
    cmr, cmi = lay_c(c_re), lay_c(c_im)
    por = jnp.stack([pwr[0][..., 1:], rev(pwr[1])[..., :t]]).reshape(2, nt, tg * ns, t)
    poi = jnp.stack([pwi[0][..., 1:], rev(pwi[1])[..., :t]]).reshape(2, nt, tg * ns, t)
    both = lambda v, ax: jnp.broadcast_to(jnp.expand_dims(v, ax), v.shape[:ax] + (2,) + v.shape[ax:])
    ps = both(jnp.stack([psr, psi]).transpose(2, 0, 3, 1, 4), 4).reshape(nt, 2, t, 2 * S5_DIR)
    bm = jnp.stack([jnp.stack([bmr, bmi], axis=1), jnp.stack([-bmi, bmr], axis=1)])
    bm = bm.transpose(3, 0, 4, 1, 2, 5).reshape(nt, 2, LANES, 2 * S5_DIR)
    cm = jnp.stack([jnp.stack([cmr, -cmi], axis=1), jnp.stack([-cmi, -cmr], axis=1)])
    cm = cm.transpose(3, 0, 1, 2, 4, 5).reshape(nt, 2, 2 * S5_DIR, LANES)
    po = both(jnp.stack([por, poi]).transpose(2, 0, 1, 3, 4), 3).reshape(nt, 2, 2 * S5_DIR, t)
    a = jnp.stack([pwr[..., t], pwi[..., t]], axis=1)
    a = a.reshape(2, 2, nt, tg, ns).transpose(2, 0, 1, 3, 4).reshape(nt, 1, 2 * S5_DIR)
    a = jnp.broadcast_to(a, (nt, SUBLANES, 2 * S5_DIR))
    m, bst, cst = _s5_expand(bd, ps, bm, cm, po)
    return dict(s5_m=m, s5_bst=bst, s5_cst=cst, s5_a=a.astype(F32))


def _s5_expand_kernel(bd_ref, ps_ref, bm_ref, cm_ref, po_ref, m_ref, bst_ref, cst_ref):
    t = S5_CHUNK
    for s in range(t):
        rows = slice(s * LANES, (s + 1) * LANES)
        bst_ref[0, rows, :] = (ps_ref[0, 0, s:s + 1, :] * bm_ref[0, 0]
                               + ps_ref[0, 1, s:s + 1, :] * bm_ref[0, 1]).astype(BF16)
        for c in range(t):
            m_ref[0, rows, c * LANES:(c + 1) * LANES] = bd_ref[0, c - s + t - 1].astype(BF16)
    for c in range(t):
        cst_ref[0, :, c * LANES:(c + 1) * LANES] = (cm_ref[0, 0] * po_ref[0, 0, :, c:c + 1]
                                                    + cm_ref[0, 1] * po_ref[0, 1, :, c:c + 1]).astype(BF16)


def _s5_expand(bd, ps, bm, cm, po):
    nt, t, kw = S5_TILES, S5_CHUNK, S5_CHUNK * LANES
    tile = lambda shape: pl.BlockSpec((1,) + shape, lambda i: (i,) + (0,) * len(shape))
    return pl.pallas_call(
        _s5_expand_kernel,
        grid=(nt,),
        in_specs=[tile((2 * t - 1, LANES, LANES)), tile((2, t, 2 * S5_DIR)), tile((2, LANES, 2 * S5_DIR)),
                  tile((2, 2 * S5_DIR, LANES)), tile((2, 2 * S5_DIR, t))],
        out_specs=[tile((kw, kw)), tile((kw, 2 * S5_DIR)), tile((2 * S5_DIR, kw))],
        out_shape=[jax.ShapeDtypeStruct((nt, kw, kw), BF16), jax.ShapeDtypeStruct((nt, kw, 2 * S5_DIR), BF16),
                   jax.ShapeDtypeStruct((nt, 2 * S5_DIR, kw), BF16)],
        compiler_params=_params("parallel"),
        name="s5_tables",
    )(bd, ps, bm, cm, po)


def _layer_weights(i, p):
    w_in = p["w_in"][i]
    k_dim = w_in.shape[0]
    half = QK_ROPE // 2
    kr = w_in[:, OFF_KR:OFF_GATE]
    z64 = jnp.zeros((k_dim, QK_NOPE), F32)
    z32 = jnp.zeros((k_dim, HEAD_PAD - QK_NOPE - QK_ROPE), F32)
    w_a = jnp.concatenate([w_in[:, :OFF_KR], z64, kr, z32, z64, kr[:, half:], kr[:, :half], z32], axis=1)
    wkv = p["w_kvb"][i].reshape(KV_LORA, MLA_HEADS, QK_NOPE + V_DIM)
    row = lambda v: v.reshape(1, -1).astype(F32)
    lw = dict(
        g_mix=row(p["g_mix"][i]), w_a=w_a.astype(BF16), w_gate=w_in[:, OFF_GATE:].astype(BF16),
        g_q=row(p["g_q"][i]), g_kv=row(p["g_kv"][i]),
        wq_a=_pad_heads(p["w_qb"][i], QK_NOPE + QK_ROPE).astype(BF16),
        wk=_pad_heads(wkv[:, :, :QK_NOPE].reshape(KV_LORA, -1), QK_NOPE).astype(BF16),
        wv_t=_pad_heads(wkv[:, :, QK_NOPE:].reshape(KV_LORA, -1), V_DIM).T.astype(BF16),
        w_fnet=p["w_fnet"][i].astype(BF16), w_glu=p["w_glu"][i].astype(BF16),
        w_s5=p["w_s5"][i].astype(BF16), w_o=p["w_o_mla"][i].astype(BF16),
        w_out=p["w_out"][i].astype(BF16), g_mlp=row(p["g_mlp"][i]),
        w_up=p["w_up"][i].astype(BF16), w_down=p["w_down"][i].astype(BF16))
    lw.update(_s5_tables(p["s5_lam_re"][i], p["s5_lam_im"][i], p["s5_log_dt"][i], p["s5_b_re"][i],
                         p["s5_b_im"][i], p["s5_c_re"][i], p["s5_c_im"][i], p["s5_d"][i]))
    return lw


def _trunk(x, layers, tabs, g_final):
    bsz, seq, _ = x.shape
    assert seq % S5_CHUNK == 0
    x = x.reshape(bsz * seq, D_MODEL)
    for i, lw in enumerate(layers):
        ucs, u5, q, k, vt = _inproj(x, lw, tabs, bsz, seq)
        yf = _fnet(ucs.reshape(bsz, -1, 2 * FNET_W), tabs, bsz, seq)
        y5 = _s5(u5, lw, seq)
        o = _attention(q, k, vt, bsz, seq).reshape(bsz * seq, MLA_HEADS * V_DIM)
        x = _merge(x, yf, y5, o, lw, seq)
        x = _mlp(x, lw, g_final, final=(i == len(layers) - 1))
    return x.reshape(bsz, seq, D_MODEL)


def kernel(x_prompt, x_sample, g_mix, w_in, w_fnet, s5_lam_re, s5_lam_im, s5_log_dt, s5_b_re,
           s5_b_im, s5_c_re, s5_c_im, s5_d, w_glu, w_s5, g_q, w_qb, g_kv, w_kvb, w_o_mla,
           w_out, g_mlp, w_up, w_down, g_final):
    p = dict(g_mix=g_mix, w_in=w_in, w_fnet=w_fnet, s5_lam_re=s5_lam_re, s5_lam_im=s5_lam_im,
             s5_log_dt=s5_log_dt, s5_b_re=s5_b_re, s5_b_im=s5_b_im, s5_c_re=s5_c_re,
             s5_c_im=s5_c_im, s5_d=s5_d, w_glu=w_glu, w_s5=w_s5, g_q=g_q, w_qb=w_qb, g_kv=g_kv,
             w_kvb=w_kvb, w_o_mla=w_o_mla, w_out=w_out, g_mlp=g_mlp, w_up=w_up, w_down=w_down)
    layers = [_layer_weights(i, p) for i in range(g_mix.shape[0])]
    gfin = g_final.reshape(1, -1).astype(F32)
    outs = []
    for x in (x_prompt, x_sample):
        tabs = _tables(x.shape[1])
        outs.append(_trunk(x, layers, tabs, gfin))
    return tuple(outs)
```
